```python
import math
import jax, jax.numpy as jnp
from jax import lax
import numpy as np

D_MODEL = 1024
BATCH = 2
SEQ = 16384
DEPTH = 2

N_A_LAYERS = DEPTH // 2
N_B_LAYERS = DEPTH - N_A_LAYERS
MEM_TOKENS = 256
D_FF = 2816
HEAD_DIM = 64
MEM_HEADS = 4
MEM_WIDTH = MEM_HEADS * HEAD_DIM
MLSTM_HEADS = 4
MLSTM_HEAD_DIM = 192
MLSTM_WIDTH = MLSTM_HEADS * MLSTM_HEAD_DIM
MLSTM_CHUNK = 128
CONV_WIDTH = 4
DILATED_GROUPS = ((128, 1), (512, 4), (2048, 16))
HEADS_PER_GROUP = 4
N_DIL_HEADS = HEADS_PER_GROUP * len(DILATED_GROUPS)
DIL_WIDTH = N_DIL_HEADS * HEAD_DIM
A_IN_WIDTH = 4 * MLSTM_WIDTH + 2 * MLSTM_HEADS + MEM_WIDTH
B_IN_WIDTH = DIL_WIDTH + MEM_WIDTH
NUM_BUCKETS = 32
MAX_DISTANCE = 2048
RMS_EPS = 1e-6
NEG_INF = -1e30
ATTN_SCALE = HEAD_DIM ** -0.5

kernel_name = 'yoco_mlstm_dilated_macaron_hybrid'


def rms_norm(x, gain):
    xf = x.astype(jnp.float32)
    y = xf * lax.rsqrt(jnp.mean(xf * xf, axis=-1, keepdims=True) + RMS_EPS)
    return (y * gain.astype(jnp.float32)).astype(x.dtype)


def swiglu(h, w_gate, w_up, w_down):
    return (jax.nn.silu(h @ w_gate) * (h @ w_up)) @ w_down


def causal_conv(x, w):
    C = x.shape[-1]
    return lax.conv_general_dilated(x, w[:, None, :].astype(x.dtype), window_strides=(1,),
                                    padding=[(w.shape[0] - 1, 0)],
                                    dimension_numbers=('NWC', 'WIO', 'NWC'),
                                    feature_group_count=C)


def t5_causal_bucket(dist):
    max_exact = NUM_BUCKETS // 2
    d_f = jnp.maximum(dist, 1).astype(jnp.float32)
    large = max_exact + (jnp.log(d_f / max_exact) / math.log(MAX_DISTANCE / max_exact)
                         * (NUM_BUCKETS - max_exact)).astype(jnp.int32)
    large = jnp.minimum(large, NUM_BUCKETS - 1)
    return jnp.where(dist < max_exact, dist, large)


def mlstm_chunkwise(q, k, v, i_pre, f_pre):
    B, S, H, DK = q.shape
    DV = v.shape[-1]
    L = MLSTM_CHUNK
    NC = S // L
    f32 = jnp.float32

    def chunks(t):
        t = t.astype(f32).reshape(B, NC, L, H, *t.shape[3:])
        return jnp.moveaxis(t, 3, 2)

    qc = chunks(q)
    kc = chunks(k) * (DK ** -0.5)
    vc = chunks(v)
    ic = chunks(i_pre)
    logf = jax.nn.log_sigmoid(chunks(f_pre))
    b = jnp.cumsum(logf, axis=-1)
    b_last = b[..., -1]
    a = b_last[..., None] - b + ic

    def step(carry, xs):
        C, n, m = carry
        k_j, v_j, a_j, bl_j = xs
        m_new = jnp.maximum(bl_j + m, jnp.max(a_j, axis=-1))
        w = jnp.exp(a_j - m_new[..., None])
        decay = jnp.exp(bl_j + m - m_new)
        C_new = decay[..., None, None] * C + jnp.einsum('bhl,bhlk,bhlv->bhkv', w, k_j, v_j)
        n_new = decay[..., None] * n + jnp.einsum('bhl,bhlk->bhk', w, k_j)
        return (C_new, n_new, m_new), (C, n, m)

    init = (jnp.zeros((B, H, DK, DV), f32), jnp.zeros((B, H, DK), f32), jnp.zeros((B, H), f32))
    xs = (jnp.moveaxis(kc, 1, 0), jnp.moveaxis(vc, 1, 0), jnp.moveaxis(a, 1, 0), jnp.moveaxis(b_last, 1, 0))
    _, (C0, n0, m0) = lax.scan(step, init, xs)
    C0 = jnp.moveaxis(C0, 0, 1)
    n0 = jnp.moveaxis(n0, 0, 1)
    m0 = jnp.moveaxis(m0, 0, 1)

    causal = jnp.tril(jnp.ones((L, L), dtype=bool))
    Dlog = jnp.where(causal, b[..., :, None] - b[..., None, :] + ic[..., None, :], NEG_INF)
    g = b + m0[..., None]
    m_t = jnp.maximum(g, jnp.max(Dlog, axis=-1))
    W = jnp.exp(Dlog - m_t[..., None]) * jnp.einsum('bchtk,bchsk->bchts', qc, kc)
    inter = jnp.exp(g - m_t)
    num = jnp.einsum('bchts,bchsv->bchtv', W, vc) + inter[..., None] * jnp.einsum('bchtk,bchkv->bchtv', qc, C0)
    den = jnp.sum(W, axis=-1) + inter * jnp.einsum('bchtk,bchk->bcht', qc, n0)
    h = num / jnp.maximum(jnp.abs(den), jnp.exp(-m_t))[..., None]
    return jnp.moveaxis(h, 2, 3).reshape(B, S, H, DV)


def dilated_group_attention(q, k, v, bias_table, window, dilation):
    B, S, H, E = q.shape
    blk = window // dilation
    span = window
    Lp = -(-S // span) * span
    nb = Lp // span
    pad = ((0, 0), (0, Lp - S), (0, 0), (0, 0))

    def to_blocks(t):
        t = jnp.pad(t.astype(jnp.float32), pad).reshape(B, Lp // dilation, dilation, H, E)
        return jnp.transpose(t, (0, 2, 1, 3, 4)).reshape(B, dilation, nb, blk, H, E)

    qb = to_blocks(q)
    kb = to_blocks(k)
    vb = to_blocks(v)
    prev = lambda t: jnp.pad(t, ((0, 0), (0, 0), (1, 0), (0, 0), (0, 0), (0, 0)))[:, :, :-1]
    kk = jnp.concatenate([prev(kb), kb], axis=3)
    vv = jnp.concatenate([prev(vb), vb], axis=3)

    qi = jnp.arange(blk)[:, None]
    kj = jnp.arange(2 * blk)[None, :]
    m_off = qi + blk - kj
    band = (m_off >= 0) & (m_off <= blk)
    bucket = t5_causal_bucket(jnp.clip(m_off, 0, blk) * dilation)
    bias = jnp.transpose(bias_table.astype(jnp.float32)[bucket], (2, 0, 1))
    n_idx = jnp.arange(nb)[:, None, None]
    allowed = band[None] & ~((n_idx == 0) & (kj[None] < blk))

    s = jnp.einsum('brnqhe,brnkhe->brnhqk', qb, kk) * ATTN_SCALE + bias
    s = jnp.where(allowed[:, None], s, NEG_INF)
    mx = jnp.max(s, axis=-1, keepdims=True)
    p = jnp.exp(s - mx)
    den = jnp.sum(p, axis=-1)
    out = jnp.einsum('brnhqk,brnkhe->brnqhe', p, vv) / jnp.moveaxis(den, 3, 4)[..., None]
    lse = mx[..., 0] + jnp.log(den)

    out = jnp.transpose(out.reshape(B, dilation, Lp // dilation, H, E), (0, 2, 1, 3, 4)).reshape(B, Lp, H, E)[:, :S]
    lse = jnp.moveaxis(lse, 3, 4).reshape(B, dilation, Lp // dilation, H)
    lse = jnp.transpose(lse, (0, 2, 1, 3)).reshape(B, Lp, H)[:, :S]
    return out, lse


def memory_cross_attention(q_mem, mem_n, w_mem_kv, q_gain, k_gain):
    B, S, _ = q_mem.shape
    M = mem_n.shape[1]
    q = rms_norm(q_mem.reshape(B, S, MEM_HEADS, HEAD_DIM), q_gain)
    kv = mem_n @ w_mem_kv
    k = rms_norm(kv[..., :MEM_WIDTH].reshape(B, M, MEM_HEADS, HEAD_DIM), k_gain)
    v = kv[..., MEM_WIDTH:].reshape(B, M, MEM_HEADS, HEAD_DIM)
    s = jnp.einsum('bshe,bmhe->bhsm', q.astype(jnp.float32), k.astype(jnp.float32)) * ATTN_SCALE
    p = jax.nn.softmax(s, axis=-1)
    out = jnp.einsum('bhsm,bmhe->bshe', p, v.astype(jnp.float32))
    return out.reshape(B, S, MEM_WIDTH).astype(q_mem.dtype)


def mlstm_mixer(h, mem_n, w_in, conv_w, gate_bias, h_gain, w_out, w_mem_kv, mq_gain, mk_gain):
    B, S, _ = h.shape
    W = MLSTM_WIDTH
    p = h @ w_in
    qk = jax.nn.silu(causal_conv(p[..., :2 * W], conv_w))
    v = p[..., 2 * W:3 * W]
    o = p[..., 3 * W:4 * W]
    gates = p[..., 4 * W:4 * W + 2 * MLSTM_HEADS].astype(jnp.float32) + gate_bias.astype(jnp.float32)
    q_mem = p[..., 4 * W + 2 * MLSTM_HEADS:]
    heads = lambda t: t.reshape(B, S, MLSTM_HEADS, MLSTM_HEAD_DIM)
    cell = mlstm_chunkwise(heads(qk[..., :W]), heads(qk[..., W:]), heads(v),
                           gates[..., :MLSTM_HEADS], gates[..., MLSTM_HEADS:])
    cell = rms_norm(cell, h_gain).reshape(B, S, W)
    h_a = (jax.nn.sigmoid(o.astype(jnp.float32)) * cell).astype(h.dtype)
    h_m = memory_cross_attention(q_mem, mem_n, w_mem_kv, mq_gain, mk_gain)
    return jnp.concatenate([h_a, h_m], axis=-1) @ w_out


def dilated_mixer(h, mem_n, k_sh, v_sh, w_q, q_gain, w_out, rel_bias, w_mem_kv, mq_gain, mk_gain):
    B, S, _ = h.shape
    p = h @ w_q
    q_d = rms_norm(p[..., :DIL_WIDTH].reshape(B, S, len(DILATED_GROUPS), HEADS_PER_GROUP, HEAD_DIM), q_gain)
    outs, lses = [], []
    for g, (window, dilation) in enumerate(DILATED_GROUPS):
        table = rel_bias[:, g * HEADS_PER_GROUP:(g + 1) * HEADS_PER_GROUP]
        o_g, l_g = dilated_group_attention(q_d[:, :, g], k_sh[:, :, g], v_sh[:, :, g], table, window, dilation)
        outs.append(o_g)
        lses.append(l_g)
    outs = jnp.stack(outs, axis=2)
    alpha = jax.nn.softmax(jnp.stack(lses, axis=2), axis=2)
    h_d = (outs * alpha[..., None]).reshape(B, S, DIL_WIDTH).astype(h.dtype)
    h_m = memory_cross_attention(p[..., DIL_WIDTH:], mem_n, w_mem_kv, mq_gain, mk_gain)
    return jnp.concatenate([h_d, h_m], axis=-1) @ w_out


def setup_inputs(seed: int = 0) -> dict:
    key = jax.random.key(seed)
    ks = iter(jax.random.split(key, 40))
    nrm = lambda shape, scale: jax.random.normal(next(ks), shape, jnp.float32) * scale
    gain = lambda shape: 1.0 + nrm(shape, 0.02)
    D, F = D_MODEL, D_FF
    return {
        'x': nrm((BATCH, SEQ, D), 1.0),
        'mem': nrm((BATCH, MEM_TOKENS, D), 1.0),
        'ffn1_norm': gain((DEPTH, D)),
        'ffn1_w_gate': nrm((DEPTH, D, F), D ** -0.5),
        'ffn1_w_up': nrm((DEPTH, D, F), D ** -0.5),
        'ffn1_w_down': nrm((DEPTH, F, D), F ** -0.5),
        'ffn2_norm': gain((DEPTH, D)),
        'ffn2_w_gate': nrm((DEPTH, D, F), D ** -0.5),
        'ffn2_w_up': nrm((DEPTH, D, F), D ** -0.5),
        'ffn2_w_down': nrm((DEPTH, F, D), F ** -0.5),
        'mix_norm': gain((DEPTH, D)),
        'mem_norm': gain((DEPTH, D)),
        'w_mem_kv': nrm((DEPTH, D, 2 * MEM_WIDTH), D ** -0.5),
        'mem_q_norm': gain((DEPTH, HEAD_DIM)),
        'mem_k_norm': gain((DEPTH, HEAD_DIM)),
        'a_w_in': nrm((N_A_LAYERS, D, A_IN_WIDTH), D ** -0.5),
        'a_conv': nrm((N_A_LAYERS, CONV_WIDTH, 2 * MLSTM_WIDTH), CONV_WIDTH ** -0.5),
        'a_gate_bias': jnp.concatenate([nrm((N_A_LAYERS, MLSTM_HEADS), 0.1),
                                        3.0 + nrm((N_A_LAYERS, MLSTM_HEADS), 0.5)], axis=-1),
        'a_h_norm': gain((N_A_LAYERS, MLSTM_HEADS, MLSTM_HEAD_DIM)),
        'a_w_out': nrm((N_A_LAYERS, MLSTM_WIDTH + MEM_WIDTH, D), (MLSTM_WIDTH + MEM_WIDTH) ** -0.5),
        'b_w_q': nrm((N_B_LAYERS, D, B_IN_WIDTH), D ** -0.5),
        'b_q_norm': gain((N_B_LAYERS, HEAD_DIM)),
        'b_w_out': nrm((N_B_LAYERS, DIL_WIDTH + MEM_WIDTH, D), (DIL_WIDTH + MEM_WIDTH) ** -0.5),
        'kv_norm': gain((D,)),
        'w_kv': nrm((D, 2 * DIL_WIDTH), D ** -0.5),
        'kv_k_norm': gain((HEAD_DIM,)),
        'rel_bias': nrm((NUM_BUCKETS, N_DIL_HEADS), 0.3),
    }


def reference(x, mem, ffn1_norm, ffn1_w_gate, ffn1_w_up, ffn1_w_down,
              ffn2_norm, ffn2_w_gate, ffn2_w_up, ffn2_w_down,
              mix_norm, mem_norm, w_mem_kv, mem_q_norm, mem_k_norm,
              a_w_in, a_conv, a_gate_bias, a_h_norm, a_w_out,
              b_w_q, b_q_norm, b_w_out, kv_norm, w_kv, kv_k_norm, rel_bias):
    B, S, _ = x.shape
    n_groups = len(DILATED_GROUPS)
    k_sh = None
    v_sh = None
    for layer in range(DEPTH):
        x = x + 0.5 * swiglu(rms_norm(x, ffn1_norm[layer]), ffn1_w_gate[layer], ffn1_w_up[layer], ffn1_w_down[layer])
        h = rms_norm(x, mix_norm[layer])
        mem_n = rms_norm(mem, mem_norm[layer])
        if layer < N_A_LAYERS:
            ia = layer
            mix = mlstm_mixer(h, mem_n, a_w_in[ia], a_conv[ia], a_gate_bias[ia], a_h_norm[ia], a_w_out[ia],
                              w_mem_kv[layer], mem_q_norm[layer], mem_k_norm[layer])
        else:
            ib = layer - N_A_LAYERS
            mix = dilated_mixer(h, mem_n, k_sh, v_sh, b_w_q[ib], b_q_norm[ib], b_w_out[ib], rel_bias,
                                w_mem_kv[layer], mem_q_norm[layer], mem_k_norm[layer])
        x = x + mix
        x = x + 0.5 * swiglu(rms_norm(x, ffn2_norm[layer]), ffn2_w_gate[layer], ffn2_w_up[layer], ffn2_w_down[layer])
        if layer == N_A_LAYERS - 1:
            kv = rms_norm(x, kv_norm) @ w_kv
            k_sh = rms_norm(kv[..., :DIL_WIDTH].reshape(B, S, N_DIL_HEADS, HEAD_DIM), kv_k_norm)
            k_sh = k_sh.reshape(B, S, n_groups, HEADS_PER_GROUP, HEAD_DIM)
            v_sh = kv[..., DIL_WIDTH:].reshape(B, S, n_groups, HEADS_PER_GROUP, HEAD_DIM)
    return x
```

```python
import functools
import math

import numpy as np
import jax
import jax.numpy as jnp
from jax import lax
from jax.experimental import pallas as pl
from jax.experimental.pallas import tpu as pltpu

F32 = jnp.float32
BF16 = jnp.bfloat16

HEAD_DIM = 64
MEM_HEADS = 4
MEM_WIDTH = MEM_HEADS * HEAD_DIM
MLSTM_HEADS = 4
MLSTM_HEAD_DIM = 192
MLSTM_WIDTH = MLSTM_HEADS * MLSTM_HEAD_DIM
MLSTM_CHUNK = 128
DILATED_GROUPS = ((128, 1), (512, 4), (2048, 16))
HEADS_PER_GROUP = 4
GROUP_WIDTH = HEADS_PER_GROUP * HEAD_DIM
DIL_WIDTH = GROUP_WIDTH * len(DILATED_GROUPS)
NUM_BUCKETS = 32
MAX_DISTANCE = 2048
RMS_EPS = 1e-6
NEG_INF = -1e30
ATTN_SCALE = HEAD_DIM ** -0.5

V7X_LANES = 128
V7X_SUBLANES = 8
V7X_MXU_DIM = 256
V7X_VMEM_BYTES = 64 * 1024 * 1024
VMEM_LIMIT_BYTES = V7X_VMEM_BYTES - 8 * 1024 * 1024

HEAD_PAD = V7X_MXU_DIM
MLSTM_PAD_WIDTH = MLSTM_HEADS * HEAD_PAD
DEN_LANE = MLSTM_HEAD_DIM
ATTN_BLOCK = 128


def _compiler_params(n_axes):
    return pltpu.CompilerParams(dimension_semantics=("arbitrary",) * n_axes,
                                vmem_limit_bytes=VMEM_LIMIT_BYTES)


def _resident(shape):
    zeros = (0,) * len(shape)
    return pl.BlockSpec(shape, lambda *_: zeros, pipeline_mode=pl.Buffered(1))


def _dot(a, b):
    return jnp.dot(a, b, preferred_element_type=F32)


def _dot_nt(a, b):
    return lax.dot_general(a, b, (((1,), (1,)), ((), ())), preferred_element_type=F32)


def _dot_tn(a, b):
    return lax.dot_general(a, b, (((0,), (0,)), ((), ())), preferred_element_type=F32)


def _rms(x, gain):
    ms = jnp.mean(x * x, axis=-1, keepdims=True)
    return x * lax.rsqrt(ms + RMS_EPS) * gain


def _split_bf16(x, terms):
    parts = []
    rest = x
    for _ in range(terms):
        part = rest.astype(BF16)
        parts.append(part)
        rest = rest - part.astype(F32)
    return parts


def _head_mean_sq(x, head_avg):
    return sum(_dot(part, head_avg) for part in _split_bf16(x * x, 2))


def _head_rms(x, head_avg, gain):
    return x * lax.rsqrt(_head_mean_sq(x, head_avg) + RMS_EPS) * gain


def _silu(x):
    return x * jax.nn.sigmoid(x)


def _log_sigmoid(x):
    return jnp.minimum(x, 0.0) - jnp.log(1.0 + jnp.exp(-jnp.abs(x)))


def _head_lane_mask(head, width=GROUP_WIDTH):
    lane = lax.broadcasted_iota(jnp.int32, (1, width), 1)
    return (lane >= head * HEAD_DIM) & (lane < (head + 1) * HEAD_DIM)


def _memory_attention(qn, mem_k, mem_v):
    out = jnp.zeros(qn.shape, F32)
    for head in range(MEM_HEADS):
        mask = _head_lane_mask(head)
        s = _dot_nt(jnp.where(mask, qn, jnp.zeros_like(qn)), mem_k)
        mx = jnp.max(s, axis=-1, keepdims=True)
        e = jnp.exp(s - mx)
        den = jnp.sum(e, axis=-1, keepdims=True)
        pv = _dot(e.astype(BF16), mem_v)
        out = out + jnp.where(mask, pv / den, 0.0)
    return out


def _ffn_body(x_ref, gain_ref, wg_ref, wu_ref, wd_ref, o_ref, h_scr, acc_scr, *, f_chunk, n_chunks):
    x = x_ref[...]
    h_scr[...] = _rms(x, gain_ref[...]).astype(BF16)
    for c in range(n_chunks):
        cols = slice(c * f_chunk, (c + 1) * f_chunk)
        h = h_scr[...]
        act = (_silu(_dot(h, wg_ref[:, cols])) * _dot(h, wu_ref[:, cols])).astype(BF16)
        down = _dot(act, wd_ref[cols, :])
        if c == 0:
            acc_scr[...] = down
        else:
            acc_scr[...] += down
    o_ref[...] = x + 0.5 * acc_scr[...]


def _ffn(x2d, gain, w_gate, w_up, w_down, *, tm):
    n, d = x2d.shape
    f = w_gate.shape[1]
    f_chunk = V7X_MXU_DIM
    row_spec = pl.BlockSpec((tm, d), lambda i: (i, 0))
    return pl.pallas_call(
        functools.partial(_ffn_body, f_chunk=f_chunk, n_chunks=f // f_chunk),
        grid=(n // tm,),
        in_specs=[row_spec, _resident((1, d)), _resident((d, f)), _resident((d, f)), _resident((f, d))],
        out_specs=row_spec,
        out_shape=jax.ShapeDtypeStruct((n, d), F32),
        scratch_shapes=[pltpu.VMEM((tm, d), BF16), pltpu.VMEM((tm, d), F32)],
        compiler_params=_compiler_params(1),
        name="ffn",
    )(x2d, gain.reshape(1, d), w_gate.astype(BF16), w_up.astype(BF16), w_down.astype(BF16))


def _memkv_body(mem_ref, gain_ref, w_ref, kgain_ref, havg_ref, k_ref, v_ref):
    mem_n = _rms(mem_ref[0], gain_ref[...]).astype(BF16)
    kv = _dot(mem_n, w_ref[...])
    k_ref[0] = _head_rms(kv[:, :MEM_WIDTH], havg_ref[...], kgain_ref[...]).astype(BF16)
    v_ref[0] = kv[:, MEM_WIDTH:].astype(BF16)


def _memkv(mem, gain, w_mem_kv, k_gain, head_avg):
    b, m, d = mem.shape
    out_spec = pl.BlockSpec((1, m, MEM_WIDTH), lambda i: (i, 0, 0))
    return pl.pallas_call(
        _memkv_body,
        grid=(b,),
        in_specs=[pl.BlockSpec((1, m, d), lambda i: (i, 0, 0)), _resident((1, d)),
                  _resident((d, 2 * MEM_WIDTH)), _resident((1, MEM_WIDTH)),
                  _resident((MEM_WIDTH, MEM_WIDTH))],
        out_specs=[out_spec, out_spec],
        out_shape=[jax.ShapeDtypeStruct((b, m, MEM_WIDTH), BF16)] * 2,
        compiler_params=_compiler_params(1),
        name="memkv",
    )(mem, gain.reshape(1, d), w_mem_kv.astype(BF16), jnp.tile(k_gain, MEM_HEADS).reshape(1, MEM_WIDTH),
      head_avg)


_Q0, _K0, _V0, _O0 = (i * MLSTM_PAD_WIDTH for i in range(4))
_QM0 = 4 * MLSTM_PAD_WIDTH
_GI0 = _QM0 + MEM_WIDTH
_GF0 = _GI0 + V7X_LANES
A_IN_PAD_WIDTH = _GF0 + V7X_LANES
CONV_TAIL = V7X_SUBLANES


def _mlstm_body(x_ref, gain_ref, win_ref, conv_ref, bi_ref, bf_ref, hgain_ref, wout_ref,
                mk_ref, mv_ref, mqgain_ref, havg_ref, ltri_ref,
                o_ref, c_scr, m_scr, ext_scr, hcat_scr, *, ts, conv_width):
    @pl.when(pl.program_id(1) == 0)
    def _():
        c_scr[...] = jnp.zeros(c_scr.shape, F32)
        m_scr[...] = jnp.zeros(m_scr.shape, F32)
        ext_scr[0:CONV_TAIL, :] = jnp.zeros((CONV_TAIL, ext_scr.shape[1]), F32)

    x = x_ref[0]
    h = _rms(x, gain_ref[...]).astype(BF16)
    p = _dot(h, win_ref[...])

    qk_pre = p[:, _Q0:_V0]
    ext_scr[CONV_TAIL:CONV_TAIL + ts, :] = qk_pre
    cw = conv_ref[...]
    y = cw[conv_width - 1:conv_width, :] * qk_pre
    for back in range(1, conv_width):
        tap = conv_width - 1 - back
        y = y + cw[tap:tap + 1, :] * ext_scr[CONV_TAIL - back:CONV_TAIL - back + ts, :]
    ext_scr[0:CONV_TAIL, :] = qk_pre[ts - CONV_TAIL:ts, :]
    qk = _silu(y)
    q = qk[:, :MLSTM_PAD_WIDTH].astype(BF16)
    k = qk[:, MLSTM_PAD_WIDTH:] * (MLSTM_HEAD_DIM ** -0.5)
    v = p[:, _V0:_O0]
    lane = lax.broadcasted_iota(jnp.int32, (1, MLSTM_PAD_WIDTH), 1)
    v = jnp.where((lane & (HEAD_PAD - 1)) == DEN_LANE, 1.0, v).astype(BF16)
    o_gate = p[:, _O0:_QM0]
    gate_i = p[:, _GI0:_GF0] + bi_ref[...]
    log_f = _log_sigmoid(p[:, _GF0:A_IN_PAD_WIDTH] + bf_ref[...])

    L = MLSTM_CHUNK
    row = lax.broadcasted_iota(jnp.int32, (L, L), 0)
    col = lax.broadcasted_iota(jnp.int32, (L, L), 1)
    causal = col <= row
    head_lane = lax.broadcasted_iota(jnp.int32, (1, HEAD_PAD), 1)
    ltri = ltri_ref[...]
    for c in range(ts // L):
        rows = slice(c * L, (c + 1) * L)
        b = sum(_dot(ltri, part) for part in _split_bf16(log_f[rows], 3))
        b_last = b[L - 1:L, :]
        r = gate_i[rows] - b
        a = b_last + r
        m0 = m_scr[...]
        m_new = jnp.maximum(b_last + m0, jnp.max(a, axis=0, keepdims=True))
        w = jnp.exp(a - m_new)
        decay = jnp.exp(b_last + m0 - m_new)
        g = b + m0
        r_rows = r.T
        for hh in range(MLSTM_HEADS):
            hl = slice(hh * HEAD_PAD, (hh + 1) * HEAD_PAD)
            qh, kh, vh = q[rows, hl], k[rows, hl], v[rows, hl]
            d_log = jnp.where(causal, b[:, hh:hh + 1] + r_rows[hh:hh + 1, :], NEG_INF)
            g_col = g[:, hh:hh + 1]
            m_t = jnp.maximum(g_col, jnp.max(d_log, axis=-1, keepdims=True))
            wmat = jnp.exp(d_log - m_t) * _dot_nt(qh, kh.astype(BF16))
            c0 = c_scr[hh]
            num = _dot(wmat.astype(BF16), vh) + jnp.exp(g_col - m_t) * _dot(qh, c0.astype(BF16))
            den = num[:, DEN_LANE:DEN_LANE + 1]
            cell = num / jnp.maximum(jnp.abs(den), jnp.exp(-m_t))
            cell = jnp.where(head_lane < MLSTM_HEAD_DIM, cell, 0.0)
            ms = jnp.sum(cell * cell, axis=-1, keepdims=True) * (1.0 / MLSTM_HEAD_DIM)
            cell = cell * lax.rsqrt(ms + RMS_EPS) * hgain_ref[:, hl]
            hcat_scr[rows, hl] = (jax.nn.sigmoid(o_gate[rows, hl]) * cell).astype(BF16)
            kw = (kh * w[:, hh:hh + 1]).astype(BF16)
            c_scr[hh] = decay[:, hh:hh + 1] * c0 + _dot_tn(kw, vh)
        m_scr[...] = m_new

    qm = _head_rms(p[:, _QM0:_GI0], havg_ref[...], mqgain_ref[...] * ATTN_SCALE).astype(BF16)
    hcat_scr[:, MLSTM_PAD_WIDTH:] = _memory_attention(qm, mk_ref[0], mv_ref[0]).astype(BF16)
    o_ref[0] = x + _dot(hcat_scr[...], wout_ref[...])


def _pad_heads(w, axis):
    shape = w.shape
    w = w.reshape(shape[:axis] + (MLSTM_HEADS, MLSTM_HEAD_DIM) + shape[axis + 1:])
    pad = [(0, 0)] * w.ndim
    pad[axis + 1] = (0, HEAD_PAD - MLSTM_HEAD_DIM)
    return jnp.pad(w, pad).reshape(shape[:axis] + (MLSTM_PAD_WIDTH,) + shape[axis + 1:])


def _pad_lanes(w):
    return jnp.pad(w, [(0, 0)] * (w.ndim - 1) + [(0, V7X_LANES - w.shape[-1])])


def _mlstm_layer(x, gain, w_in, conv_w, gate_bias, h_gain, w_out, mem_k, mem_v, mq_gain,
                 head_avg, ltri, *, ts):
    b, s, d = x.shape
    W, H = MLSTM_WIDTH, MLSTM_HEADS
    sections = [_pad_heads(w_in[:, i * W:(i + 1) * W], 1) for i in range(4)]
    w_in_p = jnp.concatenate(sections + [w_in[:, 4 * W + 2 * H:], _pad_lanes(w_in[:, 4 * W:4 * W + H]),
                                         _pad_lanes(w_in[:, 4 * W + H:4 * W + 2 * H])], axis=1).astype(BF16)
    conv_p = jnp.concatenate([_pad_heads(conv_w[:, :W], 1), _pad_heads(conv_w[:, W:], 1)], axis=1)
    w_out_p = jnp.concatenate([_pad_heads(w_out[:W], 0), w_out[W:]], axis=0).astype(BF16)
    conv_width = conv_w.shape[0]
    tile = pl.BlockSpec((1, ts, d), lambda i, j: (i, j, 0))
    mem_spec = pl.BlockSpec((1,) + mem_k.shape[1:], lambda i, j: (i, 0, 0))
    return pl.pallas_call(
        functools.partial(_mlstm_body, ts=ts, conv_width=conv_width),
        grid=(b, s // ts),
        in_specs=[tile, _resident((1, d)), _resident(w_in_p.shape), _resident(conv_p.shape),
                  _resident((1, V7X_LANES)), _resident((1, V7X_LANES)), _resident((1, MLSTM_PAD_WIDTH)),
                  _resident(w_out_p.shape), mem_spec, mem_spec, _resident((1, MEM_WIDTH)),
                  _resident(head_avg.shape), _resident(ltri.shape)],
        out_specs=tile,
        out_shape=jax.ShapeDtypeStruct((b, s, d), F32),
        scratch_shapes=[pltpu.VMEM((H, HEAD_PAD, HEAD_PAD), F32), pltpu.VMEM((1, V7X_LANES), F32),
                        pltpu.VMEM((CONV_TAIL + ts, 2 * MLSTM_PAD_WIDTH), F32),
                        pltpu.VMEM((ts, MLSTM_PAD_WIDTH + MEM_WIDTH), BF16)],
        compiler_params=_compiler_params(2),
        name="mlstm_mixer",
    )(x, gain.reshape(1, d), w_in_p, conv_p, _pad_lanes(gate_bias[:H].reshape(1, H)),
      _pad_lanes(gate_bias[H:].reshape(1, H)), _pad_heads(h_gain.reshape(1, W), 1), w_out_p,
      mem_k, mem_v, jnp.tile(mq_gain, MEM_HEADS).reshape(1, MEM_WIDTH), head_avg, ltri)


def _proj_body(x_ref, gain_ref, w_ref, hgain_ref, havg_ref, *out_refs, normed_width):
    h = _rms(x_ref[...], gain_ref[...]).astype(BF16)
    y = _dot(h, w_ref[...])
    pieces = []
    for c in range(y.shape[1] // GROUP_WIDTH):
        cols = slice(c * GROUP_WIDTH, (c + 1) * GROUP_WIDTH)
        if c * GROUP_WIDTH < normed_width:
            pieces.append(_head_rms(y[:, cols], havg_ref[...], hgain_ref[:, cols]).astype(BF16))
        else:
            pieces.append(y[:, cols].astype(BF16))
    start = 0
    for ref in out_refs:
        n = ref.shape[1] // GROUP_WIDTH
        for c in range(n):
            ref[:, c * GROUP_WIDTH:(c + 1) * GROUP_WIDTH] = pieces[start + c]
        start += n


def _proj(x2d, gain, w, head_gain, head_avg, out_widths, normed_width, *, tm, name):
    n, d = x2d.shape
    width = w.shape[1]
    return pl.pallas_call(
        functools.partial(_proj_body, normed_width=normed_width),
        grid=(n // tm,),
        in_specs=[pl.BlockSpec((tm, d), lambda i: (i, 0)), _resident((1, d)), _resident((d, width)),
                  _resident((1, normed_width)), _resident(head_avg.shape)],
        out_specs=[pl.BlockSpec((tm, ow), lambda i: (i, 0)) for ow in out_widths],
        out_shape=[jax.ShapeDtypeStruct((n, ow), BF16) for ow in out_widths],
        compiler_params=_compiler_params(1),
        name=name,
    )(x2d, gain.reshape(1, d), w.astype(BF16), head_gain.reshape(1, normed_width), head_avg)


def _dilated_body(q_ref, kc_ref, kp_ref, vc_ref, vp_ref, bias_ref, o_ref, l_ref, kk_scr, vv_scr, *, nblk):
    A = ATTN_BLOCK
    no_prev_limit = jnp.where(pl.program_id(2) == 0, A, 0)
    kk_scr[0:A, :] = kp_ref[0]
    kk_scr[A:, :] = kc_ref[0]
    vv_scr[0:A, :] = vp_ref[0]
    vv_scr[A:, :] = vc_ref[0]
    key = lax.broadcasted_iota(jnp.int32, (1, 2 * A), 1)
    for i in range(nblk):
        rows = slice(i * A, (i + 1) * A)
        qi = q_ref[0, rows, :]
        kk = kk_scr[i * A:(i + 2) * A, :]
        vv = vv_scr[i * A:(i + 2) * A, :]
        out = jnp.zeros((A, GROUP_WIDTH), F32)
        lse = jnp.zeros((A, GROUP_WIDTH), F32)
        for hh in range(HEADS_PER_GROUP):
            mask = _head_lane_mask(hh)
            s = _dot_nt(jnp.where(mask, qi, jnp.zeros_like(qi)), kk) + bias_ref[hh]
            if i == 0:
                s = jnp.where(key < no_prev_limit, NEG_INF, s)
            mx = jnp.max(s, axis=-1, keepdims=True)
            e = jnp.exp(s - mx)
            den = jnp.sum(e, axis=-1, keepdims=True)
            pv = _dot(e.astype(BF16), vv)
            out = out + jnp.where(mask, pv / den, 0.0)
            lse = lse + jnp.where(mask, mx + jnp.log(den), 0.0)
        o_ref[0, rows, :] = out
        l_ref[0, rows, :] = lse


def _dilated_attention(q, k, v, bias, group, dilation, *, nblk):
    b, s, _ = q.shape
    n_groups = len(DILATED_GROUPS)
    sub = s // dilation
    qb = nblk * ATTN_BLOCK
    view = lambda t: t.reshape(b, sub, dilation * DIL_WIDTH)
    cur = pl.BlockSpec((1, qb, GROUP_WIDTH), lambda bi, r, n: (bi, n, r * n_groups + group))
    prev = pl.BlockSpec((1, ATTN_BLOCK, GROUP_WIDTH),
                        lambda bi, r, n: (bi, jnp.maximum(n * nblk - 1, 0), r * n_groups + group))
    out_spec = pl.BlockSpec((1, qb, GROUP_WIDTH), lambda bi, r, n: (bi, n, r))
    out_shape = jax.ShapeDtypeStruct((b, sub, dilation * GROUP_WIDTH), F32)
    out, lse = pl.pallas_call(
        functools.partial(_dilated_body, nblk=nblk),
        grid=(b, dilation, sub // qb),
        in_specs=[cur, cur, prev, cur, prev, _resident(bias.shape)],
        out_specs=[out_spec, out_spec],
        out_shape=[out_shape, out_shape],
        scratch_shapes=[pltpu.VMEM((qb + ATTN_BLOCK, GROUP_WIDTH), BF16)] * 2,
        compiler_params=_compiler_params(3),
        name=f"dilated_attn_g{group}",
    )(view(q), view(k), view(k), view(v), view(v), bias)
    return out.reshape(b, s, GROUP_WIDTH), lse.reshape(b, s, GROUP_WIDTH)


def _dilated_bias(rel_bias, group, dilation):
    A = ATTN_BLOCK
    m_off = np.arange(A)[:, None] + A - np.arange(2 * A)[None, :]
    band = (m_off >= 0) & (m_off <= A)
    dist = np.clip(m_off, 0, A) * dilation
    max_exact = NUM_BUCKETS // 2
    large = max_exact + (np.log(np.maximum(dist, 1).astype(np.float32) / max_exact)
                         / math.log(MAX_DISTANCE / max_exact) * (NUM_BUCKETS - max_exact)).astype(np.int32)
    bucket = np.where(dist < max_exact, dist, np.minimum(large, NUM_BUCKETS - 1))
    table = rel_bias[:, group * HEADS_PER_GROUP:(group + 1) * HEADS_PER_GROUP].astype(F32)
    bias = jnp.transpose(table[bucket], (2, 0, 1))
    return jnp.where(band[None], bias, NEG_INF)


def _mixout_body(x_ref, o0_ref, o1_ref, o2_ref, l0_ref, l1_ref, l2_ref, qm_ref, mk_ref, mv_ref, wout_ref,
                 out_ref, hcat_scr):
    lses = [l0_ref[0], l1_ref[0], l2_ref[0]]
    mx = jnp.maximum(jnp.maximum(lses[0], lses[1]), lses[2])
    es = [jnp.exp(l - mx) for l in lses]
    inv = 1.0 / (es[0] + es[1] + es[2])
    for g, o_ref in enumerate((o0_ref, o1_ref, o2_ref)):
        hcat_scr[:, g * GROUP_WIDTH:(g + 1) * GROUP_WIDTH] = (o_ref[0] * (es[g] * inv)).astype(BF16)
    hcat_scr[:, DIL_WIDTH:] = _memory_attention(qm_ref[0], mk_ref[0], mv_ref[0]).astype(BF16)
    out_ref[0] = x_ref[0] + _dot(hcat_scr[...], wout_ref[...])


def _mixout(x, outs, lses, q_mem, mem_k, mem_v, w_out, *, ts):
    b, s, d = x.shape
    tile = lambda w: pl.BlockSpec((1, ts, w), lambda i, j: (i, j, 0))
    mem_spec = pl.BlockSpec((1,) + mem_k.shape[1:], lambda i, j: (i, 0, 0))
    return pl.pallas_call(
        _mixout_body,
        grid=(b, s // ts),
        in_specs=[tile(d)] + [tile(GROUP_WIDTH)] * 7 + [mem_spec, mem_spec, _resident(w_out.shape)],
        out_specs=tile(d),
        out_shape=jax.ShapeDtypeStruct((b, s, d), F32),
        scratch_shapes=[pltpu.VMEM((ts, DIL_WIDTH + MEM_WIDTH), BF16)],
        compiler_params=_compiler_params(2),
        name="dilated_mixer_out",
    )(x, *outs, *lses, q_mem, mem_k, mem_v, w_out.astype(BF16))


def kernel(x, mem, ffn1_norm, ffn1_w_gate, ffn1_w_up, ffn1_w_down, ffn2_norm, ffn2_w_gate, ffn2_w_up, ffn2_w_down, mix_norm, mem_norm, w_mem_kv, mem_q_norm, mem_k_norm, a_w_in, a_conv, a_gate_bias, a_h_norm, a_w_out, b_w_q, b_q_norm, b_w_out, kv_norm, w_kv, kv_k_norm, rel_bias):
    b, s, d = x.shape
    n = b * s
    tm = 1024
    head_id = np.arange(GROUP_WIDTH) // HEAD_DIM
    head_avg = jnp.asarray((head_id[:, None] == head_id[None, :]) / HEAD_DIM, BF16)
    ltri = jnp.asarray(np.tril(np.ones((MLSTM_CHUNK, MLSTM_CHUNK))), BF16)
    flat = lambda t: t.reshape(n, t.shape[-1])
    ffn = lambda t, norm, wg, wu, wd, layer: _ffn(flat(t), norm[layer], wg[layer], wu[layer], wd[layer],
                                                   tm=tm).reshape(b, s, d)

    x = ffn(x, ffn1_norm, ffn1_w_gate, ffn1_w_up, ffn1_w_down, 0)
    mem_k, mem_v = _memkv(mem, mem_norm[0], w_mem_kv[0], mem_k_norm[0], head_avg)
    x = _mlstm_layer(x, mix_norm[0], a_w_in[0], a_conv[0], a_gate_bias[0], a_h_norm[0], a_w_out[0],
                     mem_k, mem_v, mem_q_norm[0], head_avg, ltri, ts=256)
    x = ffn(x, ffn2_norm, ffn2_w_gate, ffn2_w_up, ffn2_w_down, 0)

    k_sh, v_sh = _proj(flat(x), kv_norm, w_kv, jnp.tile(kv_k_norm, DIL_WIDTH // HEAD_DIM), head_avg,
                       (DIL_WIDTH, DIL_WIDTH), DIL_WIDTH, tm=tm, name="shared_kv_proj")

    x = ffn(x, ffn1_norm, ffn1_w_gate, ffn1_w_up, ffn1_w_down, 1)
    q_gain = jnp.concatenate([jnp.tile(b_q_norm[0], DIL_WIDTH // HEAD_DIM),
                              jnp.tile(mem_q_norm[1], MEM_HEADS)]) * ATTN_SCALE
    q_d, q_mem = _proj(flat(x), mix_norm[1], b_w_q[0], q_gain, head_avg, (DIL_WIDTH, MEM_WIDTH),
                       DIL_WIDTH + MEM_WIDTH, tm=tm, name="dilated_q_proj")
    mem_k, mem_v = _memkv(mem, mem_norm[1], w_mem_kv[1], mem_k_norm[1], head_avg)
    seq = lambda t: t.reshape(b, s, t.shape[-1])
    outs, lses = [], []
    for g, (window, dilation) in enumerate(DILATED_GROUPS):
        assert window // dilation == ATTN_BLOCK and s % window == 0
        o_g, l_g = _dilated_attention(seq(q_d), seq(k_sh), seq(v_sh), _dilated_bias(rel_bias, g, dilation),
                                      g, dilation, nblk=min(4, s // window))
        outs.append(o_g)
        lses.append(l_g)
    x = _mixout(x, outs, lses, seq(q_mem), mem_k, mem_v, b_w_out[0], ts=512)
    x = ffn(x, ffn2_norm, ffn2_w_gate, ffn2_w_up, ffn2_w_down, 1)
    return x
```

```python
import functools
import math

import numpy as np
import jax
import jax.numpy as jnp
from jax import lax
from jax.experimental import pallas as pl
from jax.experimental.pallas import tpu as pltpu

F32 = jnp.float32
BF16 = jnp.bfloat16

HEAD_DIM = 64
MEM_HEADS = 4
MEM_WIDTH = MEM_HEADS * HEAD_DIM
MLSTM_HEADS = 4
MLSTM_HEAD_DIM = 192
MLSTM_WIDTH = MLSTM_HEADS * MLSTM_HEAD_DIM
MLSTM_CHUNK = 128
DILATED_GROUPS = ((128, 1), (512, 4), (2048, 16))
HEADS_PER_GROUP = 4
GROUP_WIDTH = HEADS_PER_GROUP * HEAD_DIM
DIL_WIDTH = GROUP_WIDTH * len(DILATED_GROUPS)
NUM_BUCKETS = 32
MAX_DISTANCE = 2048
RMS_EPS = 1e-6
NEG_INF = -1e30
ATTN_SCALE = HEAD_DIM ** -0.5

V7X_LANES = 128
V7X_SUBLANES = 8
V7X_MXU_DIM = 256
V7X_VMEM_BYTES = 64 * 1024 * 1024
VMEM_LIMIT_BYTES = V7X_VMEM_BYTES - 8 * 1024 * 1024

HEAD_PAD = V7X_MXU_DIM
MLSTM_PAD_WIDTH = MLSTM_HEADS * HEAD_PAD
DEN_LANE = MLSTM_HEAD_DIM
ATTN_BLOCK = 128


def _compiler_params(n_axes):
    return pltpu.CompilerParams(dimension_semantics=("arbitrary",) * n_axes,
                                vmem_limit_bytes=VMEM_LIMIT_BYTES)


def _resident(shape):
    zeros = (0,) * len(shape)
    return pl.BlockSpec(shape, lambda *_: zeros, pipeline_mode=pl.Buffered(1))


def _dot(a, b):
    return jnp.dot(a, b, preferred_element_type=F32)


def _dot_nt(a, b):
    return lax.dot_general(a, b, (((1,), (1,)), ((), ())), preferred_element_type=F32)


def _dot_tn(a, b):
    return lax.dot_general(a, b, (((0,), (0,)), ((), ())), preferred_element_type=F32)


def _rms(x, gain):
    ms = jnp.mean(x * x, axis=-1, keepdims=True)
    return x * lax.rsqrt(ms + RMS_EPS) * gain


def _split_bf16(x, terms):
    parts = []
    rest = x
    for _ in range(terms):
        part = rest.astype(BF16)
        parts.append(part)
        rest = rest - part.astype(F32)
    return parts


def _head_mean_sq(x, head_avg):
    return sum(_dot(part, head_avg) for part in _split_bf16(x * x, 2))


def _head_rms(x, head_avg, gain):
    return x * lax.rsqrt(_head_mean_sq(x, head_avg) + RMS_EPS) * gain


def _silu(x):
    return x * jax.nn.sigmoid(x)


def _log_sigmoid(x):
    return jnp.minimum(x, 0.0) - jnp.log(1.0 + jnp.exp(-jnp.abs(x)))


def _head_lane_mask(head, width=GROUP_WIDTH):
    lane = lax.broadcasted_iota(jnp.int32, (1, width), 1)
    return (lane >= head * HEAD_DIM) & (lane < (head + 1) * HEAD_DIM)


def _memory_attention(qn, mem_k, mem_v):
    out, _ = _stacked_head_attention(qn, mem_k, mem_v, None, None)
    return out


def _stacked_head_attention(q, keys, values, bias, masked_below):
    t = q.shape[0]
    masks = [_head_lane_mask(h) for h in range(HEADS_PER_GROUP)]
    qs = jnp.concatenate([jnp.where(m, q, jnp.zeros_like(q)) for m in masks], axis=0)
    s = _dot_nt(qs, keys)
    if bias is not None:
        s = s + bias
    if masked_below is not None:
        key = lax.broadcasted_iota(jnp.int32, (1, keys.shape[0]), 1)
        s = jnp.where(key < masked_below, NEG_INF, s)
    mx = jnp.max(s, axis=-1, keepdims=True)
    e = jnp.exp(s - mx)
    den = jnp.sum(e, axis=-1, keepdims=True)
    o_all = _dot(e.astype(BF16), values) / den
    l_all = mx + jnp.log(den)
    out = jnp.zeros((t, q.shape[1]), F32)
    lse = jnp.zeros((t, q.shape[1]), F32)
    for h, m in enumerate(masks):
        out = out + jnp.where(m, o_all[h * t:(h + 1) * t], 0.0)
        lse = lse + jnp.where(m, l_all[h * t:(h + 1) * t], 0.0)
    return out, lse


def _ffn_body(x_ref, gain_ref, wg_ref, wu_ref, wd_ref, o_ref, h_scr, acc_scr, *, f_chunk, n_chunks):
    x = x_ref[...]
    h_scr[...] = _rms(x, gain_ref[...]).astype(BF16)
    for c in range(n_chunks):
        cols = slice(c * f_chunk, (c + 1) * f_chunk)
        h = h_scr[...]
        act = (_silu(_dot(h, wg_ref[:, cols])) * _dot(h, wu_ref[:, cols])).astype(BF16)
        down = _dot(act, wd_ref[cols, :])
        if c == 0:
            acc_scr[...] = down
        else:
            acc_scr[...] += down
    o_ref[...] = x + 0.5 * acc_scr[...]


def _ffn(x2d, gain, w_gate, w_up, w_down, *, tm):
    n, d = x2d.shape
    f = w_gate.shape[1]
    f_chunk = V7X_MXU_DIM
    row_spec = pl.BlockSpec((tm, d), lambda i: (i, 0))
    return pl.pallas_call(
        functools.partial(_ffn_body, f_chunk=f_chunk, n_chunks=f // f_chunk),
        grid=(n // tm,),
        in_specs=[row_spec, _resident((1, d)), _resident((d, f)), _resident((d, f)), _resident((f, d))],
        out_specs=row_spec,
        out_shape=jax.ShapeDtypeStruct((n, d), F32),
        scratch_shapes=[pltpu.VMEM((tm, d), BF16), pltpu.VMEM((tm, d), F32)],
        compiler_params=_compiler_params(1),
        name="ffn",
    )(x2d, gain.reshape(1, d), w_gate.astype(BF16), w_up.astype(BF16), w_down.astype(BF16))


def _memkv_body(mem_ref, gain_ref, w_ref, kgain_ref, havg_ref, k_ref, v_ref):
    mem_n = _rms(mem_ref[0], gain_ref[...]).astype(BF16)
    kv = _dot(mem_n, w_ref[...])
    k_ref[0] = _head_rms(kv[:, :MEM_WIDTH], havg_ref[...], kgain_ref[...]).astype(BF16)
    v_ref[0] = kv[:, MEM_WIDTH:].astype(BF16)


def _memkv(mem, gain, w_mem_kv, k_gain, head_avg):
    b, m, d = mem.shape
    out_spec = pl.BlockSpec((1, m, MEM_WIDTH), lambda i: (i, 0, 0))
    return pl.pallas_call(
        _memkv_body,
        grid=(b,),
        in_specs=[pl.BlockSpec((1, m, d), lambda i: (i, 0, 0)), _resident((1, d)),
                  _resident((d, 2 * MEM_WIDTH)), _resident((1, MEM_WIDTH)),
                  _resident((MEM_WIDTH, MEM_WIDTH))],
        out_specs=[out_spec, out_spec],
        out_shape=[jax.ShapeDtypeStruct((b, m, MEM_WIDTH), BF16)] * 2,
        compiler_params=_compiler_params(1),
        name="memkv",
    )(mem, gain.reshape(1, d), w_mem_kv.astype(BF16), jnp.tile(k_gain, MEM_HEADS).reshape(1, MEM_WIDTH),
      head_avg)


_Q0, _K0, _V0, _O0 = (i * MLSTM_PAD_WIDTH for i in range(4))
_QM0 = 4 * MLSTM_PAD_WIDTH
_GI0 = _QM0 + MEM_WIDTH
_GF0 = _GI0 + V7X_LANES
A_IN_PAD_WIDTH = _GF0 + V7X_LANES
CONV_TAIL = V7X_SUBLANES


def _mlstm_body(x_ref, gain_ref, win_ref, conv_ref, bi_ref, bf_ref, hgain_ref, wout_ref,
                mk_ref, mv_ref, mqgain_ref, havg_ref, ltri_ref,
                o_ref, c_scr, m_scr, ext_scr, hcat_scr, *, ts, conv_width):
    @pl.when(pl.program_id(1) == 0)
    def _():
        c_scr[...] = jnp.zeros(c_scr.shape, F32)
        m_scr[...] = jnp.zeros(m_scr.shape, F32)
        ext_scr[0:CONV_TAIL, :] = jnp.zeros((CONV_TAIL, ext_scr.shape[1]), F32)

    x = x_ref[0]
    h = _rms(x, gain_ref[...]).astype(BF16)
    p = _dot(h, win_ref[...])

    qk_pre = p[:, _Q0:_V0]
    ext_scr[CONV_TAIL:CONV_TAIL + ts, :] = qk_pre
    cw = conv_ref[...]
    y = cw[conv_width - 1:conv_width, :] * qk_pre
    for back in range(1, conv_width):
        tap = conv_width - 1 - back
        y = y + cw[tap:tap + 1, :] * ext_scr[CONV_TAIL - back:CONV_TAIL - back + ts, :]
    ext_scr[0:CONV_TAIL, :] = qk_pre[ts - CONV_TAIL:ts, :]
    qk = _silu(y)
    q = qk[:, :MLSTM_PAD_WIDTH].astype(BF16)
    k = qk[:, MLSTM_PAD_WIDTH:] * (MLSTM_HEAD_DIM ** -0.5)
    v = p[:, _V0:_O0]
    lane = lax.broadcasted_iota(jnp.int32, (1, MLSTM_PAD_WIDTH), 1)
    v = jnp.where((lane & (HEAD_PAD - 1)) == DEN_LANE, 1.0, v).astype(BF16)
    o_gate = p[:, _O0:_QM0]
    gate_i = p[:, _GI0:_GF0] + bi_ref[...]
    log_f = _log_sigmoid(p[:, _GF0:A_IN_PAD_WIDTH] + bf_ref[...])

    L = MLSTM_CHUNK
    row = lax.broadcasted_iota(jnp.int32, (L, L), 0)
    col = lax.broadcasted_iota(jnp.int32, (L, L), 1)
    causal = col <= row
    head_lane = lax.broadcasted_iota(jnp.int32, (1, HEAD_PAD), 1)
    ltri = ltri_ref[...]
    for c in range(ts // L):
        rows = slice(c * L, (c + 1) * L)
        b = sum(_dot(ltri, part) for part in _split_bf16(log_f[rows], 3))
        b_last = b[L - 1:L, :]
        r = gate_i[rows] - b
        a = b_last + r
        m0 = m_scr[...]
        m_new = jnp.maximum(b_last + m0, jnp.max(a, axis=0, keepdims=True))
        w = jnp.exp(a - m_new)
        decay = jnp.exp(b_last + m0 - m_new)
        g = b + m0
        r_rows = r.T
        for hh in range(MLSTM_HEADS):
            hl = slice(hh * HEAD_PAD, (hh + 1) * HEAD_PAD)
            qh, kh, vh = q[rows, hl], k[rows, hl], v[rows, hl]
            d_log = jnp.where(causal, b[:, hh:hh + 1] + r_rows[hh:hh + 1, :], NEG_INF)
            g_col = g[:, hh:hh + 1]
            m_t = jnp.maximum(g_col, jnp.max(d_log, axis=-1, keepdims=True))
            wmat = jnp.exp(d_log - m_t) * _dot_nt(qh, kh.astype(BF16))
            c0 = c_scr[hh]
            num = _dot(wmat.astype(BF16), vh) + jnp.exp(g_col - m_t) * _dot(qh, c0.astype(BF16))
            den = num[:, DEN_LANE:DEN_LANE + 1]
            cell = num / jnp.maximum(jnp.abs(den), jnp.exp(-m_t))
            cell = jnp.where(head_lane < MLSTM_HEAD_DIM, cell, 0.0)
            ms = jnp.sum(cell * cell, axis=-1, keepdims=True) * (1.0 / MLSTM_HEAD_DIM)
            cell = cell * lax.rsqrt(ms + RMS_EPS) * hgain_ref[:, hl]
            hcat_scr[rows, hl] = (jax.nn.sigmoid(o_gate[rows, hl]) * cell).astype(BF16)
            kw = (kh * w[:, hh:hh + 1]).astype(BF16)
            c_scr[hh] = decay[:, hh:hh + 1] * c0 + _dot_tn(kw, vh)
        m_scr[...] = m_new

    qm = _head_rms(p[:, _QM0:_GI0], havg_ref[...], mqgain_ref[...] * ATTN_SCALE).astype(BF16)
    hcat_scr[:, MLSTM_PAD_WIDTH:] = _memory_attention(qm, mk_ref[0], mv_ref[0]).astype(BF16)
    o_ref[0] = x + _dot(hcat_scr[...], wout_ref[...])


def _pad_heads(w, axis):
    shape = w.shape
    w = w.reshape(shape[:axis] + (MLSTM_HEADS, MLSTM_HEAD_DIM) + shape[axis + 1:])
    pad = [(0, 0)] * w.ndim
    pad[axis + 1] = (0, HEAD_PAD - MLSTM_HEAD_DIM)
    return jnp.pad(w, pad).reshape(shape[:axis] + (MLSTM_PAD_WIDTH,) + shape[axis + 1:])


def _pad_lanes(w):
    return jnp.pad(w, [(0, 0)] * (w.ndim - 1) + [(0, V7X_LANES - w.shape[-1])])


def _mlstm_layer(x, gain, w_in, conv_w, gate_bias, h_gain, w_out, mem_k, mem_v, mq_gain,
                 head_avg, ltri, *, ts):
    b, s, d = x.shape
    W, H = MLSTM_WIDTH, MLSTM_HEADS
    sections = [_pad_heads(w_in[:, i * W:(i + 1) * W], 1) for i in range(4)]
    w_in_p = jnp.concatenate(sections + [w_in[:, 4 * W + 2 * H:], _pad_lanes(w_in[:, 4 * W:4 * W + H]),
                                         _pad_lanes(w_in[:, 4 * W + H:4 * W + 2 * H])], axis=1).astype(BF16)
    conv_p = jnp.concatenate([_pad_heads(conv_w[:, :W], 1), _pad_heads(conv_w[:, W:], 1)], axis=1)
    w_out_p = jnp.concatenate([_pad_heads(w_out[:W], 0), w_out[W:]], axis=0).astype(BF16)
    conv_width = conv_w.shape[0]
    tile = pl.BlockSpec((1, ts, d), lambda i, j: (i, j, 0))
    mem_spec = pl.BlockSpec((1,) + mem_k.shape[1:], lambda i, j: (i, 0, 0))
    return pl.pallas_call(
        functools.partial(_mlstm_body, ts=ts, conv_width=conv_width),
        grid=(b, s // ts),
        in_specs=[tile, _resident((1, d)), _resident(w_in_p.shape), _resident(conv_p.shape),
                  _resident((1, V7X_LANES)), _resident((1, V7X_LANES)), _resident((1, MLSTM_PAD_WIDTH)),
                  _resident(w_out_p.shape), mem_spec, mem_spec, _resident((1, MEM_WIDTH)),
                  _resident(head_avg.shape), _resident(ltri.shape)],
        out_specs=tile,
        out_shape=jax.ShapeDtypeStruct((b, s, d), F32),
        scratch_shapes=[pltpu.VMEM((H, HEAD_PAD, HEAD_PAD), F32), pltpu.VMEM((1, V7X_LANES), F32),
                        pltpu.VMEM((CONV_TAIL + ts, 2 * MLSTM_PAD_WIDTH), F32),
                        pltpu.VMEM((ts, MLSTM_PAD_WIDTH + MEM_WIDTH), BF16)],
        compiler_params=_compiler_params(2),
        name="mlstm_mixer",
    )(x, gain.reshape(1, d), w_in_p, conv_p, _pad_lanes(gate_bias[:H].reshape(1, H)),
      _pad_lanes(gate_bias[H:].reshape(1, H)), _pad_heads(h_gain.reshape(1, W), 1), w_out_p,
      mem_k, mem_v, jnp.tile(mq_gain, MEM_HEADS).reshape(1, MEM_WIDTH), head_avg, ltri)


def _proj_body(x_ref, gain_ref, w_ref, hgain_ref, havg_ref, *out_refs, normed_width):
    h = _rms(x_ref[0], gain_ref[...]).astype(BF16)
    y = _dot(h, w_ref[...])
    pieces = []
    for c in range(y.shape[1] // GROUP_WIDTH):
        cols = slice(c * GROUP_WIDTH, (c + 1) * GROUP_WIDTH)
        if c * GROUP_WIDTH < normed_width:
            pieces.append(_head_rms(y[:, cols], havg_ref[...], hgain_ref[:, cols]))
        else:
            pieces.append(y[:, cols])
    start = 0
    for ref in out_refs:
        if len(ref.shape) == 4:
            n = ref.shape[1] * V7X_LANES // GROUP_WIDTH
            for c in range(n):
                for half in range(GROUP_WIDTH // V7X_LANES):
                    lanes = slice(half * V7X_LANES, (half + 1) * V7X_LANES)
                    ref[0, c * (GROUP_WIDTH // V7X_LANES) + half] = pieces[start + c][:, lanes]
        else:
            n = ref.shape[2] // GROUP_WIDTH
            for c in range(n):
                ref[0, :, c * GROUP_WIDTH:(c + 1) * GROUP_WIDTH] = pieces[start + c].astype(BF16)
        start += n


def _proj(x, gain, w, head_gain, head_avg, outputs, normed_width, *, tm, name):
    b, s, d = x.shape
    width = w.shape[1]
    out_specs, out_shapes = [], []
    for ow, slab_major in outputs:
        if slab_major:
            out_specs.append(pl.BlockSpec((1, ow // V7X_LANES, tm, V7X_LANES), lambda i, j: (i, 0, j, 0)))
            out_shapes.append(jax.ShapeDtypeStruct((b, ow // V7X_LANES, s, V7X_LANES), F32))
        else:
            out_specs.append(pl.BlockSpec((1, tm, ow), lambda i, j: (i, j, 0)))
            out_shapes.append(jax.ShapeDtypeStruct((b, s, ow), BF16))
    return pl.pallas_call(
        functools.partial(_proj_body, normed_width=normed_width),
        grid=(b, s // tm),
        in_specs=[pl.BlockSpec((1, tm, d), lambda i, j: (i, j, 0)), _resident((1, d)), _resident((d, width)),
                  _resident((1, normed_width)), _resident(head_avg.shape)],
        out_specs=out_specs,
        out_shape=out_shapes,
        compiler_params=_compiler_params(2),
        name=name,
    )(x, gain.reshape(1, d), w.astype(BF16), head_gain.reshape(1, normed_width), head_avg)


SLABS = GROUP_WIDTH // V7X_LANES


def _dilated_body(q_ref, kc_ref, kp_ref, vc_ref, vp_ref, bias_ref, o_ref, l_ref, *, dilation, nblk):
    A = ATTN_BLOCK
    no_prev_limit = jnp.where(pl.program_id(1) == 0, A, 0)
    bias = bias_ref[0]

    def rows_of(start):
        return pl.ds(start, A, stride=dilation) if dilation > 1 else pl.ds(start, A)

    def load(ref, start):
        rows = rows_of(start)
        return jnp.concatenate([ref[0, sl, rows, :] for sl in range(SLABS)], axis=-1).astype(BF16)

    for r in range(dilation):
        k_cur = load(kp_ref, r)
        v_cur = load(vp_ref, r)
        for i in range(nblk):
            start = i * A * dilation + r
            k_prev, v_prev = k_cur, v_cur
            k_cur, v_cur = load(kc_ref, start), load(vc_ref, start)
            out, lse = _stacked_head_attention(
                load(q_ref, start), jnp.concatenate([k_prev, k_cur], axis=0),
                jnp.concatenate([v_prev, v_cur], axis=0), bias, no_prev_limit if i == 0 else None)
            for sl in range(SLABS):
                lanes = slice(sl * V7X_LANES, (sl + 1) * V7X_LANES)
                o_ref[0, sl, rows_of(start), :] = out[:, lanes]
                l_ref[0, sl, rows_of(start), :] = lse[:, lanes]


def _dilated_attention(q, k, v, bias, group, dilation, *, tile):
    b, _, s, _ = q.shape
    span = ATTN_BLOCK * dilation
    nblk = tile // span
    cur = pl.BlockSpec((1, SLABS, tile, V7X_LANES), lambda bi, n: (bi, group, n, 0))
    prev = pl.BlockSpec((1, SLABS, span, V7X_LANES),
                        lambda bi, n: (bi, group, jnp.maximum(n * nblk - 1, 0), 0))
    out_spec = pl.BlockSpec((1, SLABS, tile, V7X_LANES), lambda bi, n: (bi, 0, n, 0))
    out_shape = jax.ShapeDtypeStruct((b, SLABS, s, V7X_LANES), F32)
    return pl.pallas_call(
        functools.partial(_dilated_body, dilation=dilation, nblk=nblk),
        grid=(b, s // tile),
        in_specs=[cur, cur, prev, cur, prev,
                  pl.BlockSpec((1,) + bias.shape[1:], lambda bi, n: (group, 0, 0), pipeline_mode=pl.Buffered(1))],
        out_specs=[out_spec, out_spec],
        out_shape=[out_shape, out_shape],
        compiler_params=_compiler_params(2),
        name=f"dilated_attn_g{group}",
    )(q, k, k, v, v, bias)


def _bias_body(table_ref, bucket_ref, o_ref):
    g = pl.program_id(0)
    bucket = bucket_ref[0]
    for hh in range(HEADS_PER_GROUP):
        acc = jnp.full(bucket.shape, NEG_INF, F32)
        for bkt in range(NUM_BUCKETS):
            acc = jnp.where(bucket == bkt, table_ref[g * HEADS_PER_GROUP + hh, bkt], acc)
        o_ref[0, hh * ATTN_BLOCK:(hh + 1) * ATTN_BLOCK, :] = acc


def _dilated_bias(rel_bias):
    A = ATTN_BLOCK
    m_off = np.arange(A)[:, None] + A - np.arange(2 * A)[None, :]
    band = (m_off >= 0) & (m_off <= A)
    max_exact = NUM_BUCKETS // 2
    buckets = []
    for _, dilation in DILATED_GROUPS:
        dist = np.clip(m_off, 0, A) * dilation
        large = max_exact + (np.log(np.maximum(dist, 1).astype(np.float32) / max_exact)
                             / math.log(MAX_DISTANCE / max_exact) * (NUM_BUCKETS - max_exact)).astype(np.int32)
        bucket = np.where(dist < max_exact, dist, np.minimum(large, NUM_BUCKETS - 1))
        buckets.append(np.where(band, bucket, -1))
    buckets = jnp.asarray(np.stack(buckets), jnp.int32)
    n_groups = len(DILATED_GROUPS)
    return pl.pallas_call(
        _bias_body,
        grid=(n_groups,),
        in_specs=[pl.BlockSpec(memory_space=pltpu.SMEM), pl.BlockSpec((1, A, 2 * A), lambda g: (g, 0, 0))],
        out_specs=pl.BlockSpec((1, HEADS_PER_GROUP * A, 2 * A), lambda g: (g, 0, 0)),
        out_shape=jax.ShapeDtypeStruct((n_groups, HEADS_PER_GROUP * A, 2 * A), F32),
        compiler_params=_compiler_params(1),
        name="dilated_bias",
    )(rel_bias.astype(F32).T, buckets)


def _mixout_body(x_ref, o0_ref, o1_ref, o2_ref, l0_ref, l1_ref, l2_ref, qm_ref, mk_ref, mv_ref, wout_ref,
                 out_ref, hcat_scr):
    wide = lambda ref: jnp.concatenate([ref[0, sl] for sl in range(SLABS)], axis=-1)
    lses = [wide(l0_ref), wide(l1_ref), wide(l2_ref)]
    mx = jnp.maximum(jnp.maximum(lses[0], lses[1]), lses[2])
    es = [jnp.exp(l - mx) for l in lses]
    inv = 1.0 / (es[0] + es[1] + es[2])
    for g, o_ref in enumerate((o0_ref, o1_ref, o2_ref)):
        hcat_scr[:, g * GROUP_WIDTH:(g + 1) * GROUP_WIDTH] = (wide(o_ref) * (es[g] * inv)).astype(BF16)
    hcat_scr[:, DIL_WIDTH:] = _memory_attention(qm_ref[0], mk_ref[0], mv_ref[0]).astype(BF16)
    out_ref[0] = x_ref[0] + _dot(hcat_scr[...], wout_ref[...])


def _mixout(x, outs, lses, q_mem, mem_k, mem_v, w_out, *, ts):
    b, s, d = x.shape
    tile = lambda w: pl.BlockSpec((1, ts, w), lambda i, j: (i, j, 0))
    slab_tile = pl.BlockSpec((1, SLABS, ts, V7X_LANES), lambda i, j: (i, 0, j, 0))
    mem_spec = pl.BlockSpec((1,) + mem_k.shape[1:], lambda i, j: (i, 0, 0))
    return pl.pallas_call(
        _mixout_body,
        grid=(b, s // ts),
        in_specs=[tile(d)] + [slab_tile] * 6 + [tile(MEM_WIDTH), mem_spec, mem_spec, _resident(w_out.shape)],
        out_specs=tile(d),
        out_shape=jax.ShapeDtypeStruct((b, s, d), F32),
        scratch_shapes=[pltpu.VMEM((ts, DIL_WIDTH + MEM_WIDTH), BF16)],
        compiler_params=_compiler_params(2),
        name="dilated_mixer_out",
    )(x, *outs, *lses, q_mem, mem_k, mem_v, w_out.astype(BF16))


def kernel(x, mem, ffn1_norm, ffn1_w_gate, ffn1_w_up, ffn1_w_down, ffn2_norm, ffn2_w_gate, ffn2_w_up, ffn2_w_down, mix_norm, mem_norm, w_mem_kv, mem_q_norm, mem_k_norm, a_w_in, a_conv, a_gate_bias, a_h_norm, a_w_out, b_w_q, b_q_norm, b_w_out, kv_norm, w_kv, kv_k_norm, rel_bias):
    b, s, d = x.shape
    n = b * s
    tm = 1024
    head_id = np.arange(GROUP_WIDTH) // HEAD_DIM
    head_avg = jnp.asarray((head_id[:, None] == head_id[None, :]) / HEAD_DIM, BF16)
    ltri = jnp.asarray(np.tril(np.ones((MLSTM_CHUNK, MLSTM_CHUNK))), BF16)
    flat = lambda t: t.reshape(n, t.shape[-1])
    ffn = lambda t, norm, wg, wu, wd, layer: _ffn(flat(t), norm[layer], wg[layer], wu[layer], wd[layer],
                                                   tm=tm).reshape(b, s, d)

    x = ffn(x, ffn1_norm, ffn1_w_gate, ffn1_w_up, ffn1_w_down, 0)
    mem_k, mem_v = _memkv(mem, mem_norm[0], w_mem_kv[0], mem_k_norm[0], head_avg)
    x = _mlstm_layer(x, mix_norm[0], a_w_in[0], a_conv[0], a_gate_bias[0], a_h_norm[0], a_w_out[0],
                     mem_k, mem_v, mem_q_norm[0], head_avg, ltri, ts=256)
    x = ffn(x, ffn2_norm, ffn2_w_gate, ffn2_w_up, ffn2_w_down, 0)

    k_sh, v_sh = _proj(x, kv_norm, w_kv, jnp.tile(kv_k_norm, DIL_WIDTH // HEAD_DIM), head_avg,
                       [(DIL_WIDTH, True), (DIL_WIDTH, True)], DIL_WIDTH, tm=tm, name="shared_kv_proj")

    x = ffn(x, ffn1_norm, ffn1_w_gate, ffn1_w_up, ffn1_w_down, 1)
    q_gain = jnp.concatenate([jnp.tile(b_q_norm[0], DIL_WIDTH // HEAD_DIM),
                              jnp.tile(mem_q_norm[1], MEM_HEADS)]) * ATTN_SCALE
    q_d, q_mem = _proj(x, mix_norm[1], b_w_q[0], q_gain, head_avg, [(DIL_WIDTH, True), (MEM_WIDTH, False)],
                       DIL_WIDTH + MEM_WIDTH, tm=tm, name="dilated_q_proj")
    mem_k, mem_v = _memkv(mem, mem_norm[1], w_mem_kv[1], mem_k_norm[1], head_avg)
    bias = _dilated_bias(rel_bias)
    attn_tile = max(window for window, _ in DILATED_GROUPS)
    outs, lses = [], []
    for g, (window, dilation) in enumerate(DILATED_GROUPS):
        assert window // dilation == ATTN_BLOCK and s % attn_tile == 0 and attn_tile % window == 0
        o_g, l_g = _dilated_attention(q_d, k_sh, v_sh, bias, g, dilation, tile=attn_tile)
        outs.append(o_g)
        lses.append(l_g)
    x = _mixout(x, outs, lses, q_mem, mem_k, mem_v, b_w_out[0], ts=512)
    x = ffn(x, ffn2_norm, ffn2_w_gate, ffn2_w_up, ffn2_w_down, 1)
    return x
```

```python
import functools
import math

import numpy as np
import jax
import jax.numpy as jnp
from jax import lax
from jax.experimental import pallas as pl
from jax.experimental.pallas import tpu as pltpu

F32 = jnp.float32
BF16 = jnp.bfloat16

HEAD_DIM = 64
MEM_HEADS = 4
MEM_WIDTH = MEM_HEADS * HEAD_DIM
MLSTM_HEADS = 4
MLSTM_HEAD_DIM = 192
MLSTM_WIDTH = MLSTM_HEADS * MLSTM_HEAD_DIM
MLSTM_CHUNK = 128
DILATED_GROUPS = ((128, 1), (512, 4), (2048, 16))
HEADS_PER_GROUP = 4
GROUP_WIDTH = HEADS_PER_GROUP * HEAD_DIM
DIL_WIDTH = GROUP_WIDTH * len(DILATED_GROUPS)
NUM_BUCKETS = 32
MAX_DISTANCE = 2048
RMS_EPS = 1e-6
NEG_INF = -1e30
ATTN_SCALE = HEAD_DIM ** -0.5

V7X_LANES = 128
V7X_SUBLANES = 8
V7X_MXU_DIM = 256
V7X_VMEM_BYTES = 64 * 1024 * 1024
VMEM_LIMIT_BYTES = V7X_VMEM_BYTES - 8 * 1024 * 1024

HEAD_PAD = V7X_MXU_DIM
MLSTM_PAD_WIDTH = MLSTM_HEADS * HEAD_PAD
DEN_LANE = MLSTM_HEAD_DIM
ATTN_BLOCK = 128


def _compiler_params(n_axes):
    return pltpu.CompilerParams(dimension_semantics=("arbitrary",) * n_axes,
                                vmem_limit_bytes=VMEM_LIMIT_BYTES)


def _resident(shape):
    zeros = (0,) * len(shape)
    return pl.BlockSpec(shape, lambda *_: zeros, pipeline_mode=pl.Buffered(1))


def _dot(a, b):
    return jnp.dot(a, b, preferred_element_type=F32)


def _dot_nt(a, b):
    return lax.dot_general(a, b, (((1,), (1,)), ((), ())), preferred_element_type=F32)


def _dot_tn(a, b):
    return lax.dot_general(a, b, (((0,), (0,)), ((), ())), preferred_element_type=F32)


def _rms(x, gain):
    ms = jnp.mean(x * x, axis=-1, keepdims=True)
    return x * lax.rsqrt(ms + RMS_EPS) * gain


def _split_bf16(x, terms):
    parts = []
    rest = x
    for _ in range(terms):
        part = rest.astype(BF16)
        parts.append(part)
        rest = rest - part.astype(F32)
    return parts


def _head_mean_sq(x, head_avg):
    return sum(_dot(part, head_avg) for part in _split_bf16(x * x, 2))


def _head_rms(x, head_avg, gain):
    return x * lax.rsqrt(_head_mean_sq(x, head_avg) + RMS_EPS) * gain


def _silu(x):
    return x * jax.nn.sigmoid(x)


def _log_sigmoid(x):
    return jnp.minimum(x, 0.0) - jnp.log(1.0 + jnp.exp(-jnp.abs(x)))


def _head_lane_mask(head, width=GROUP_WIDTH):
    lane = lax.broadcasted_iota(jnp.int32, (1, width), 1)
    return (lane >= head * HEAD_DIM) & (lane < (head + 1) * HEAD_DIM)


def _memory_attention(qn, mem_k, mem_v):
    out, _ = _stacked_head_attention(qn, mem_k, mem_v, None, None)
    return out


def _stacked_head_attention(q, keys, values, bias, masked_below):
    t = q.shape[0]
    masks = [_head_lane_mask(h) for h in range(HEADS_PER_GROUP)]
    qs = jnp.concatenate([jnp.where(m, q, jnp.zeros_like(q)) for m in masks], axis=0)
    s = _dot_nt(qs, keys)
    if bias is not None:
        s = s + bias
    if masked_below is not None:
        key = lax.broadcasted_iota(jnp.int32, (1, keys.shape[0]), 1)
        s = jnp.where(key < masked_below, NEG_INF, s)
    mx = jnp.max(s, axis=-1, keepdims=True)
    e = jnp.exp(s - mx)
    den = jnp.sum(e, axis=-1, keepdims=True)
    o_all = _dot(e.astype(BF16), values) / den
    l_all = mx + jnp.log(den)
    out = jnp.zeros((t, q.shape[1]), F32)
    lse = jnp.zeros((t, q.shape[1]), F32)
    for h, m in enumerate(masks):
        out = out + jnp.where(m, o_all[h * t:(h + 1) * t], 0.0)
        lse = lse + jnp.where(m, l_all[h * t:(h + 1) * t], 0.0)
    return out, lse


def _ffn_body(x_ref, gain_ref, wg_ref, wu_ref, wd_ref, o_ref, h_scr, acc_scr, *, f_chunk, n_chunks):
    x = x_ref[...]
    h_scr[...] = _rms(x, gain_ref[...]).astype(BF16)
    for c in range(n_chunks):
        cols = slice(c * f_chunk, (c + 1) * f_chunk)
        h = h_scr[...]
        act = (_silu(_dot(h, wg_ref[:, cols])) * _dot(h, wu_ref[:, cols])).astype(BF16)
        down = _dot(act, wd_ref[cols, :])
        if c == 0:
            acc_scr[...] = down
        else:
            acc_scr[...] += down
    o_ref[...] = x + 0.5 * acc_scr[...]


def _ffn(x2d, gain, w_gate, w_up, w_down, *, tm):
    n, d = x2d.shape
    f = w_gate.shape[1]
    f_chunk = V7X_MXU_DIM
    row_spec = pl.BlockSpec((tm, d), lambda i: (i, 0))
    return pl.pallas_call(
        functools.partial(_ffn_body, f_chunk=f_chunk, n_chunks=f // f_chunk),
        grid=(n // tm,),
        in_specs=[row_spec, _resident((1, d)), _resident((d, f)), _resident((d, f)), _resident((f, d))],
        out_specs=row_spec,
        out_shape=jax.ShapeDtypeStruct((n, d), F32),
        scratch_shapes=[pltpu.VMEM((tm, d), BF16), pltpu.VMEM((tm, d), F32)],
        compiler_params=_compiler_params(1),
        name="ffn",
    )(x2d, gain.reshape(1, d), w_gate.astype(BF16), w_up.astype(BF16), w_down.astype(BF16))


def _memkv_body(mem_ref, gain_ref, w_ref, kgain_ref, havg_ref, k_ref, v_ref):
    mem_n = _rms(mem_ref[0], gain_ref[...]).astype(BF16)
    kv = _dot(mem_n, w_ref[...])
    k_ref[0] = _head_rms(kv[:, :MEM_WIDTH], havg_ref[...], kgain_ref[...]).astype(BF16)
    v_ref[0] = kv[:, MEM_WIDTH:].astype(BF16)


def _memkv(mem, gain, w_mem_kv, k_gain, head_avg):
    b, m, d = mem.shape
    out_spec = pl.BlockSpec((1, m, MEM_WIDTH), lambda i: (i, 0, 0))
    return pl.pallas_call(
        _memkv_body,
        grid=(b,),
        in_specs=[pl.BlockSpec((1, m, d), lambda i: (i, 0, 0)), _resident((1, d)),
                  _resident((d, 2 * MEM_WIDTH)), _resident((1, MEM_WIDTH)),
                  _resident((MEM_WIDTH, MEM_WIDTH))],
        out_specs=[out_spec, out_spec],
        out_shape=[jax.ShapeDtypeStruct((b, m, MEM_WIDTH), BF16)] * 2,
        compiler_params=_compiler_params(1),
        name="memkv",
    )(mem, gain.reshape(1, d), w_mem_kv.astype(BF16), jnp.tile(k_gain, MEM_HEADS).reshape(1, MEM_WIDTH),
      head_avg)


_Q0, _K0, _V0, _O0 = (i * MLSTM_PAD_WIDTH for i in range(4))
_QM0 = 4 * MLSTM_PAD_WIDTH
_GI0 = _QM0 + MEM_WIDTH
_GF0 = _GI0 + V7X_LANES
A_IN_PAD_WIDTH = _GF0 + V7X_LANES
CONV_TAIL = V7X_SUBLANES


def _mlstm_body(x_ref, gain_ref, win_ref, conv_ref, bi_ref, bf_ref, hgain_ref, wout_ref,
                mk_ref, mv_ref, mqgain_ref, havg_ref, ltri_ref,
                o_ref, c_scr, m_scr, ext_scr, hcat_scr, *, ts, conv_width):
    @pl.when(pl.program_id(1) == 0)
    def _():
        c_scr[...] = jnp.zeros(c_scr.shape, F32)
        m_scr[...] = jnp.zeros(m_scr.shape, F32)
        ext_scr[:, 0:CONV_TAIL, :] = jnp.zeros((ext_scr.shape[0], CONV_TAIL, V7X_LANES), F32)

    x = x_ref[0]
    h = _rms(x, gain_ref[...]).astype(BF16)
    p = _dot(h, win_ref[...])

    cw = conv_ref[...]
    conv_slabs = []
    for sl in range(ext_scr.shape[0]):
        lanes = slice(sl * V7X_LANES, (sl + 1) * V7X_LANES)
        pre = p[:, lanes]
        ext_scr[sl, CONV_TAIL:CONV_TAIL + ts, :] = pre
        y = cw[conv_width - 1:conv_width, lanes] * pre
        for back in range(1, conv_width):
            tap = conv_width - 1 - back
            y = y + cw[tap:tap + 1, lanes] * ext_scr[sl, CONV_TAIL - back:CONV_TAIL - back + ts, :]
        ext_scr[sl, 0:CONV_TAIL, :] = pre[ts - CONV_TAIL:ts, :]
        conv_slabs.append(_silu(y))
    qk = jnp.concatenate(conv_slabs, axis=-1)
    q = qk[:, :MLSTM_PAD_WIDTH].astype(BF16)
    k = qk[:, MLSTM_PAD_WIDTH:] * (MLSTM_HEAD_DIM ** -0.5)
    v = p[:, _V0:_O0]
    lane = lax.broadcasted_iota(jnp.int32, (1, MLSTM_PAD_WIDTH), 1)
    v = jnp.where((lane & (HEAD_PAD - 1)) == DEN_LANE, 1.0, v).astype(BF16)
    o_gate = p[:, _O0:_QM0]
    gate_i = p[:, _GI0:_GF0] + bi_ref[...]
    log_f = _log_sigmoid(p[:, _GF0:A_IN_PAD_WIDTH] + bf_ref[...])

    L = ltri_ref.shape[0]
    row = lax.broadcasted_iota(jnp.int32, (L, L), 0)
    col = lax.broadcasted_iota(jnp.int32, (L, L), 1)
    causal = col <= row
    head_lane = lax.broadcasted_iota(jnp.int32, (1, HEAD_PAD), 1)
    ltri = ltri_ref[...]
    for c in range(ts // L):
        rows = slice(c * L, (c + 1) * L)
        b = sum(_dot(ltri, part) for part in _split_bf16(log_f[rows], 3))
        b_last = b[L - 1:L, :]
        r = gate_i[rows] - b
        a = b_last + r
        m0 = m_scr[...]
        m_new = jnp.maximum(b_last + m0, jnp.max(a, axis=0, keepdims=True))
        w = jnp.exp(a - m_new)
        decay = jnp.exp(b_last + m0 - m_new)
        g = b + m0
        r_rows = r.T
        for hh in range(MLSTM_HEADS):
            hl = slice(hh * HEAD_PAD, (hh + 1) * HEAD_PAD)
            qh, kh, vh = q[rows, hl], k[rows, hl], v[rows, hl]
            d_log = jnp.where(causal, b[:, hh:hh + 1] + r_rows[hh:hh + 1, :], NEG_INF)
            g_col = g[:, hh:hh + 1]
            m_t = jnp.maximum(g_col, jnp.max(d_log, axis=-1, keepdims=True))
            wmat = jnp.exp(d_log - m_t) * _dot_nt(qh, kh.astype(BF16))
            c0 = c_scr[hh]
            num = _dot(wmat.astype(BF16), vh) + jnp.exp(g_col - m_t) * _dot(qh, c0.astype(BF16))
            den = num[:, DEN_LANE:DEN_LANE + 1]
            cell = num / jnp.maximum(jnp.abs(den), jnp.exp(-m_t))
            cell = jnp.where(head_lane < MLSTM_HEAD_DIM, cell, 0.0)
            ms = jnp.sum(cell * cell, axis=-1, keepdims=True) * (1.0 / MLSTM_HEAD_DIM)
            cell = cell * lax.rsqrt(ms + RMS_EPS) * hgain_ref[:, hl]
            hcat_scr[rows, hl] = (jax.nn.sigmoid(o_gate[rows, hl]) * cell).astype(BF16)
            kw = (kh * w[:, hh:hh + 1]).astype(BF16)
            c_scr[hh] = decay[:, hh:hh + 1] * c0 + _dot_tn(kw, vh)
        m_scr[...] = m_new

    qm = _head_rms(p[:, _QM0:_GI0], havg_ref[...], mqgain_ref[...] * ATTN_SCALE).astype(BF16)
    hcat_scr[:, MLSTM_PAD_WIDTH:] = _memory_attention(qm, mk_ref[0], mv_ref[0]).astype(BF16)
    o_ref[0] = x + _dot(hcat_scr[...], wout_ref[...])


def _pad_heads(w, axis):
    shape = w.shape
    w = w.reshape(shape[:axis] + (MLSTM_HEADS, MLSTM_HEAD_DIM) + shape[axis + 1:])
    pad = [(0, 0)] * w.ndim
    pad[axis + 1] = (0, HEAD_PAD - MLSTM_HEAD_DIM)
    return jnp.pad(w, pad).reshape(shape[:axis] + (MLSTM_PAD_WIDTH,) + shape[axis + 1:])


def _pad_lanes(w):
    return jnp.pad(w, [(0, 0)] * (w.ndim - 1) + [(0, V7X_LANES - w.shape[-1])])


def _mlstm_layer(x, gain, w_in, conv_w, gate_bias, h_gain, w_out, mem_k, mem_v, mq_gain,
                 head_avg, ltri, *, ts):
    b, s, d = x.shape
    W, H = MLSTM_WIDTH, MLSTM_HEADS
    sections = [_pad_heads(w_in[:, i * W:(i + 1) * W], 1) for i in range(4)]
    w_in_p = jnp.concatenate(sections + [w_in[:, 4 * W + 2 * H:], _pad_lanes(w_in[:, 4 * W:4 * W + H]),
                                         _pad_lanes(w_in[:, 4 * W + H:4 * W + 2 * H])], axis=1).astype(BF16)
    conv_p = jnp.concatenate([_pad_heads(conv_w[:, :W], 1), _pad_heads(conv_w[:, W:], 1)], axis=1)
    w_out_p = jnp.concatenate([_pad_heads(w_out[:W], 0), w_out[W:]], axis=0).astype(BF16)
    conv_width = conv_w.shape[0]
    tile = pl.BlockSpec((1, ts, d), lambda i, j: (i, j, 0))
    mem_spec = pl.BlockSpec((1,) + mem_k.shape[1:], lambda i, j: (i, 0, 0))
    return pl.pallas_call(
        functools.partial(_mlstm_body, ts=ts, conv_width=conv_width),
        grid=(b, s // ts),
        in_specs=[tile, _resident((1, d)), _resident(w_in_p.shape), _resident(conv_p.shape),
                  _resident((1, V7X_LANES)), _resident((1, V7X_LANES)), _resident((1, MLSTM_PAD_WIDTH)),
                  _resident(w_out_p.shape), mem_spec, mem_spec, _resident((1, MEM_WIDTH)),
                  _resident(head_avg.shape), _resident(ltri.shape)],
        out_specs=tile,
        out_shape=jax.ShapeDtypeStruct((b, s, d), F32),
        scratch_shapes=[pltpu.VMEM((H, HEAD_PAD, HEAD_PAD), F32), pltpu.VMEM((1, V7X_LANES), F32),
                        pltpu.VMEM((2 * MLSTM_PAD_WIDTH // V7X_LANES, CONV_TAIL + ts, V7X_LANES), F32),
                        pltpu.VMEM((ts, MLSTM_PAD_WIDTH + MEM_WIDTH), BF16)],
        compiler_params=_compiler_params(2),
        name="mlstm_mixer",
    )(x, gain.reshape(1, d), w_in_p, conv_p, _pad_lanes(gate_bias[:H].reshape(1, H)),
      _pad_lanes(gate_bias[H:].reshape(1, H)), _pad_heads(h_gain.reshape(1, W), 1), w_out_p,
      mem_k, mem_v, jnp.tile(mq_gain, MEM_HEADS).reshape(1, MEM_WIDTH), head_avg, ltri)


def _proj_body(x_ref, gain_ref, w_ref, hgain_ref, havg_ref, *out_refs, normed_width):
    h = _rms(x_ref[0], gain_ref[...]).astype(BF16)
    y = _dot(h, w_ref[...])
    pieces = []
    for c in range(y.shape[1] // GROUP_WIDTH):
        cols = slice(c * GROUP_WIDTH, (c + 1) * GROUP_WIDTH)
        if c * GROUP_WIDTH < normed_width:
            pieces.append(_head_rms(y[:, cols], havg_ref[...], hgain_ref[:, cols]))
        else:
            pieces.append(y[:, cols])
    start = 0
    for ref in out_refs:
        if len(ref.shape) == 4:
            n = ref.shape[1] * V7X_LANES // GROUP_WIDTH
            for c in range(n):
                for half in range(GROUP_WIDTH // V7X_LANES):
                    lanes = slice(half * V7X_LANES, (half + 1) * V7X_LANES)
                    ref[0, c * (GROUP_WIDTH // V7X_LANES) + half] = pieces[start + c][:, lanes]
        else:
            n = ref.shape[2] // GROUP_WIDTH
            for c in range(n):
                ref[0, :, c * GROUP_WIDTH:(c + 1) * GROUP_WIDTH] = pieces[start + c].astype(BF16)
        start += n


def _proj(x, gain, w, head_gain, head_avg, outputs, normed_width, *, tm, name):
    b, s, d = x.shape
    width = w.shape[1]
    out_specs, out_shapes = [], []
    for ow, slab_major in outputs:
        if slab_major:
            out_specs.append(pl.BlockSpec((1, ow // V7X_LANES, tm, V7X_LANES), lambda i, j: (i, 0, j, 0)))
            out_shapes.append(jax.ShapeDtypeStruct((b, ow // V7X_LANES, s, V7X_LANES), F32))
        else:
            out_specs.append(pl.BlockSpec((1, tm, ow), lambda i, j: (i, j, 0)))
            out_shapes.append(jax.ShapeDtypeStruct((b, s, ow), BF16))
    return pl.pallas_call(
        functools.partial(_proj_body, normed_width=normed_width),
        grid=(b, s // tm),
        in_specs=[pl.BlockSpec((1, tm, d), lambda i, j: (i, j, 0)), _resident((1, d)), _resident((d, width)),
                  _resident((1, normed_width)), _resident(head_avg.shape)],
        out_specs=out_specs,
        out_shape=out_shapes,
        compiler_params=_compiler_params(2),
        name=name,
    )(x, gain.reshape(1, d), w.astype(BF16), head_gain.reshape(1, normed_width), head_avg)


SLABS = GROUP_WIDTH // V7X_LANES


def _dilated_body(q_ref, kc_ref, kp_ref, vc_ref, vp_ref, bias_ref, o_ref, l_ref, *, dilation, nblk):
    A = ATTN_BLOCK
    no_prev_limit = jnp.where(pl.program_id(1) == 0, A, 0)
    bias = bias_ref[0]

    def rows_of(start):
        return pl.ds(start, A, stride=dilation) if dilation > 1 else pl.ds(start, A)

    def load(ref, start):
        rows = rows_of(start)
        return jnp.concatenate([ref[0, sl, rows, :] for sl in range(SLABS)], axis=-1).astype(BF16)

    for r in range(dilation):
        k_cur = load(kp_ref, r)
        v_cur = load(vp_ref, r)
        for i in range(nblk):
            start = i * A * dilation + r
            k_prev, v_prev = k_cur, v_cur
            k_cur, v_cur = load(kc_ref, start), load(vc_ref, start)
            out, lse = _stacked_head_attention(
                load(q_ref, start), jnp.concatenate([k_prev, k_cur], axis=0),
                jnp.concatenate([v_prev, v_cur], axis=0), bias, no_prev_limit if i == 0 else None)
            for sl in range(SLABS):
                lanes = slice(sl * V7X_LANES, (sl + 1) * V7X_LANES)
                o_ref[0, sl, rows_of(start), :] = out[:, lanes]
                l_ref[0, sl, rows_of(start), :] = lse[:, lanes]


def _dilated_attention(q, k, v, bias, group, dilation, *, tile):
    b, _, s, _ = q.shape
    span = ATTN_BLOCK * dilation
    nblk = tile // span
    cur = pl.BlockSpec((1, SLABS, tile, V7X_LANES), lambda bi, n: (bi, group, n, 0))
    prev = pl.BlockSpec((1, SLABS, span, V7X_LANES),
                        lambda bi, n: (bi, group, jnp.maximum(n * nblk - 1, 0), 0))
    out_spec = pl.BlockSpec((1, SLABS, tile, V7X_LANES), lambda bi, n: (bi, 0, n, 0))
    out_shape = jax.ShapeDtypeStruct((b, SLABS, s, V7X_LANES), F32)
    return pl.pallas_call(
        functools.partial(_dilated_body, dilation=dilation, nblk=nblk),
        grid=(b, s // tile),
        in_specs=[cur, cur, prev, cur, prev,
                  pl.BlockSpec((1,) + bias.shape[1:], lambda bi, n: (group, 0, 0), pipeline_mode=pl.Buffered(1))],
        out_specs=[out_spec, out_spec],
        out_shape=[out_shape, out_shape],
        compiler_params=_compiler_params(2),
        name=f"dilated_attn_g{group}",
    )(q, k, k, v, v, bias)


def _bias_body(table_ref, bucket_ref, o_ref):
    g = pl.program_id(0)
    bucket = bucket_ref[0]
    for hh in range(HEADS_PER_GROUP):
        acc = jnp.full(bucket.shape, NEG_INF, F32)
        for bkt in range(NUM_BUCKETS):
            acc = jnp.where(bucket == bkt, table_ref[g * HEADS_PER_GROUP + hh, bkt], acc)
        o_ref[0, hh * ATTN_BLOCK:(hh + 1) * ATTN_BLOCK, :] = acc


def _dilated_bias(rel_bias):
    A = ATTN_BLOCK
    m_off = np.arange(A)[:, None] + A - np.arange(2 * A)[None, :]
    band = (m_off >= 0) & (m_off <= A)
    max_exact = NUM_BUCKETS // 2
    buckets = []
    for _, dilation in DILATED_GROUPS:
        dist = np.clip(m_off, 0, A) * dilation
        large = max_exact + (np.log(np.maximum(dist, 1).astype(np.float32) / max_exact)
                             / math.log(MAX_DISTANCE / max_exact) * (NUM_BUCKETS - max_exact)).astype(np.int32)
        bucket = np.where(dist < max_exact, dist, np.minimum(large, NUM_BUCKETS - 1))
        buckets.append(np.where(band, bucket, -1))
    buckets = jnp.asarray(np.stack(buckets), jnp.int32)
    n_groups = len(DILATED_GROUPS)
    return pl.pallas_call(
        _bias_body,
        grid=(n_groups,),
        in_specs=[pl.BlockSpec(memory_space=pltpu.SMEM), pl.BlockSpec((1, A, 2 * A), lambda g: (g, 0, 0))],
        out_specs=pl.BlockSpec((1, HEADS_PER_GROUP * A, 2 * A), lambda g: (g, 0, 0)),
        out_shape=jax.ShapeDtypeStruct((n_groups, HEADS_PER_GROUP * A, 2 * A), F32),
        compiler_params=_compiler_params(1),
        name="dilated_bias",
    )(rel_bias.astype(F32).T, buckets)


def _mixout_body(x_ref, o0_ref, o1_ref, o2_ref, l0_ref, l1_ref, l2_ref, qm_ref, mk_ref, mv_ref, wout_ref,
                 out_ref, hcat_scr):
    wide = lambda ref: jnp.concatenate([ref[0, sl] for sl in range(SLABS)], axis=-1)
    lses = [wide(l0_ref), wide(l1_ref), wide(l2_ref)]
    mx = jnp.maximum(jnp.maximum(lses[0], lses[1]), lses[2])
    es = [jnp.exp(l - mx) for l in lses]
    inv = 1.0 / (es[0] + es[1] + es[2])
    for g, o_ref in enumerate((o0_ref, o1_ref, o2_ref)):
        hcat_scr[:, g * GROUP_WIDTH:(g + 1) * GROUP_WIDTH] = (wide(o_ref) * (es[g] * inv)).astype(BF16)
    hcat_scr[:, DIL_WIDTH:] = _memory_attention(qm_ref[0], mk_ref[0], mv_ref[0]).astype(BF16)
    out_ref[0] = x_ref[0] + _dot(hcat_scr[...], wout_ref[...])


def _mixout(x, outs, lses, q_mem, mem_k, mem_v, w_out, *, ts):
    b, s, d = x.shape
    tile = lambda w: pl.BlockSpec((1, ts, w), lambda i, j: (i, j, 0))
    slab_tile = pl.BlockSpec((1, SLABS, ts, V7X_LANES), lambda i, j: (i, 0, j, 0))
    mem_spec = pl.BlockSpec((1,) + mem_k.shape[1:], lambda i, j: (i, 0, 0))
    return pl.pallas_call(
        _mixout_body,
        grid=(b, s // ts),
        in_specs=[tile(d)] + [slab_tile] * 6 + [tile(MEM_WIDTH), mem_spec, mem_spec, _resident(w_out.shape)],
        out_specs=tile(d),
        out_shape=jax.ShapeDtypeStruct((b, s, d), F32),
        scratch_shapes=[pltpu.VMEM((ts, DIL_WIDTH + MEM_WIDTH), BF16)],
        compiler_params=_compiler_params(2),
        name="dilated_mixer_out",
    )(x, *outs, *lses, q_mem, mem_k, mem_v, w_out.astype(BF16))


def kernel(x, mem, ffn1_norm, ffn1_w_gate, ffn1_w_up, ffn1_w_down, ffn2_norm, ffn2_w_gate, ffn2_w_up, ffn2_w_down, mix_norm, mem_norm, w_mem_kv, mem_q_norm, mem_k_norm, a_w_in, a_conv, a_gate_bias, a_h_norm, a_w_out, b_w_q, b_q_norm, b_w_out, kv_norm, w_kv, kv_k_norm, rel_bias):
    b, s, d = x.shape
    n = b * s
    tm = 1024
    head_id = np.arange(GROUP_WIDTH) // HEAD_DIM
    head_avg = jnp.asarray((head_id[:, None] == head_id[None, :]) / HEAD_DIM, BF16)
    mlstm_chunk = V7X_MXU_DIM
    ltri = jnp.asarray(np.tril(np.ones((mlstm_chunk, mlstm_chunk))), BF16)
    flat = lambda t: t.reshape(n, t.shape[-1])
    ffn = lambda t, norm, wg, wu, wd, layer: _ffn(flat(t), norm[layer], wg[layer], wu[layer], wd[layer],
                                                   tm=tm).reshape(b, s, d)

    x = ffn(x, ffn1_norm, ffn1_w_gate, ffn1_w_up, ffn1_w_down, 0)
    mem_k, mem_v = _memkv(mem, mem_norm[0], w_mem_kv[0], mem_k_norm[0], head_avg)
    x = _mlstm_layer(x, mix_norm[0], a_w_in[0], a_conv[0], a_gate_bias[0], a_h_norm[0], a_w_out[0],
                     mem_k, mem_v, mem_q_norm[0], head_avg, ltri, ts=256)
    x = ffn(x, ffn2_norm, ffn2_w_gate, ffn2_w_up, ffn2_w_down, 0)

    k_sh, v_sh = _proj(x, kv_norm, w_kv, jnp.tile(kv_k_norm, DIL_WIDTH // HEAD_DIM), head_avg,
                       [(DIL_WIDTH, True), (DIL_WIDTH, True)], DIL_WIDTH, tm=tm, name="shared_kv_proj")

    x = ffn(x, ffn1_norm, ffn1_w_gate, ffn1_w_up, ffn1_w_down, 1)
    q_gain = jnp.concatenate([jnp.tile(b_q_norm[0], DIL_WIDTH // HEAD_DIM),
                              jnp.tile(mem_q_norm[1], MEM_HEADS)]) * ATTN_SCALE
    q_d, q_mem = _proj(x, mix_norm[1], b_w_q[0], q_gain, head_avg, [(DIL_WIDTH, True), (MEM_WIDTH, False)],
                       DIL_WIDTH + MEM_WIDTH, tm=tm, name="dilated_q_proj")
    mem_k, mem_v = _memkv(mem, mem_norm[1], w_mem_kv[1], mem_k_norm[1], head_avg)
    bias = _dilated_bias(rel_bias)
    attn_tile = max(window for window, _ in DILATED_GROUPS)
    outs, lses = [], []
    for g, (window, dilation) in enumerate(DILATED_GROUPS):
        assert window // dilation == ATTN_BLOCK and s % attn_tile == 0 and attn_tile % window == 0
        o_g, l_g = _dilated_attention(q_d, k_sh, v_sh, bias, g, dilation, tile=attn_tile)
        outs.append(o_g)
        lses.append(l_g)
    x = _mixout(x, outs, lses, q_mem, mem_k, mem_v, b_w_out[0], ts=512)
    x = ffn(x, ffn2_norm, ffn2_w_gate, ffn2_w_up, ffn2_w_down, 1)
    return x
```

```python
import functools
import math

import numpy as np
import jax
import jax.numpy as jnp
from jax import lax
from jax.experimental import pallas as pl
from jax.experimental.pallas import tpu as pltpu

F32 = jnp.float32
BF16 = jnp.bfloat16

HEAD_DIM = 64
MEM_HEADS = 4
MEM_WIDTH = MEM_HEADS * HEAD_DIM
MLSTM_HEADS = 4
MLSTM_HEAD_DIM = 192
MLSTM_WIDTH = MLSTM_HEADS * MLSTM_HEAD_DIM
MLSTM_CHUNK = 128
DILATED_GROUPS = ((128, 1), (512, 4), (2048, 16))
HEADS_PER_GROUP = 4
GROUP_WIDTH = HEADS_PER_GROUP * HEAD_DIM
DIL_WIDTH = GROUP_WIDTH * len(DILATED_GROUPS)
NUM_BUCKETS = 32
MAX_DISTANCE = 2048
RMS_EPS = 1e-6
NEG_INF = -1e30
ATTN_SCALE = HEAD_DIM ** -0.5

V7X_LANES = 128
V7X_SUBLANES = 8
V7X_MXU_DIM = 256
V7X_VMEM_BYTES = 64 * 1024 * 1024
VMEM_LIMIT_BYTES = V7X_VMEM_BYTES - 8 * 1024 * 1024

HEAD_PAD = V7X_MXU_DIM
MLSTM_PAD_WIDTH = MLSTM_HEADS * HEAD_PAD
DEN_LANE = MLSTM_HEAD_DIM
ATTN_BLOCK = 128


def _compiler_params(n_axes):
    return pltpu.CompilerParams(dimension_semantics=("arbitrary",) * n_axes,
                                vmem_limit_bytes=VMEM_LIMIT_BYTES)


def _resident(shape):
    zeros = (0,) * len(shape)
    return pl.BlockSpec(shape, lambda *_: zeros, pipeline_mode=pl.Buffered(1))


def _dot(a, b):
    return jnp.dot(a, b, preferred_element_type=F32)


def _dot_nt(a, b):
    return lax.dot_general(a, b, (((1,), (1,)), ((), ())), preferred_element_type=F32)


def _dot_tn(a, b):
    return lax.dot_general(a, b, (((0,), (0,)), ((), ())), preferred_element_type=F32)


def _rms(x, gain):
    ms = jnp.mean(x * x, axis=-1, keepdims=True)
    return x * lax.rsqrt(ms + RMS_EPS) * gain


def _split_bf16(x, terms):
    parts = []
    rest = x
    for _ in range(terms):
        part = rest.astype(BF16)
        parts.append(part)
        rest = rest - part.astype(F32)
    return parts


def _head_mean_sq(x, head_avg):
    return sum(_dot(part, head_avg) for part in _split_bf16(x * x, 2))


def _head_rms(x, head_avg, gain):
    return x * lax.rsqrt(_head_mean_sq(x, head_avg) + RMS_EPS) * gain


def _silu(x):
    return x * jax.nn.sigmoid(x)


def _log_sigmoid(x):
    return jnp.minimum(x, 0.0) - jnp.log(1.0 + jnp.exp(-jnp.abs(x)))


def _head_lane_mask(head, width=GROUP_WIDTH):
    lane = lax.broadcasted_iota(jnp.int32, (1, width), 1)
    return (lane >= head * HEAD_DIM) & (lane < (head + 1) * HEAD_DIM)


def _memory_attention(qn, mem_k, mem_v):
    out, _ = _stacked_head_attention(qn, mem_k, mem_v, None, None)
    return out


def _stacked_head_attention(q, keys, values, bias, masked_below):
    t = q.shape[0]
    masks = [_head_lane_mask(h) for h in range(HEADS_PER_GROUP)]
    qs = jnp.concatenate([jnp.where(m, q, jnp.zeros_like(q)) for m in masks], axis=0)
    s = _dot_nt(qs, keys)
    if bias is not None:
        s = s + bias
    if masked_below is not None:
        key = lax.broadcasted_iota(jnp.int32, (1, keys.shape[0]), 1)
        s = jnp.where(key < masked_below, NEG_INF, s)
    mx = jnp.max(s, axis=-1, keepdims=True)
    e = jnp.exp(s - mx)
    den = jnp.sum(e, axis=-1, keepdims=True)
    o_all = _dot(e.astype(BF16), values) / den
    l_all = mx + jnp.log(den)
    out = jnp.zeros((t, q.shape[1]), F32)
    lse = jnp.zeros((t, q.shape[1]), F32)
    for h, m in enumerate(masks):
        out = out + jnp.where(m, o_all[h * t:(h + 1) * t], 0.0)
        lse = lse + jnp.where(m, l_all[h * t:(h + 1) * t], 0.0)
    return out, lse


def _ffn_body(x_ref, gain_ref, wg_ref, wu_ref, wd_ref, o_ref, h_scr, acc_scr, *, f_chunk, n_chunks):
    x = x_ref[...]
    h_scr[...] = _rms(x, gain_ref[...]).astype(BF16)
    for c in range(n_chunks):
        cols = slice(c * f_chunk, (c + 1) * f_chunk)
        h = h_scr[...]
        act = (_silu(_dot(h, wg_ref[:, cols])) * _dot(h, wu_ref[:, cols])).astype(BF16)
        down = _dot(act, wd_ref[cols, :])
        if c == 0:
            acc_scr[...] = down
        else:
            acc_scr[...] += down
    o_ref[...] = x + 0.5 * acc_scr[...]


def _ffn(x2d, gain, w_gate, w_up, w_down, layer, *, tm):
    n, d = x2d.shape
    f = w_gate.shape[2]
    f_chunk = V7X_MXU_DIM
    row_spec = pl.BlockSpec((tm, d), lambda i: (i, 0))
    layer_weight = lambda rows, cols: pl.BlockSpec((None, rows, cols), lambda i: (layer, 0, 0),
                                                   pipeline_mode=pl.Buffered(1))
    return pl.pallas_call(
        functools.partial(_ffn_body, f_chunk=f_chunk, n_chunks=f // f_chunk),
        grid=(n // tm,),
        in_specs=[row_spec, _resident((1, d)), layer_weight(d, f), layer_weight(d, f), layer_weight(f, d)],
        out_specs=row_spec,
        out_shape=jax.ShapeDtypeStruct((n, d), F32),
        scratch_shapes=[pltpu.VMEM((tm, d), BF16), pltpu.VMEM((tm, d), F32)],
        compiler_params=_compiler_params(1),
        name="ffn",
    )(x2d, gain.reshape(1, d), w_gate, w_up, w_down)


def _memkv_body(mem_ref, gain_ref, w_ref, kgain_ref, havg_ref, k_ref, v_ref):
    mem_n = _rms(mem_ref[0], gain_ref[...]).astype(BF16)
    kv = _dot(mem_n, w_ref[...])
    k_ref[0] = _head_rms(kv[:, :MEM_WIDTH], havg_ref[...], kgain_ref[...]).astype(BF16)
    v_ref[0] = kv[:, MEM_WIDTH:].astype(BF16)


def _memkv(mem, gain, w_mem_kv, k_gain, head_avg):
    b, m, d = mem.shape
    out_spec = pl.BlockSpec((1, m, MEM_WIDTH), lambda i: (i, 0, 0))
    return pl.pallas_call(
        _memkv_body,
        grid=(b,),
        in_specs=[pl.BlockSpec((1, m, d), lambda i: (i, 0, 0)), _resident((1, d)),
                  _resident((d, 2 * MEM_WIDTH)), _resident((1, MEM_WIDTH)),
                  _resident((MEM_WIDTH, MEM_WIDTH))],
        out_specs=[out_spec, out_spec],
        out_shape=[jax.ShapeDtypeStruct((b, m, MEM_WIDTH), BF16)] * 2,
        compiler_params=_compiler_params(1),
        name="memkv",
    )(mem, gain.reshape(1, d), w_mem_kv.astype(BF16), jnp.tile(k_gain, MEM_HEADS).reshape(1, MEM_WIDTH),
      head_avg)


_Q0, _K0, _V0, _O0 = (i * MLSTM_PAD_WIDTH for i in range(4))
_QM0 = 4 * MLSTM_PAD_WIDTH
_GI0 = _QM0 + MEM_WIDTH
_GF0 = _GI0 + V7X_LANES
A_IN_PAD_WIDTH = _GF0 + V7X_LANES
CONV_TAIL = V7X_SUBLANES


def _mlstm_body(x_ref, gain_ref, win_ref, conv_ref, bi_ref, bf_ref, hgain_ref, wout_ref,
                mk_ref, mv_ref, mqgain_ref, havg_ref, ltri_ref,
                o_ref, c_scr, m_scr, ext_scr, hcat_scr, *, ts, conv_width):
    @pl.when(pl.program_id(0) == 0)
    def _():
        c_scr[...] = jnp.zeros(c_scr.shape, F32)
        m_scr[...] = jnp.zeros(m_scr.shape, F32)
        ext_scr[:, :, 0:CONV_TAIL, :] = jnp.zeros(ext_scr.shape[:2] + (CONV_TAIL, V7X_LANES), F32)

    nb = x_ref.shape[0]
    projections = [_dot(_rms(x_ref[bi], gain_ref[...]).astype(BF16), win_ref[...]) for bi in range(nb)]
    for bi in range(nb):
        _mlstm_tile(projections[bi], bi, conv_ref, bi_ref, bf_ref, hgain_ref, mk_ref, mv_ref,
                    mqgain_ref, havg_ref, ltri_ref, c_scr, m_scr, ext_scr, hcat_scr, ts=ts,
                    conv_width=conv_width)
        rows = slice(bi * ts, (bi + 1) * ts)
        o_ref[bi] = x_ref[bi] + _dot(hcat_scr[rows, :], wout_ref[...])


def _mlstm_tile(p, bi, conv_ref, bi_ref, bf_ref, hgain_ref, mk_ref, mv_ref, mqgain_ref, havg_ref, ltri_ref,
                c_scr, m_scr, ext_scr, hcat_scr, *, ts, conv_width):
    out_rows = slice(bi * ts, (bi + 1) * ts)
    cw = conv_ref[...]
    conv_slabs = []
    for sl in range(ext_scr.shape[1]):
        lanes = slice(sl * V7X_LANES, (sl + 1) * V7X_LANES)
        pre = p[:, lanes]
        ext_scr[bi, sl, CONV_TAIL:CONV_TAIL + ts, :] = pre
        y = cw[conv_width - 1:conv_width, lanes] * pre
        for back in range(1, conv_width):
            tap = conv_width - 1 - back
            y = y + cw[tap:tap + 1, lanes] * ext_scr[bi, sl, CONV_TAIL - back:CONV_TAIL - back + ts, :]
        ext_scr[bi, sl, 0:CONV_TAIL, :] = pre[ts - CONV_TAIL:ts, :]
        conv_slabs.append(_silu(y))
    qk = jnp.concatenate(conv_slabs, axis=-1)
    q = qk[:, :MLSTM_PAD_WIDTH].astype(BF16)
    k = qk[:, MLSTM_PAD_WIDTH:] * (MLSTM_HEAD_DIM ** -0.5)
    v = p[:, _V0:_O0]
    lane = lax.broadcasted_iota(jnp.int32, (1, MLSTM_PAD_WIDTH), 1)
    v = jnp.where((lane & (HEAD_PAD - 1)) == DEN_LANE, 1.0, v).astype(BF16)
    o_gate = p[:, _O0:_QM0]
    gate_i = p[:, _GI0:_GF0] + bi_ref[...]
    log_f = _log_sigmoid(p[:, _GF0:A_IN_PAD_WIDTH] + bf_ref[...])

    L = ltri_ref.shape[0]
    row = lax.broadcasted_iota(jnp.int32, (L, L), 0)
    col = lax.broadcasted_iota(jnp.int32, (L, L), 1)
    causal = col <= row
    head_lane = lax.broadcasted_iota(jnp.int32, (1, HEAD_PAD), 1)
    ltri = ltri_ref[...]
    for c in range(ts // L):
        rows = slice(c * L, (c + 1) * L)
        b = sum(_dot(ltri, part) for part in _split_bf16(log_f[rows], 3))
        b_last = b[L - 1:L, :]
        r = gate_i[rows] - b
        a = b_last + r
        m0 = m_scr[bi]
        m_new = jnp.maximum(b_last + m0, jnp.max(a, axis=0, keepdims=True))
        w = jnp.exp(a - m_new)
        decay = jnp.exp(b_last + m0 - m_new)
        g = b + m0
        r_rows = r.T
        for hh in range(MLSTM_HEADS):
            hl =slice(hh * HEAD_PAD, (hh + 1) * HEAD_PAD)
            qh, kh, vh = q[rows, hl], k[rows, hl], v[rows, hl]
            d_log = jnp.where(causal, b[:, hh:hh + 1] + r_rows[hh:hh + 1, :], NEG_INF)
            g_col = g[:, hh:hh + 1]
            m_t = jnp.maximum(g_col, jnp.max(d_log, axis=-1, keepdims=True))
            wmat = jnp.exp(d_log - m_t) * _dot_nt(qh, kh.astype(BF16))
            c0 = c_scr[bi, hh]
            num = _dot(wmat.astype(BF16), vh) + jnp.exp(g_col - m_t) * _dot(qh, c0.astype(BF16))
            den = num[:, DEN_LANE:DEN_LANE + 1]
            cell = num / jnp.maximum(jnp.abs(den), jnp.exp(-m_t))
            cell = jnp.where(head_lane < MLSTM_HEAD_DIM, cell, 0.0)
            ms = jnp.sum(cell * cell, axis=-1, keepdims=True) * (1.0 / MLSTM_HEAD_DIM)
            cell = cell * lax.rsqrt(ms + RMS_EPS) * hgain_ref[:, hl]
            hcat_scr[bi * ts + c * L:bi * ts + (c + 1) * L, hl] = (
                jax.nn.sigmoid(o_gate[rows, hl]) * cell).astype(BF16)
            kw = (kh * w[:, hh:hh + 1]).astype(BF16)
            c_scr[bi, hh] = decay[:, hh:hh + 1] * c0 + _dot_tn(kw, vh)
        m_scr[bi] = m_new

    qm = _head_rms(p[:, _QM0:_GI0], havg_ref[...], mqgain_ref[...] * ATTN_SCALE).astype(BF16)
    hcat_scr[out_rows, MLSTM_PAD_WIDTH:] = _memory_attention(qm, mk_ref[bi], mv_ref[bi]).astype(BF16)


def _pad_heads(w, axis):
    shape = w.shape
    w = w.reshape(shape[:axis] + (MLSTM_HEADS, MLSTM_HEAD_DIM) + shape[axis + 1:])
    pad = [(0, 0)] * w.ndim
    pad[axis + 1] = (0, HEAD_PAD - MLSTM_HEAD_DIM)
    return jnp.pad(w, pad).reshape(shape[:axis] + (MLSTM_PAD_WIDTH,) + shape[axis + 1:])


def _pad_lanes(w):
    return jnp.pad(w, [(0, 0)] * (w.ndim - 1) + [(0, V7X_LANES - w.shape[-1])])


def _mlstm_layer(x, gain, w_in, conv_w, gate_bias, h_gain, w_out, mem_k, mem_v, mq_gain,
                 head_avg, ltri, *, ts):
    b, s, d = x.shape
    W, H = MLSTM_WIDTH, MLSTM_HEADS
    sections = [_pad_heads(w_in[:, i * W:(i + 1) * W], 1) for i in range(4)]
    w_in_p = jnp.concatenate(sections + [w_in[:, 4 * W + 2 * H:], _pad_lanes(w_in[:, 4 * W:4 * W + H]),
                                         _pad_lanes(w_in[:, 4 * W + H:4 * W + 2 * H])], axis=1).astype(BF16)
    conv_p = jnp.concatenate([_pad_heads(conv_w[:, :W], 1), _pad_heads(conv_w[:, W:], 1)], axis=1)
    w_out_p = jnp.concatenate([_pad_heads(w_out[:W], 0), w_out[W:]], axis=0).astype(BF16)
    conv_width = conv_w.shape[0]
    tile = pl.BlockSpec((b, ts, d), lambda j: (0, j, 0))
    return pl.pallas_call(
        functools.partial(_mlstm_body, ts=ts, conv_width=conv_width),
        grid=(s // ts,),
        in_specs=[tile, _resident((1, d)), _resident(w_in_p.shape), _resident(conv_p.shape),
                  _resident((1, V7X_LANES)), _resident((1, V7X_LANES)), _resident((1, MLSTM_PAD_WIDTH)),
                  _resident(w_out_p.shape), _resident(mem_k.shape), _resident(mem_v.shape),
                  _resident((1, MEM_WIDTH)), _resident(head_avg.shape), _resident(ltri.shape)],
        out_specs=tile,
        out_shape=jax.ShapeDtypeStruct((b, s, d), F32),
        scratch_shapes=[pltpu.VMEM((b, H, HEAD_PAD, HEAD_PAD), F32), pltpu.VMEM((b, 1, V7X_LANES), F32),
                        pltpu.VMEM((b, 2 * MLSTM_PAD_WIDTH // V7X_LANES, CONV_TAIL + ts, V7X_LANES), F32),
                        pltpu.VMEM((b * ts, MLSTM_PAD_WIDTH + MEM_WIDTH), BF16)],
        compiler_params=_compiler_params(1),
        name="mlstm_mixer",
    )(x, gain.reshape(1, d), w_in_p, conv_p, _pad_lanes(gate_bias[:H].reshape(1, H)),
      _pad_lanes(gate_bias[H:].reshape(1, H)), _pad_heads(h_gain.reshape(1, W), 1), w_out_p,
      mem_k, mem_v, jnp.tile(mq_gain, MEM_HEADS).reshape(1, MEM_WIDTH), head_avg, ltri)


def _proj_body(x_ref, gain_ref, w_ref, hgain_ref, havg_ref, *out_refs, normed_width):
    h = _rms(x_ref[0], gain_ref[...]).astype(BF16)
    y = _dot(h, w_ref[...])
    pieces = []
    for c in range(y.shape[1] // GROUP_WIDTH):
        cols = slice(c * GROUP_WIDTH, (c + 1) * GROUP_WIDTH)
        if c * GROUP_WIDTH < normed_width:
            pieces.append(_head_rms(y[:, cols], havg_ref[...], hgain_ref[:, cols]))
        else:
            pieces.append(y[:, cols])
    start = 0
    for ref in out_refs:
        if len(ref.shape) == 4:
            n = ref.shape[1] * V7X_LANES // GROUP_WIDTH
            for c in range(n):
                for half in range(GROUP_WIDTH // V7X_LANES):
                    lanes = slice(half * V7X_LANES, (half + 1) * V7X_LANES)
                    ref[0, c * (GROUP_WIDTH // V7X_LANES) + half] = pieces[start + c][:, lanes]
        else:
            n = ref.shape[2] // GROUP_WIDTH
            for c in range(n):
                ref[0, :, c * GROUP_WIDTH:(c + 1) * GROUP_WIDTH] = pieces[start + c].astype(BF16)
        start += n


def _proj(x, gain, w, head_gain, head_avg, outputs, normed_width, *, tm, name):
    b, s, d = x.shape
    width = w.shape[1]
    out_specs, out_shapes = [], []
    for ow, slab_major in outputs:
        if slab_major:
            out_specs.append(pl.BlockSpec((1, ow // V7X_LANES, tm, V7X_LANES), lambda i, j: (i, 0, j, 0)))
            out_shapes.append(jax.ShapeDtypeStruct((b, ow // V7X_LANES, s, V7X_LANES), F32))
        else:
            out_specs.append(pl.BlockSpec((1, tm, ow), lambda i, j: (i, j, 0)))
            out_shapes.append(jax.ShapeDtypeStruct((b, s, ow), BF16))
    return pl.pallas_call(
        functools.partial(_proj_body, normed_width=normed_width),
        grid=(b, s // tm),
        in_specs=[pl.BlockSpec((1, tm, d), lambda i, j: (i, j, 0)), _resident((1, d)), _resident((d, width)),
                  _resident((1, normed_width)), _resident(head_avg.shape)],
        out_specs=out_specs,
        out_shape=out_shapes,
        compiler_params=_compiler_params(2),
        name=name,
    )(x, gain.reshape(1, d), w.astype(BF16), head_gain.reshape(1, normed_width), head_avg)


SLABS = GROUP_WIDTH // V7X_LANES


def _dilated_body(q_ref, kc_ref, kp_ref, vc_ref, vp_ref, bias_ref, o_ref, l_ref, *, dilation, nblk):
    A = ATTN_BLOCK
    no_prev_limit = jnp.where(pl.program_id(1) == 0, A, 0)
    bias = bias_ref[0]

    def rows_of(start):
        return pl.ds(start, A, stride=dilation) if dilation > 1 else pl.ds(start, A)

    def load(ref, start):
        rows = rows_of(start)
        return jnp.concatenate([ref[0, sl, rows, :] for sl in range(SLABS)], axis=-1).astype(BF16)

    for r in range(dilation):
        k_cur = load(kp_ref, r)
        v_cur = load(vp_ref, r)
        for i in range(nblk):
            start = i * A * dilation + r
            k_prev, v_prev = k_cur, v_cur
            k_cur, v_cur = load(kc_ref, start), load(vc_ref, start)
            out, lse = _stacked_head_attention(
                load(q_ref, start), jnp.concatenate([k_prev, k_cur], axis=0),
                jnp.concatenate([v_prev, v_cur], axis=0), bias, no_prev_limit if i == 0 else None)
            for sl in range(SLABS):
                lanes = slice(sl * V7X_LANES, (sl + 1) * V7X_LANES)
                o_ref[0, sl, rows_of(start), :] = out[:, lanes]
                l_ref[0, sl, rows_of(start), :] = lse[:, lanes]


def _dilated_attention(q, k, v, bias, group, dilation, *, tile):
    b, _, s, _ = q.shape
    span = ATTN_BLOCK * dilation
    nblk = tile // span
    cur = pl.BlockSpec((1, SLABS, tile, V7X_LANES), lambda bi, n: (bi, group, n, 0))
    prev = pl.BlockSpec((1, SLABS, span, V7X_LANES),
                        lambda bi, n: (bi, group, jnp.maximum(n * nblk - 1, 0), 0))
    out_spec = pl.BlockSpec((1, SLABS, tile, V7X_LANES), lambda bi, n: (bi, 0, n, 0))
    out_shape = jax.ShapeDtypeStruct((b, SLABS, s, V7X_LANES), F32)
    return pl.pallas_call(
        functools.partial(_dilated_body, dilation=dilation, nblk=nblk),
        grid=(b, s // tile),
        in_specs=[cur, cur, prev, cur, prev,
                  pl.BlockSpec((1,) + bias.shape[1:], lambda bi, n: (group, 0, 0), pipeline_mode=pl.Buffered(1))],
        out_specs=[out_spec, out_spec],
        out_shape=[out_shape, out_shape],
        compiler_params=_compiler_params(2),
        name=f"dilated_attn_g{group}",
    )(q, k, k, v, v, bias)


def _bias_body(table_ref, bucket_ref, o_ref):
    g = pl.program_id(0)
    bucket = bucket_ref[0]
    for hh in range(HEADS_PER_GROUP):
        acc = jnp.full(bucket.shape, NEG_INF, F32)
        for bkt in range(NUM_BUCKETS):
            acc = jnp.where(bucket == bkt, table_ref[g * HEADS_PER_GROUP + hh, bkt], acc)
        o_ref[0, hh * ATTN_BLOCK:(hh + 1) * ATTN_BLOCK, :] = acc


def _dilated_bias(rel_bias):
    A = ATTN_BLOCK
    m_off = np.arange(A)[:, None] + A - np.arange(2 * A)[None, :]
    band = (m_off >= 0) & (m_off <= A)
    max_exact = NUM_BUCKETS // 2
    buckets = []
    for _, dilation in DILATED_GROUPS:
        dist = np.clip(m_off, 0, A) * dilation
        large = max_exact + (np.log(np.maximum(dist, 1).astype(np.float32) / max_exact)
                             / math.log(MAX_DISTANCE / max_exact) * (NUM_BUCKETS - max_exact)).astype(np.int32)
        bucket = np.where(dist < max_exact, dist, np.minimum(large, NUM_BUCKETS - 1))
        buckets.append(np.where(band, bucket, -1))
    buckets = jnp.asarray(np.stack(buckets), jnp.int32)
    n_groups = len(DILATED_GROUPS)
    return pl.pallas_call(
        _bias_body,
        grid=(n_groups,),
        in_specs=[pl.BlockSpec(memory_space=pltpu.SMEM), pl.BlockSpec((1, A, 2 * A), lambda g: (g, 0, 0))],
        out_specs=pl.BlockSpec((1, HEADS_PER_GROUP * A, 2 * A), lambda g: (g, 0, 0)),
        out_shape=jax.ShapeDtypeStruct((n_groups, HEADS_PER_GROUP * A, 2 * A), F32),
        compiler_params=_compiler_params(1),
        name="dilated_bias",
    )(rel_bias.astype(F32).T, buckets)


def _mixout_body(x_ref, o0_ref, o1_ref, o2_ref, l0_ref, l1_ref, l2_ref, qm_ref, mk_ref, mv_ref, wout_ref,
                 out_ref, hcat_scr):
    wide = lambda ref: jnp.concatenate([ref[0, sl] for sl in range(SLABS)], axis=-1)
    lses = [wide(l0_ref), wide(l1_ref), wide(l2_ref)]
    mx = jnp.maximum(jnp.maximum(lses[0], lses[1]), lses[2])
    es = [jnp.exp(l - mx) for l in lses]
    inv = 1.0 / (es[0] + es[1] + es[2])
    for g, o_ref in enumerate((o0_ref, o1_ref, o2_ref)):
        hcat_scr[:, g * GROUP_WIDTH:(g + 1) * GROUP_WIDTH] = (wide(o_ref) * (es[g] * inv)).astype(BF16)
    hcat_scr[:, DIL_WIDTH:] = _memory_attention(qm_ref[0], mk_ref[0], mv_ref[0]).astype(BF16)
    out_ref[0] = x_ref[0] + _dot(hcat_scr[...], wout_ref[...])


def _mixout(x, outs, lses, q_mem, mem_k, mem_v, w_out, *, ts):
    b, s, d = x.shape
    tile = lambda w: pl.BlockSpec((1, ts, w), lambda i, j: (i, j, 0))
    slab_tile = pl.BlockSpec((1, SLABS, ts, V7X_LANES), lambda i, j: (i, 0, j, 0))
    mem_spec = pl.BlockSpec((1,) + mem_k.shape[1:], lambda i, j: (i, 0, 0))
    return pl.pallas_call(
        _mixout_body,
        grid=(b, s // ts),
        in_specs=[tile(d)] + [slab_tile] * 6 + [tile(MEM_WIDTH), mem_spec, mem_spec, _resident(w_out.shape)],
        out_specs=tile(d),
        out_shape=jax.ShapeDtypeStruct((b, s, d), F32),
        scratch_shapes=[pltpu.VMEM((ts, DIL_WIDTH + MEM_WIDTH), BF16)],
        compiler_params=_compiler_params(2),
        name="dilated_mixer_out",
    )(x, *outs, *lses, q_mem, mem_k, mem_v, w_out.astype(BF16))


def kernel(x, mem, ffn1_norm, ffn1_w_gate, ffn1_w_up, ffn1_w_down, ffn2_norm, ffn2_w_gate, ffn2_w_up, ffn2_w_down, mix_norm, mem_norm, w_mem_kv, mem_q_norm, mem_k_norm, a_w_in, a_conv, a_gate_bias, a_h_norm, a_w_out, b_w_q, b_q_norm, b_w_out, kv_norm, w_kv, kv_k_norm, rel_bias):
    b, s, d = x.shape
    n = b * s
    tm = 1024
    head_id = np.arange(GROUP_WIDTH) // HEAD_DIM
    head_avg = jnp.asarray((head_id[:, None] == head_id[None, :]) / HEAD_DIM, BF16)
    mlstm_chunk = V7X_MXU_DIM
    ltri = jnp.asarray(np.tril(np.ones((mlstm_chunk, mlstm_chunk))), BF16)
    flat = lambda t: t.reshape(n, t.shape[-1])
    ffn1_w_gate, ffn1_w_up, ffn1_w_down, ffn2_w_gate, ffn2_w_up, ffn2_w_down = (
        w.astype(BF16) for w in (ffn1_w_gate, ffn1_w_up, ffn1_w_down, ffn2_w_gate, ffn2_w_up, ffn2_w_down))
    ffn = lambda t, norm, wg, wu, wd, layer: _ffn(flat(t), norm[layer], wg, wu, wd, layer,
                                                   tm=tm).reshape(b, s, d)

    x = ffn(x, ffn1_norm, ffn1_w_gate, ffn1_w_up, ffn1_w_down, 0)
    mem_k, mem_v = _memkv(mem, mem_norm[0], w_mem_kv[0], mem_k_norm[0], head_avg)
    x = _mlstm_layer(x, mix_norm[0], a_w_in[0], a_conv[0], a_gate_bias[0], a_h_norm[0], a_w_out[0],
                     mem_k, mem_v, mem_q_norm[0], head_avg, ltri, ts=256)
    x = ffn(x, ffn2_norm, ffn2_w_gate, ffn2_w_up, ffn2_w_down, 0)

    k_sh, v_sh = _proj(x, kv_norm, w_kv, jnp.tile(kv_k_norm, DIL_WIDTH // HEAD_DIM), head_avg,
                       [(DIL_WIDTH, True), (DIL_WIDTH, True)], DIL_WIDTH, tm=tm, name="shared_kv_proj")

    x = ffn(x, ffn1_norm, ffn1_w_gate, ffn1_w_up, ffn1_w_down, 1)
    q_gain = jnp.concatenate([jnp.tile(b_q_norm[0], DIL_WIDTH // HEAD_DIM),
                              jnp.tile(mem_q_norm[1], MEM_HEADS)]) * ATTN_SCALE
    q_d, q_mem = _proj(x, mix_norm[1], b_w_q[0], q_gain, head_avg, [(DIL_WIDTH, True), (MEM_WIDTH, False)],
                       DIL_WIDTH + MEM_WIDTH, tm=tm, name="dilated_q_proj")
    mem_k, mem_v = _memkv(mem, mem_norm[1], w_mem_kv[1], mem_k_norm[1], head_avg)
    bias = _dilated_bias(rel_bias)
    attn_tile = max(window for window, _ in DILATED_GROUPS)
    outs, lses = [], []
    for g, (window, dilation) in enumerate(DILATED_GROUPS):
        assert window // dilation == ATTN_BLOCK and s % attn_tile == 0 and attn_tile % window == 0
        o_g, l_g = _dilated_attention(q_d, k_sh, v_sh, bias, g, dilation, tile=attn_tile)
        outs.append(o_g)
        lses.append(l_g)
    x = _mixout(x, outs, lses, q_mem, mem_k, mem_v, b_w_out[0], ts=512)
    x = ffn(x, ffn2_norm, ffn2_w_gate, ffn2_w_up, ffn2_w_down, 1)
    return x
```

```python
import functools
import math

import numpy as np
import jax
import jax.numpy as jnp
from jax import lax
from jax.experimental import pallas as pl
from jax.experimental.pallas import tpu as pltpu

F32 = jnp.float32
BF16 = jnp.bfloat16

HEAD_DIM = 64
MEM_HEADS = 4
MEM_WIDTH = MEM_HEADS * HEAD_DIM
MLSTM_HEADS = 4
MLSTM_HEAD_DIM = 192
MLSTM_WIDTH = MLSTM_HEADS * MLSTM_HEAD_DIM
DILATED_GROUPS = ((128, 1), (512, 4), (2048, 16))
HEADS_PER_GROUP = 4
GROUP_WIDTH = HEADS_PER_GROUP * HEAD_DIM
DIL_WIDTH = GROUP_WIDTH * len(DILATED_GROUPS)
NUM_BUCKETS = 32
MAX_DISTANCE = 2048
RMS_EPS = 1e-6
NEG_INF = -1e30
ATTN_SCALE = HEAD_DIM ** -0.5

V7X_LANES = 128
V7X_SUBLANES = 8
V7X_MXU_DIM = 256
V7X_VMEM_BYTES = 64 * 1024 * 1024
VMEM_LIMIT_BYTES = V7X_VMEM_BYTES - 8 * 1024 * 1024

HEAD_WIN = V7X_MXU_DIM
HEAD_WIN_START = tuple((h * MLSTM_HEAD_DIM) // V7X_LANES * V7X_LANES for h in range(MLSTM_HEADS))
HEAD_WIN_OFFSET = tuple(h * MLSTM_HEAD_DIM - s for h, s in enumerate(HEAD_WIN_START))
assert all(o + MLSTM_HEAD_DIM <= HEAD_WIN for o in HEAD_WIN_OFFSET) and MLSTM_HEAD_DIM < HEAD_WIN
HEAD_DEN_LANE = tuple((o + MLSTM_HEAD_DIM) % HEAD_WIN for o in HEAD_WIN_OFFSET)
ATTN_BLOCK = 128


def _compiler_params(n_axes):
    return pltpu.CompilerParams(dimension_semantics=("arbitrary",) * n_axes,
                                vmem_limit_bytes=VMEM_LIMIT_BYTES)


def _resident(shape):
    zeros = (0,) * len(shape)
    return pl.BlockSpec(shape, lambda *_: zeros, pipeline_mode=pl.Buffered(1))


def _dot(a, b):
    return jnp.dot(a, b, preferred_element_type=F32)


def _dot_nt(a, b):
    return lax.dot_general(a, b, (((1,), (1,)), ((), ())), preferred_element_type=F32)


def _dot_tn(a, b):
    return lax.dot_general(a, b, (((0,), (0,)), ((), ())), preferred_element_type=F32)


def _rms(x, gain):
    ms = jnp.mean(x * x, axis=-1, keepdims=True)
    return x * lax.rsqrt(ms + RMS_EPS) * gain


def _split_bf16(x, terms):
    parts = []
    rest = x
    for _ in range(terms):
        part = rest.astype(BF16)
        parts.append(part)
        rest = rest - part.astype(F32)
    return parts


def _head_mean_sq(x, head_avg):
    return sum(_dot(part, head_avg) for part in _split_bf16(x * x, 2))


def _head_rms(x, head_avg, gain):
    return x * lax.rsqrt(_head_mean_sq(x, head_avg) + RMS_EPS) * gain


def _silu(x):
    return x * jax.nn.sigmoid(x)


def _log_sigmoid(x):
    return jnp.minimum(x, 0.0) - jnp.log(1.0 + jnp.exp(-jnp.abs(x)))


def _head_lane_mask(head, width=GROUP_WIDTH):
    lane = lax.broadcasted_iota(jnp.int32, (1, width), 1)
    return (lane >= head * HEAD_DIM) & (lane < (head + 1) * HEAD_DIM)


def _memory_attention(qn, mem_k, mem_v):
    out, _ = _stacked_head_attention(qn, mem_k, mem_v, None, None)
    return out


def _stacked_head_attention(q, keys, values, bias, masked_below):
    t = q.shape[0]
    masks = [_head_lane_mask(h) for h in range(HEADS_PER_GROUP)]
    qs = jnp.concatenate([jnp.where(m, q, jnp.zeros_like(q)) for m in masks], axis=0)
    s = _dot_nt(qs, keys)
    if bias is not None:
        s = s + bias
    if masked_below is not None:
        key = lax.broadcasted_iota(jnp.int32, (1, keys.shape[0]), 1)
        s = jnp.where(key < masked_below, NEG_INF, s)
    mx = jnp.max(s, axis=-1, keepdims=True)
    e = jnp.exp(s - mx)
    den = jnp.sum(e, axis=-1, keepdims=True)
    o_all = _dot(e.astype(BF16), values) / den
    l_all = mx + jnp.log(den)
    out = jnp.zeros((t, q.shape[1]), F32)
    lse = jnp.zeros((t, q.shape[1]), F32)
    for h, m in enumerate(masks):
        out = out + jnp.where(m, o_all[h * t:(h + 1) * t], 0.0)
        lse = lse + jnp.where(m, l_all[h * t:(h + 1) * t], 0.0)
    return out, lse


def _ffn_body(x_ref, gain_ref, wg_ref, wu_ref, wd_ref, o_ref, h_scr, acc_scr, *, f_chunk, n_chunks):
    x = x_ref[...]
    h_scr[...] = _rms(x, gain_ref[...]).astype(BF16)
    for c in range(n_chunks):
        cols = slice(c * f_chunk, (c + 1) * f_chunk)
        h = h_scr[...]
        act = (_silu(_dot(h, wg_ref[:, cols])) * _dot(h, wu_ref[:, cols])).astype(BF16)
        down = _dot(act, wd_ref[cols, :])
        if c == 0:
            acc_scr[...] = down
        else:
            acc_scr[...] += down
    o_ref[...] = x + 0.5 * acc_scr[...]


def _ffn(x2d, gain, w_gate, w_up, w_down, layer, *, tm):
    n, d = x2d.shape
    f = w_gate.shape[2]
    f_chunk = V7X_MXU_DIM
    row_spec = pl.BlockSpec((tm, d), lambda i: (i, 0))
    layer_weight = lambda rows, cols: pl.BlockSpec((None, rows, cols), lambda i: (layer, 0, 0),
                                                   pipeline_mode=pl.Buffered(1))
    return pl.pallas_call(
        functools.partial(_ffn_body, f_chunk=f_chunk, n_chunks=f // f_chunk),
        grid=(n // tm,),
        in_specs=[row_spec, _resident((1, d)), layer_weight(d, f), layer_weight(d, f), layer_weight(f, d)],
        out_specs=row_spec,
        out_shape=jax.ShapeDtypeStruct((n, d), F32),
        scratch_shapes=[pltpu.VMEM((tm, d), BF16), pltpu.VMEM((tm, d), F32)],
        compiler_params=_compiler_params(1),
        name="ffn",
    )(x2d, gain.reshape(1, d), w_gate, w_up, w_down)


def _memkv_body(mem_ref, gain_ref, w_ref, kgain_ref, havg_ref, k_ref, v_ref):
    mem_n = _rms(mem_ref[0], gain_ref[...]).astype(BF16)
    kv = _dot(mem_n, w_ref[...])
    k_ref[0] = _head_rms(kv[:, :MEM_WIDTH], havg_ref[...], kgain_ref[...]).astype(BF16)
    v_ref[0] = kv[:, MEM_WIDTH:].astype(BF16)


def _memkv(mem, gain, w_mem_kv, k_gain, head_avg):
    b, m, d = mem.shape
    out_spec = pl.BlockSpec((1, m, MEM_WIDTH), lambda i: (i, 0, 0))
    return pl.pallas_call(
        _memkv_body,
        grid=(b,),
        in_specs=[pl.BlockSpec((1, m, d), lambda i: (i, 0, 0)), _resident((1, d)),
                  _resident((d, 2 * MEM_WIDTH)), _resident((1, MEM_WIDTH)),
                  _resident((MEM_WIDTH, MEM_WIDTH))],
        out_specs=[out_spec, out_spec],
        out_shape=[jax.ShapeDtypeStruct((b, m, MEM_WIDTH), BF16)] * 2,
        compiler_params=_compiler_params(1),
        name="memkv",
    )(mem, gain.reshape(1, d), w_mem_kv.astype(BF16), jnp.tile(k_gain, MEM_HEADS).reshape(1, MEM_WIDTH),
      head_avg)


_Q0, _K0, _V0, _O0 = (i * MLSTM_WIDTH for i in range(4))
_QM0 = 4 * MLSTM_WIDTH
_GI0 = _QM0 + MEM_WIDTH
_GF0 = _GI0 + V7X_LANES
A_IN_PAD_WIDTH = _GF0 + V7X_LANES
CONV_TAIL = V7X_SUBLANES


def _mlstm_body(x_ref, gain_ref, win_ref, conv_ref, bi_ref, bf_ref, hgain_ref, wout_ref,
                mk_ref, mv_ref, mqgain_ref, havg_ref, ltri_ref,
                o_ref, c_scr, m_scr, ext_scr, hcat_scr, *, ts, conv_width):
    @pl.when(pl.program_id(0) == 0)
    def _():
        c_scr[...] = jnp.zeros(c_scr.shape, F32)
        m_scr[...] = jnp.zeros(m_scr.shape, F32)
        ext_scr[:, :, 0:CONV_TAIL, :] = jnp.zeros(ext_scr.shape[:2] + (CONV_TAIL, V7X_LANES), F32)

    nb = x_ref.shape[0]
    projections = [_dot(_rms(x_ref[bi], gain_ref[...]).astype(BF16), win_ref[...]) for bi in range(nb)]
    for bi in range(nb):
        _mlstm_tile(projections[bi], bi, conv_ref, bi_ref, bf_ref, hgain_ref, mk_ref, mv_ref,
                    mqgain_ref, havg_ref, ltri_ref, c_scr, m_scr, ext_scr, hcat_scr, ts=ts,
                    conv_width=conv_width)
        rows = slice(bi * ts, (bi + 1) * ts)
        o_ref[bi] = x_ref[bi] + _dot(hcat_scr[rows, :], wout_ref[...])


def _mlstm_tile(p, bi, conv_ref, bi_ref, bf_ref, hgain_ref, mk_ref, mv_ref, mqgain_ref, havg_ref, ltri_ref,
                c_scr, m_scr, ext_scr, hcat_scr, *, ts, conv_width):
    out_rows = slice(bi * ts, (bi + 1) * ts)
    cw = conv_ref[...]
    conv_slabs = []
    for sl in range(ext_scr.shape[1]):
        lanes = slice(sl * V7X_LANES, (sl + 1) * V7X_LANES)
        pre = p[:, lanes]
        ext_scr[bi, sl, CONV_TAIL:CONV_TAIL + ts, :] = pre
        y = cw[conv_width - 1:conv_width, lanes] * pre
        for back in range(1, conv_width):
            tap = conv_width - 1 - back
            y = y + cw[tap:tap + 1, lanes] * ext_scr[bi, sl, CONV_TAIL - back:CONV_TAIL - back + ts, :]
        ext_scr[bi, sl, 0:CONV_TAIL, :] = pre[ts - CONV_TAIL:ts, :]
        conv_slabs.append(_silu(y))
    qk = jnp.concatenate(conv_slabs, axis=-1)
    q = qk[:, :MLSTM_WIDTH].astype(BF16)
    k = qk[:, MLSTM_WIDTH:] * (MLSTM_HEAD_DIM ** -0.5)
    v = p[:, _V0:_O0].astype(BF16)
    o_gate = p[:, _O0:_QM0]
    gate_i = p[:, _GI0:_GF0] + bi_ref[...]
    log_f = _log_sigmoid(p[:, _GF0:A_IN_PAD_WIDTH] + bf_ref[...])

    L = ltri_ref.shape[0]
    row = lax.broadcasted_iota(jnp.int32, (L, L), 0)
    col = lax.broadcasted_iota(jnp.int32, (L, L), 1)
    causal = col <= row
    win_lane = lax.broadcasted_iota(jnp.int32, (1, HEAD_WIN), 1)
    ltri = ltri_ref[...]
    for c in range(ts // L):
        rows = slice(c * L, (c + 1) * L)
        cells = []
        b = sum(_dot(ltri, part) for part in _split_bf16(log_f[rows], 3))
        b_last = b[L - 1:L, :]
        r = gate_i[rows] - b
        a = b_last + r
        m0 = m_scr[bi]
        m_new = jnp.maximum(b_last + m0, jnp.max(a, axis=0, keepdims=True))
        w = jnp.exp(a - m_new)
        decay = jnp.exp(b_last + m0 - m_new)
        g = b + m0
        r_rows = r.T
        for hh in range(MLSTM_HEADS):
            win = slice(HEAD_WIN_START[hh], HEAD_WIN_START[hh] + HEAD_WIN)
            off, den_lane = HEAD_WIN_OFFSET[hh], HEAD_DEN_LANE[hh]
            in_head = (win_lane >= off) & (win_lane < off + MLSTM_HEAD_DIM)
            qh = jnp.where(in_head, q[rows, win], jnp.zeros((), BF16))
            kh = k[rows, win]
            ones_lane = jnp.where(win_lane == den_lane, 1.0, 0.0).astype(BF16)
            vh = jnp.where(in_head, v[rows, win], ones_lane)
            d_log = jnp.where(causal, b[:, hh:hh + 1] + r_rows[hh:hh + 1, :], NEG_INF)
            g_col = g[:, hh:hh + 1]
            m_t = jnp.maximum(g_col, jnp.max(d_log, axis=-1, keepdims=True))
            wmat = jnp.exp(d_log - m_t) * _dot_nt(qh, kh.astype(BF16))
            c0 = c_scr[bi, hh]
            num = _dot(wmat.astype(BF16), vh) + jnp.exp(g_col - m_t) * _dot(qh, c0.astype(BF16))
            den = num[:, den_lane:den_lane + 1]
            cell = num / jnp.maximum(jnp.abs(den), jnp.exp(-m_t))
            cell = jnp.where(in_head, cell, 0.0)
            ms = jnp.sum(cell * cell, axis=-1, keepdims=True) * (1.0 / MLSTM_HEAD_DIM)
            cells.append(cell * lax.rsqrt(ms + RMS_EPS))
            kw = (kh * w[:, hh:hh + 1]).astype(BF16)
            c_scr[bi, hh] = decay[:, hh:hh + 1] * c0 + _dot_tn(kw, vh)
        m_scr[bi] = m_new
        pieces = {}
        for hh, cell in enumerate(cells):
            for half in range(HEAD_WIN // V7X_LANES):
                tile_idx = HEAD_WIN_START[hh] // V7X_LANES + half
                piece = cell[:, half * V7X_LANES:(half + 1) * V7X_LANES]
                pieces[tile_idx] = piece if tile_idx not in pieces else pieces[tile_idx] + piece
        cell_all = jnp.concatenate([pieces[t] for t in range(MLSTM_WIDTH // V7X_LANES)], axis=-1)
        hcat_scr[bi * ts + c * L:bi * ts + (c + 1) * L, :MLSTM_WIDTH] = (
            jax.nn.sigmoid(o_gate[rows]) * cell_all * hgain_ref[...]).astype(BF16)

    qm = _head_rms(p[:, _QM0:_GI0], havg_ref[...], mqgain_ref[...] * ATTN_SCALE).astype(BF16)
    hcat_scr[out_rows, MLSTM_WIDTH:] = _memory_attention(qm, mk_ref[bi], mv_ref[bi]).astype(BF16)


def _pad_lanes(w):
    return jnp.pad(w, [(0, 0)] * (w.ndim - 1) + [(0, V7X_LANES - w.shape[-1])])


def _mlstm_layer(x, gain, w_in, conv_w, gate_bias, h_gain, w_out, mem_k, mem_v, mq_gain,
                 head_avg, ltri, *, ts):
    b, s, d = x.shape
    W, H = MLSTM_WIDTH, MLSTM_HEADS
    w_in_p = jnp.concatenate([w_in[:, :4 * W], w_in[:, 4 * W + 2 * H:], _pad_lanes(w_in[:, 4 * W:4 * W + H]),
                              _pad_lanes(w_in[:, 4 * W + H:4 * W + 2 * H])], axis=1).astype(BF16)
    conv_p = conv_w
    w_out_p = w_out.astype(BF16)
    conv_width = conv_w.shape[0]
    tile = pl.BlockSpec((b, ts, d), lambda j: (0, j, 0))
    return pl.pallas_call(
        functools.partial(_mlstm_body, ts=ts, conv_width=conv_width),
        grid=(s // ts,),
        in_specs=[tile, _resident((1, d)), _resident(w_in_p.shape), _resident(conv_p.shape),
                  _resident((1, V7X_LANES)), _resident((1, V7X_LANES)), _resident((1, W)),
                  _resident(w_out_p.shape), _resident(mem_k.shape), _resident(mem_v.shape),
                  _resident((1, MEM_WIDTH)), _resident(head_avg.shape), _resident(ltri.shape)],
        out_specs=tile,
        out_shape=jax.ShapeDtypeStruct((b, s, d), F32),
        scratch_shapes=[pltpu.VMEM((b, H, HEAD_WIN, HEAD_WIN), F32), pltpu.VMEM((b, 1, V7X_LANES), F32),
                        pltpu.VMEM((b, 2 * W // V7X_LANES, CONV_TAIL + ts, V7X_LANES), F32),
                        pltpu.VMEM((b * ts, W + MEM_WIDTH), BF16)],
        compiler_params=_compiler_params(1),
        name="mlstm_mixer",
    )(x, gain.reshape(1, d), w_in_p, conv_p, _pad_lanes(gate_bias[:H].reshape(1, H)),
      _pad_lanes(gate_bias[H:].reshape(1, H)), h_gain.reshape(1, W), w_out_p,
      mem_k, mem_v, jnp.tile(mq_gain, MEM_HEADS).reshape(1, MEM_WIDTH), head_avg, ltri)


def _proj_body(x_ref, gain_ref, w_ref, hgain_ref, havg_ref, *out_refs, normed_width):
    h = _rms(x_ref[0], gain_ref[...]).astype(BF16)
    y = _dot(h, w_ref[...])
    pieces = []
    for c in range(y.shape[1] // GROUP_WIDTH):
        cols = slice(c * GROUP_WIDTH, (c + 1) * GROUP_WIDTH)
        if c * GROUP_WIDTH < normed_width:
            pieces.append(_head_rms(y[:, cols], havg_ref[...], hgain_ref[:, cols]))
        else:
            pieces.append(y[:, cols])
    start = 0
    for ref in out_refs:
        if len(ref.shape) == 4:
            n = ref.shape[1] * V7X_LANES // GROUP_WIDTH
            for c in range(n):
                for half in range(GROUP_WIDTH // V7X_LANES):
                    lanes = slice(half * V7X_LANES, (half + 1) * V7X_LANES)
                    ref[0, c * (GROUP_WIDTH // V7X_LANES) + half] = pieces[start + c][:, lanes]
        else:
            n = ref.shape[2] // GROUP_WIDTH
            for c in range(n):
                ref[0, :, c * GROUP_WIDTH:(c + 1) * GROUP_WIDTH] = pieces[start + c].astype(BF16)
        start += n


def _proj(x, gain, w, head_gain, head_avg, outputs, normed_width, *, tm, name):
    b, s, d = x.shape
    width = w.shape[1]
    out_specs, out_shapes = [], []
    for ow, slab_major in outputs:
        if slab_major:
            out_specs.append(pl.BlockSpec((1, ow // V7X_LANES, tm, V7X_LANES), lambda i, j: (i, 0, j, 0)))
            out_shapes.append(jax.ShapeDtypeStruct((b, ow // V7X_LANES, s, V7X_LANES), F32))
        else:
            out_specs.append(pl.BlockSpec((1, tm, ow), lambda i, j: (i, j, 0)))
            out_shapes.append(jax.ShapeDtypeStruct((b, s, ow), BF16))
    return pl.pallas_call(
        functools.partial(_proj_body, normed_width=normed_width),
        grid=(b, s // tm),
        in_specs=[pl.BlockSpec((1, tm, d), lambda i, j: (i, j, 0)), _resident((1, d)), _resident((d, width)),
                  _resident((1, normed_width)), _resident(head_avg.shape)],
        out_specs=out_specs,
        out_shape=out_shapes,
        compiler_params=_compiler_params(2),
        name=name,
    )(x, gain.reshape(1, d), w.astype(BF16), head_gain.reshape(1, normed_width), head_avg)


SLABS = GROUP_WIDTH // V7X_LANES


def _dilated_body(q_ref, kc_ref, kp_ref, vc_ref, vp_ref, bias_ref, o_ref, l_ref, *, dilation, nblk):
    A = ATTN_BLOCK
    no_prev_limit = jnp.where(pl.program_id(1) == 0, A, 0)
    bias = bias_ref[0]

    def rows_of(start):
        return pl.ds(start, A, stride=dilation) if dilation > 1 else pl.ds(start, A)

    def load(ref, start):
        rows = rows_of(start)
        return jnp.concatenate([ref[0, sl, rows, :] for sl in range(SLABS)], axis=-1).astype(BF16)

    for r in range(dilation):
        k_cur = load(kp_ref, r)
        v_cur = load(vp_ref, r)
        for i in range(nblk):
            start = i * A * dilation + r
            k_prev, v_prev = k_cur, v_cur
            k_cur, v_cur = load(kc_ref, start), load(vc_ref, start)
            out, lse = _stacked_head_attention(
                load(q_ref, start), jnp.concatenate([k_prev, k_cur], axis=0),
                jnp.concatenate([v_prev, v_cur], axis=0), bias, no_prev_limit if i == 0 else None)
            for sl in range(SLABS):
                lanes = slice(sl * V7X_LANES, (sl + 1) * V7X_LANES)
                o_ref[0, sl, rows_of(start), :] = out[:, lanes]
                l_ref[0, sl, rows_of(start), :] = lse[:, lanes]


def _dilated_attention(q, k, v, bias, group, dilation, *, tile):
    b, _, s, _ = q.shape
    span = ATTN_BLOCK * dilation
    nblk = tile // span
    cur = pl.BlockSpec((1, SLABS, tile, V7X_LANES), lambda bi, n: (bi, group, n, 0))
    prev = pl.BlockSpec((1, SLABS, span, V7X_LANES),
                        lambda bi, n: (bi, group, jnp.maximum(n * nblk - 1, 0), 0))
    out_spec = pl.BlockSpec((1, SLABS, tile, V7X_LANES), lambda bi, n: (bi, 0, n, 0))
    out_shape = jax.ShapeDtypeStruct((b, SLABS, s, V7X_LANES), F32)
    return pl.pallas_call(
        functools.partial(_dilated_body, dilation=dilation, nblk=nblk),
        grid=(b, s // tile),
        in_specs=[cur, cur, prev, cur, prev,
                  pl.BlockSpec((1,) + bias.shape[1:], lambda bi, n: (group, 0, 0), pipeline_mode=pl.Buffered(1))],
        out_specs=[out_spec, out_spec],
        out_shape=[out_shape, out_shape],
        compiler_params=_compiler_params(2),
        name=f"dilated_attn_g{group}",
    )(q, k, k, v, v, bias)


def _bias_body(table_ref, bucket_ref, o_ref):
    g = pl.program_id(0)
    bucket = bucket_ref[0]
    for hh in range(HEADS_PER_GROUP):
        acc = jnp.full(bucket.shape, NEG_INF, F32)
        for bkt in range(NUM_BUCKETS):
            acc = jnp.where(bucket == bkt, table_ref[g * HEADS_PER_GROUP + hh, bkt], acc)
        o_ref[0, hh * ATTN_BLOCK:(hh + 1) * ATTN_BLOCK, :] = acc


def _dilated_bias(rel_bias):
    A = ATTN_BLOCK
    m_off = np.arange(A)[:, None] + A - np.arange(2 * A)[None, :]
    band = (m_off >= 0) & (m_off <= A)
    max_exact = NUM_BUCKETS // 2
    buckets = []
    for _, dilation in DILATED_GROUPS:
        dist = np.clip(m_off, 0, A) * dilation
        large = max_exact + (np.log(np.maximum(dist, 1).astype(np.float32) / max_exact)
                             / math.log(MAX_DISTANCE / max_exact) * (NUM_BUCKETS - max_exact)).astype(np.int32)
        bucket = np.where(dist < max_exact, dist, np.minimum(large, NUM_BUCKETS - 1))
        buckets.append(np.where(band, bucket, -1))
    buckets = jnp.asarray(np.stack(buckets), jnp.int32)
    n_groups = len(DILATED_GROUPS)
    return pl.pallas_call(
        _bias_body,
        grid=(n_groups,),
        in_specs=[pl.BlockSpec(memory_space=pltpu.SMEM), pl.BlockSpec((1, A, 2 * A), lambda g: (g, 0, 0))],
        out_specs=pl.BlockSpec((1, HEADS_PER_GROUP * A, 2 * A), lambda g: (g, 0, 0)),
        out_shape=jax.ShapeDtypeStruct((n_groups, HEADS_PER_GROUP * A, 2 * A), F32),
        compiler_params=_compiler_params(1),
        name="dilated_bias",
    )(rel_bias.astype(F32).T, buckets)


def _mixout_body(x_ref, o0_ref, o1_ref, o2_ref, l0_ref, l1_ref, l2_ref, qm_ref, mk_ref, mv_ref, wout_ref,
                 out_ref, hcat_scr):
    wide = lambda ref: jnp.concatenate([ref[0, sl] for sl in range(SLABS)], axis=-1)
    lses = [wide(l0_ref), wide(l1_ref), wide(l2_ref)]
    mx = jnp.maximum(jnp.maximum(lses[0], lses[1]), lses[2])
    es = [jnp.exp(l - mx) for l in lses]
    inv = 1.0 / (es[0] + es[1] + es[2])
    for g, o_ref in enumerate((o0_ref, o1_ref, o2_ref)):
        hcat_scr[:, g * GROUP_WIDTH:(g + 1) * GROUP_WIDTH] = (wide(o_ref) * (es[g] * inv)).astype(BF16)
    hcat_scr[:, DIL_WIDTH:] = _memory_attention(qm_ref[0], mk_ref[0], mv_ref[0]).astype(BF16)
    out_ref[0] = x_ref[0] + _dot(hcat_scr[...], wout_ref[...])


def _mixout(x, outs, lses, q_mem, mem_k, mem_v, w_out, *, ts):
    b, s, d = x.shape
    tile = lambda w: pl.BlockSpec((1, ts, w), lambda i, j: (i, j, 0))
    slab_tile = pl.BlockSpec((1, SLABS, ts, V7X_LANES), lambda i, j: (i, 0, j, 0))
    mem_spec = pl.BlockSpec((1,) + mem_k.shape[1:], lambda i, j: (i, 0, 0))
    return pl.pallas_call(
        _mixout_body,
        grid=(b, s // ts),
        in_specs=[tile(d)] + [slab_tile] * 6 + [tile(MEM_WIDTH), mem_spec, mem_spec, _resident(w_out.shape)],
        out_specs=tile(d),
        out_shape=jax.ShapeDtypeStruct((b, s, d), F32),
        scratch_shapes=[pltpu.VMEM((ts, DIL_WIDTH + MEM_WIDTH), BF16)],
        compiler_params=_compiler_params(2),
        name="dilated_mixer_out",
    )(x, *outs, *lses, q_mem, mem_k, mem_v, w_out.astype(BF16))


def kernel(x, mem, ffn1_norm, ffn1_w_gate, ffn1_w_up, ffn1_w_down, ffn2_norm, ffn2_w_gate, ffn2_w_up, ffn2_w_down, mix_norm, mem_norm, w_mem_kv, mem_q_norm, mem_k_norm, a_w_in, a_conv, a_gate_bias, a_h_norm, a_w_out, b_w_q, b_q_norm, b_w_out, kv_norm, w_kv, kv_k_norm, rel_bias):
    b, s, d = x.shape
    n = b * s
    tm = 1024
    head_id = np.arange(GROUP_WIDTH) // HEAD_DIM
    head_avg = jnp.asarray((head_id[:, None] == head_id[None, :]) / HEAD_DIM, BF16)
    mlstm_chunk = V7X_MXU_DIM
    ltri = jnp.asarray(np.tril(np.ones((mlstm_chunk, mlstm_chunk))), BF16)
    flat = lambda t: t.reshape(n, t.shape[-1])
    ffn1_w_gate, ffn1_w_up, ffn1_w_down, ffn2_w_gate, ffn2_w_up, ffn2_w_down = (
        w.astype(BF16) for w in (ffn1_w_gate, ffn1_w_up, ffn1_w_down, ffn2_w_gate, ffn2_w_up, ffn2_w_down))
    ffn = lambda t, norm, wg, wu, wd, layer: _ffn(flat(t), norm[layer], wg, wu, wd, layer,
                                                   tm=tm).reshape(b, s, d)

    x = ffn(x, ffn1_norm, ffn1_w_gate, ffn1_w_up, ffn1_w_down, 0)
    mem_k, mem_v = _memkv(mem, mem_norm[0], w_mem_kv[0], mem_k_norm[0], head_avg)
    x = _mlstm_layer(x, mix_norm[0], a_w_in[0], a_conv[0], a_gate_bias[0], a_h_norm[0], a_w_out[0],
                     mem_k, mem_v, mem_q_norm[0], head_avg, ltri, ts=256)
    x = ffn(x, ffn2_norm, ffn2_w_gate, ffn2_w_up, ffn2_w_down, 0)

    k_sh, v_sh = _proj(x, kv_norm, w_kv, jnp.tile(kv_k_norm, DIL_WIDTH // HEAD_DIM), head_avg,
                       [(DIL_WIDTH, True), (DIL_WIDTH, True)], DIL_WIDTH, tm=tm, name="shared_kv_proj")

    x = ffn(x, ffn1_norm, ffn1_w_gate, ffn1_w_up, ffn1_w_down, 1)
    q_gain = jnp.concatenate([jnp.tile(b_q_norm[0], DIL_WIDTH // HEAD_DIM),
                              jnp.tile(mem_q_norm[1], MEM_HEADS)]) * ATTN_SCALE
    q_d, q_mem = _proj(x, mix_norm[1], b_w_q[0], q_gain, head_avg, [(DIL_WIDTH, True), (MEM_WIDTH, False)],
                       DIL_WIDTH + MEM_WIDTH, tm=tm, name="dilated_q_proj")
    mem_k, mem_v = _memkv(mem, mem_norm[1], w_mem_kv[1], mem_k_norm[1], head_avg)
    bias = _dilated_bias(rel_bias)
    attn_tile = max(window for window, _ in DILATED_GROUPS)
    outs, lses = [], []
    for g, (window, dilation) in enumerate(DILATED_GROUPS):
        assert window // dilation == ATTN_BLOCK and s % attn_tile == 0 and attn_tile % window == 0
        o_g, l_g = _dilated_attention(q_d, k_sh, v_sh, bias, g, dilation, tile=attn_tile)
        outs.append(o_g)
        lses.append(l_g)
    x = _mixout(x, outs, lses, q_mem, mem_k, mem_v, b_w_out[0], ts=512)
    x = ffn(x, ffn2_norm, ffn2_w_gate, ffn2_w_up, ffn2_w_down, 1)
    return x
```

```python
import functools
import math

import numpy as np
import jax
import jax.numpy as jnp
from jax import lax
from jax.experimental import pallas as pl
from jax.experimental.pallas import tpu as pltpu

F32 = jnp.float32
BF16 = jnp.bfloat16

HEAD_DIM = 64
MEM_HEADS = 4
MEM_WIDTH = MEM_HEADS * HEAD_DIM
MLSTM_HEADS = 4
MLSTM_HEAD_DIM = 192
MLSTM_WIDTH = MLSTM_HEADS * MLSTM_HEAD_DIM
DILATED_GROUPS = ((128, 1), (512, 4), (2048, 16))
HEADS_PER_GROUP = 4
GROUP_WIDTH = HEADS_PER_GROUP * HEAD_DIM
DIL_WIDTH = GROUP_WIDTH * len(DILATED_GROUPS)
NUM_BUCKETS = 32
MAX_DISTANCE = 2048
RMS_EPS = 1e-6
NEG_INF = -1e30
ATTN_SCALE = HEAD_DIM ** -0.5

V7X_LANES = 128
V7X_SUBLANES = 8
V7X_MXU_DIM = 256
V7X_VMEM_BYTES = 64 * 1024 * 1024
VMEM_LIMIT_BYTES = V7X_VMEM_BYTES - 8 * 1024 * 1024

HEAD_WIN = V7X_MXU_DIM
HEAD_WIN_START = tuple((h * MLSTM_HEAD_DIM) // V7X_LANES * V7X_LANES for h in range(MLSTM_HEADS))
HEAD_WIN_OFFSET = tuple(h * MLSTM_HEAD_DIM - s for h, s in enumerate(HEAD_WIN_START))
assert all(o + MLSTM_HEAD_DIM <= HEAD_WIN for o in HEAD_WIN_OFFSET) and MLSTM_HEAD_DIM < HEAD_WIN
HEAD_DEN_LANE = tuple((o + MLSTM_HEAD_DIM) % HEAD_WIN for o in HEAD_WIN_OFFSET)
ATTN_BLOCK = 128


def _compiler_params(n_axes):
    return pltpu.CompilerParams(dimension_semantics=("arbitrary",) * n_axes,
                                vmem_limit_bytes=VMEM_LIMIT_BYTES)


def _resident(shape):
    zeros = (0,) * len(shape)
    return pl.BlockSpec(shape, lambda *_: zeros, pipeline_mode=pl.Buffered(1))


def _dot(a, b):
    return jnp.dot(a, b, preferred_element_type=F32)


def _dot_nt(a, b):
    return lax.dot_general(a, b, (((1,), (1,)), ((), ())), preferred_element_type=F32)


def _dot_tn(a, b):
    return lax.dot_general(a, b, (((0,), (0,)), ((), ())), preferred_element_type=F32)


def _rms(x, gain):
    ms = jnp.mean(x * x, axis=-1, keepdims=True)
    return x * lax.rsqrt(ms + RMS_EPS) * gain


def _split_bf16(x, terms):
    parts = []
    rest = x
    for _ in range(terms):
        part = rest.astype(BF16)
        parts.append(part)
        rest = rest - part.astype(F32)
    return parts


def _head_mean_sq(x, head_avg):
    return _dot((x * x).astype(BF16), head_avg)


def _head_rms(x, head_avg, gain):
    return x * lax.rsqrt(_head_mean_sq(x, head_avg) + RMS_EPS) * gain


def _silu(x):
    return x * jax.nn.sigmoid(x)


def _log_sigmoid(x):
    return jnp.minimum(x, 0.0) - jnp.log(1.0 + jnp.exp(-jnp.abs(x)))


def _head_lane_mask(head, width=GROUP_WIDTH):
    lane = lax.broadcasted_iota(jnp.int32, (1, width), 1)
    return (lane >= head * HEAD_DIM) & (lane < (head + 1) * HEAD_DIM)


def _memory_attention(qn, mem_k, mem_v):
    out, _ = _stacked_head_attention(qn, mem_k, mem_v, None, None)
    return out


def _stacked_head_attention(q, keys, values, bias, masked_below):
    t = q.shape[0]
    masks = [_head_lane_mask(h) for h in range(HEADS_PER_GROUP)]
    qs = jnp.concatenate([jnp.where(m, q, jnp.zeros_like(q)) for m in masks], axis=0)
    s = _dot_nt(qs, keys)
    if bias is not None:
        s = s + bias
    if masked_below is not None:
        key = lax.broadcasted_iota(jnp.int32, (1, keys.shape[0]), 1)
        s = jnp.where(key < masked_below, NEG_INF, s)
    mx = jnp.max(s, axis=-1, keepdims=True)
    e = jnp.exp(s - mx)
    den = jnp.sum(e, axis=-1, keepdims=True)
    o_all = _dot(e.astype(BF16), values) / den
    l_all = mx + jnp.log(den)
    out = jnp.zeros((t, q.shape[1]), F32)
    lse = jnp.zeros((t, q.shape[1]), F32)
    for h, m in enumerate(masks):
        out = out + jnp.where(m, o_all[h * t:(h + 1) * t], 0.0)
        lse = lse + jnp.where(m, l_all[h * t:(h + 1) * t], 0.0)
    return out, lse


def _ffn_body(x_ref, gain_ref, wg_ref, wu_ref, wd_ref, o_ref, h_scr, acc_scr, *, f_chunk, n_chunks):
    x = x_ref[...]
    h_scr[...] = _rms(x, gain_ref[...]).astype(BF16)
    for c in range(n_chunks):
        cols = slice(c * f_chunk, (c + 1) * f_chunk)
        h = h_scr[...]
        act = (_silu(_dot(h, wg_ref[:, cols])) * _dot(h, wu_ref[:, cols])).astype(BF16)
        down = _dot(act, wd_ref[cols, :])
        if c == 0:
            acc_scr[...] = down
        else:
            acc_scr[...] += down
    o_ref[...] = x + 0.5 * acc_scr[...]


def _ffn(x2d, gain, w_gate, w_up, w_down, layer, *, tm):
    n, d = x2d.shape
    f = w_gate.shape[2]
    f_chunk = V7X_MXU_DIM
    row_spec = pl.BlockSpec((tm, d), lambda i: (i, 0))
    layer_weight = lambda rows, cols: pl.BlockSpec((None, rows, cols), lambda i: (layer, 0, 0),
                                                   pipeline_mode=pl.Buffered(1))
    return pl.pallas_call(
        functools.partial(_ffn_body, f_chunk=f_chunk, n_chunks=f // f_chunk),
        grid=(n // tm,),
        in_specs=[row_spec, _resident((1, d)), layer_weight(d, f), layer_weight(d, f), layer_weight(f, d)],
        out_specs=row_spec,
        out_shape=jax.ShapeDtypeStruct((n, d), F32),
        scratch_shapes=[pltpu.VMEM((tm, d), BF16), pltpu.VMEM((tm, d), F32)],
        compiler_params=_compiler_params(1),
        name="ffn",
    )(x2d, gain.reshape(1, d), w_gate, w_up, w_down)


def _memkv_body(mem_ref, gain_ref, w_ref, kgain_ref, havg_ref, k_ref, v_ref):
    mem_n = _rms(mem_ref[0], gain_ref[...]).astype(BF16)
    kv = _dot(mem_n, w_ref[...])
    k_ref[0] = _head_rms(kv[:, :MEM_WIDTH], havg_ref[...], kgain_ref[...]).astype(BF16)
    v_ref[0] = kv[:, MEM_WIDTH:].astype(BF16)


def _memkv(mem, gain, w_mem_kv, k_gain, head_avg):
    b, m, d = mem.shape
    out_spec = pl.BlockSpec((1, m, MEM_WIDTH), lambda i: (i, 0, 0))
    return pl.pallas_call(
        _memkv_body,
        grid=(b,),
        in_specs=[pl.BlockSpec((1, m, d), lambda i: (i, 0, 0)), _resident((1, d)),
                  _resident((d, 2 * MEM_WIDTH)), _resident((1, MEM_WIDTH)),
                  _resident((MEM_WIDTH, MEM_WIDTH))],
        out_specs=[out_spec, out_spec],
        out_shape=[jax.ShapeDtypeStruct((b, m, MEM_WIDTH), BF16)] * 2,
        compiler_params=_compiler_params(1),
        name="memkv",
    )(mem, gain.reshape(1, d), w_mem_kv.astype(BF16), jnp.tile(k_gain, MEM_HEADS).reshape(1, MEM_WIDTH),
      head_avg)


_Q0, _K0, _V0, _O0 = (i * MLSTM_WIDTH for i in range(4))
_QM0 = 4 * MLSTM_WIDTH
_GI0 = _QM0 + MEM_WIDTH
_GF0 = _GI0 + V7X_LANES
A_IN_PAD_WIDTH = _GF0 + V7X_LANES
CONV_TAIL = V7X_SUBLANES


def _mlstm_body(x_ref, gain_ref, win_ref, conv_ref, bi_ref, bf_ref, hgain_ref, wout_ref,
                mk_ref, mv_ref, mqgain_ref, havg_ref, ltri_ref,
                o_ref, c_scr, m_scr, ext_scr, hcat_scr, *, ts, conv_width):
    @pl.when(pl.program_id(0) == 0)
    def _():
        c_scr[...] = jnp.zeros(c_scr.shape, F32)
        m_scr[...] = jnp.zeros(m_scr.shape, F32)
        ext_scr[:, :, 0:CONV_TAIL, :] = jnp.zeros(ext_scr.shape[:2] + (CONV_TAIL, V7X_LANES), F32)

    nb = x_ref.shape[0]
    projections = [_dot(_rms(x_ref[bi], gain_ref[...]).astype(BF16), win_ref[...]) for bi in range(nb)]
    for bi in range(nb):
        _mlstm_tile(projections[bi], bi, conv_ref, bi_ref, bf_ref, hgain_ref, mk_ref, mv_ref,
                    mqgain_ref, havg_ref, ltri_ref, c_scr, m_scr, ext_scr, hcat_scr, ts=ts,
                    conv_width=conv_width)
        rows = slice(bi * ts, (bi + 1) * ts)
        o_ref[bi] = x_ref[bi] + _dot(hcat_scr[rows, :], wout_ref[...])


def _mlstm_tile(p, bi, conv_ref, bi_ref, bf_ref, hgain_ref, mk_ref, mv_ref, mqgain_ref, havg_ref, ltri_ref,
                c_scr, m_scr, ext_scr, hcat_scr, *, ts, conv_width):
    out_rows = slice(bi * ts, (bi + 1) * ts)
    cw = conv_ref[...]
    conv_slabs = []
    for sl in range(ext_scr.shape[1]):
        lanes = slice(sl * V7X_LANES, (sl + 1) * V7X_LANES)
        pre = p[:, lanes]
        ext_scr[bi, sl, CONV_TAIL:CONV_TAIL + ts, :] = pre
        y = cw[conv_width - 1:conv_width, lanes] * pre
        for back in range(1, conv_width):
            tap = conv_width - 1 - back
            y = y + cw[tap:tap + 1, lanes] * ext_scr[bi, sl, CONV_TAIL - back:CONV_TAIL - back + ts, :]
        ext_scr[bi, sl, 0:CONV_TAIL, :] = pre[ts - CONV_TAIL:ts, :]
        conv_slabs.append(_silu(y))
    qk = jnp.concatenate(conv_slabs, axis=-1)
    q = qk[:, :MLSTM_WIDTH].astype(BF16)
    k = qk[:, MLSTM_WIDTH:] * (MLSTM_HEAD_DIM ** -0.5)
    v = p[:, _V0:_O0].astype(BF16)
    o_gate = p[:, _O0:_QM0]
    gate_i = p[:, _GI0:_GF0] + bi_ref[...]
    log_f = _log_sigmoid(p[:, _GF0:A_IN_PAD_WIDTH] + bf_ref[...])

    L = ltri_ref.shape[0]
    row = lax.broadcasted_iota(jnp.int32, (L, L), 0)
    col = lax.broadcasted_iota(jnp.int32, (L, L), 1)
    causal = col <= row
    win_lane = lax.broadcasted_iota(jnp.int32, (1, HEAD_WIN), 1)
    ltri = ltri_ref[...]
    for c in range(ts // L):
        rows = slice(c * L, (c + 1) * L)
        cells = []
        b = sum(_dot(ltri, part) for part in _split_bf16(log_f[rows], 3))
        b_last = b[L - 1:L, :]
        r = gate_i[rows] - b
        a = b_last + r
        m0 = m_scr[bi]
        m_new = jnp.maximum(b_last + m0, jnp.max(a, axis=0, keepdims=True))
        w = jnp.exp(a - m_new)
        decay = jnp.exp(b_last + m0 - m_new)
        g = b + m0
        r_rows = r.T
        for hh in range(MLSTM_HEADS):
            win = slice(HEAD_WIN_START[hh], HEAD_WIN_START[hh] + HEAD_WIN)
            off, den_lane = HEAD_WIN_OFFSET[hh], HEAD_DEN_LANE[hh]
            in_head = (win_lane >= off) & (win_lane < off + MLSTM_HEAD_DIM)
            qh = jnp.where(in_head, q[rows, win], jnp.zeros((), BF16))
            kh = k[rows, win]
            ones_lane = jnp.where(win_lane == den_lane, 1.0, 0.0).astype(BF16)
            vh = jnp.where(in_head, v[rows, win], ones_lane)
            d_log = jnp.where(causal, b[:, hh:hh + 1] + r_rows[hh:hh + 1, :], NEG_INF)
            g_col = g[:, hh:hh + 1]
            m_t = jnp.maximum(g_col, jnp.max(d_log, axis=-1, keepdims=True))
            wmat = jnp.exp(d_log - m_t) * _dot_nt(qh, kh.astype(BF16))
            c0 = c_scr[bi, hh]
            num = _dot(wmat.astype(BF16), vh) + jnp.exp(g_col - m_t) * _dot(qh, c0.astype(BF16))
            den = num[:, den_lane:den_lane + 1]
            cell = num / jnp.maximum(jnp.abs(den), jnp.exp(-m_t))
            cell = jnp.where(in_head, cell, 0.0)
            ms = jnp.sum(cell * cell, axis=-1, keepdims=True) * (1.0 / MLSTM_HEAD_DIM)
            cells.append(cell * lax.rsqrt(ms + RMS_EPS))
            kw = (kh * w[:, hh:hh + 1]).astype(BF16)
            c_scr[bi, hh] = decay[:, hh:hh + 1] * c0 + _dot_tn(kw, vh)
        m_scr[bi] = m_new
        pieces = {}
        for hh, cell in enumerate(cells):
            for half in range(HEAD_WIN // V7X_LANES):
                tile_idx = HEAD_WIN_START[hh] // V7X_LANES + half
                piece = cell[:, half * V7X_LANES:(half + 1) * V7X_LANES]
                pieces[tile_idx] = piece if tile_idx not in pieces else pieces[tile_idx] + piece
        cell_all = jnp.concatenate([pieces[t] for t in range(MLSTM_WIDTH // V7X_LANES)], axis=-1)
        hcat_scr[bi * ts + c * L:bi * ts + (c + 1) * L, :MLSTM_WIDTH] = (
            jax.nn.sigmoid(o_gate[rows]) * cell_all * hgain_ref[...]).astype(BF16)

    qm = _head_rms(p[:, _QM0:_GI0], havg_ref[...], mqgain_ref[...] * ATTN_SCALE).astype(BF16)
    hcat_scr[out_rows, MLSTM_WIDTH:] = _memory_attention(qm, mk_ref[bi], mv_ref[bi]).astype(BF16)


def _pad_lanes(w):
    return jnp.pad(w, [(0, 0)] * (w.ndim - 1) + [(0, V7X_LANES - w.shape[-1])])


def _mlstm_layer(x, gain, w_in, conv_w, gate_bias, h_gain, w_out, mem_k, mem_v, mq_gain,
                 head_avg, ltri, *, ts):
    b, s, d = x.shape
    W, H = MLSTM_WIDTH, MLSTM_HEADS
    w_in_p = jnp.concatenate([w_in[:, :4 * W], w_in[:, 4 * W + 2 * H:], _pad_lanes(w_in[:, 4 * W:4 * W + H]),
                              _pad_lanes(w_in[:, 4 * W + H:4 * W + 2 * H])], axis=1).astype(BF16)
    conv_p = conv_w
    w_out_p = w_out.astype(BF16)
    conv_width = conv_w.shape[0]
    tile = pl.BlockSpec((b, ts, d), lambda j: (0, j, 0))
    return pl.pallas_call(
        functools.partial(_mlstm_body, ts=ts, conv_width=conv_width),
        grid=(s // ts,),
        in_specs=[tile, _resident((1, d)), _resident(w_in_p.shape), _resident(conv_p.shape),
                  _resident((1, V7X_LANES)), _resident((1, V7X_LANES)), _resident((1, W)),
                  _resident(w_out_p.shape), _resident(mem_k.shape), _resident(mem_v.shape),
                  _resident((1, MEM_WIDTH)), _resident(head_avg.shape), _resident(ltri.shape)],
        out_specs=tile,
        out_shape=jax.ShapeDtypeStruct((b, s, d), F32),
        scratch_shapes=[pltpu.VMEM((b, H, HEAD_WIN, HEAD_WIN), F32), pltpu.VMEM((b, 1, V7X_LANES), F32),
                        pltpu.VMEM((b, 2 * W // V7X_LANES, CONV_TAIL + ts, V7X_LANES), F32),
                        pltpu.VMEM((b * ts, W + MEM_WIDTH), BF16)],
        compiler_params=_compiler_params(1),
        name="mlstm_mixer",
    )(x, gain.reshape(1, d), w_in_p, conv_p, _pad_lanes(gate_bias[:H].reshape(1, H)),
      _pad_lanes(gate_bias[H:].reshape(1, H)), h_gain.reshape(1, W), w_out_p,
      mem_k, mem_v, jnp.tile(mq_gain, MEM_HEADS).reshape(1, MEM_WIDTH), head_avg, ltri)


def _proj_body(x_ref, gain_ref, w_ref, hgain_ref, havg_ref, *out_refs, normed_width):
    h = _rms(x_ref[0], gain_ref[...]).astype(BF16)
    y = _dot(h, w_ref[...])
    pieces = []
    for c in range(y.shape[1] // GROUP_WIDTH):
        cols = slice(c * GROUP_WIDTH, (c + 1) * GROUP_WIDTH)
        if c * GROUP_WIDTH < normed_width:
            pieces.append(_head_rms(y[:, cols], havg_ref[...], hgain_ref[:, cols]))
        else:
            pieces.append(y[:, cols])
    start = 0
    for ref in out_refs:
        if len(ref.shape) == 4:
            n = ref.shape[1] * V7X_LANES // GROUP_WIDTH
            for c in range(n):
                for half in range(GROUP_WIDTH // V7X_LANES):
                    lanes = slice(half * V7X_LANES, (half + 1) * V7X_LANES)
                    ref[0, c * (GROUP_WIDTH // V7X_LANES) + half] = pieces[start + c][:, lanes]
        else:
            n = ref.shape[2] // GROUP_WIDTH
            for c in range(n):
                ref[0, :, c * GROUP_WIDTH:(c + 1) * GROUP_WIDTH] = pieces[start + c].astype(BF16)
        start += n


def _proj(x, gain, w, head_gain, head_avg, outputs, normed_width, *, tm, name):
    b, s, d = x.shape
    width = w.shape[1]
    out_specs, out_shapes = [], []
    for ow, slab_major in outputs:
        if slab_major:
            out_specs.append(pl.BlockSpec((1, ow // V7X_LANES, tm, V7X_LANES), lambda i, j: (i, 0, j, 0)))
            out_shapes.append(jax.ShapeDtypeStruct((b, ow // V7X_LANES, s, V7X_LANES), F32))
        else:
            out_specs.append(pl.BlockSpec((1, tm, ow), lambda i, j: (i, j, 0)))
            out_shapes.append(jax.ShapeDtypeStruct((b, s, ow), BF16))
    return pl.pallas_call(
        functools.partial(_proj_body, normed_width=normed_width),
        grid=(b, s // tm),
        in_specs=[pl.BlockSpec((1, tm, d), lambda i, j: (i, j, 0)), _resident((1, d)), _resident((d, width)),
                  _resident((1, normed_width)), _resident(head_avg.shape)],
        out_specs=out_specs,
        out_shape=out_shapes,
        compiler_params=_compiler_params(2),
        name=name,
    )(x, gain.reshape(1, d), w.astype(BF16), head_gain.reshape(1, normed_width), head_avg)


SLABS = GROUP_WIDTH // V7X_LANES


def _dilated_body(q_ref, kc_ref, kp_ref, vc_ref, vp_ref, bias_ref, o_ref, l_ref, *, dilation, nblk):
    A = ATTN_BLOCK
    no_prev_limit = jnp.where(pl.program_id(1) == 0, A, 0)
    bias = bias_ref[0]

    def rows_of(start):
        return pl.ds(start, A, stride=dilation) if dilation > 1 else pl.ds(start, A)

    def load(ref, start):
        rows = rows_of(start)
        return jnp.concatenate([ref[0, sl, rows, :] for sl in range(SLABS)], axis=-1).astype(BF16)

    for r in range(dilation):
        k_cur = load(kp_ref, r)
        v_cur = load(vp_ref, r)
        for i in range(nblk):
            start = i * A * dilation + r
            k_prev, v_prev = k_cur, v_cur
            k_cur, v_cur = load(kc_ref, start), load(vc_ref, start)
            out, lse = _stacked_head_attention(
                load(q_ref, start), jnp.concatenate([k_prev, k_cur], axis=0),
                jnp.concatenate([v_prev, v_cur], axis=0), bias, no_prev_limit if i == 0 else None)
            for sl in range(SLABS):
                lanes = slice(sl * V7X_LANES, (sl + 1) * V7X_LANES)
                o_ref[0, sl, rows_of(start), :] = out[:, lanes]
                l_ref[0, sl, rows_of(start), :] = lse[:, lanes]


def _dilated_attention(q, k, v, bias, group, dilation, *, tile):
    b, _, s, _ = q.shape
    span = ATTN_BLOCK * dilation
    nblk = tile // span
    cur = pl.BlockSpec((1, SLABS, tile, V7X_LANES), lambda bi, n: (bi, group, n, 0))
    prev = pl.BlockSpec((1, SLABS, span, V7X_LANES),
                        lambda bi, n: (bi, group, jnp.maximum(n * nblk - 1, 0), 0))
    out_spec = pl.BlockSpec((1, SLABS, tile, V7X_LANES), lambda bi, n: (bi, 0, n, 0))
    out_shape = jax.ShapeDtypeStruct((b, SLABS, s, V7X_LANES), F32)
    return pl.pallas_call(
        functools.partial(_dilated_body, dilation=dilation, nblk=nblk),
        grid=(b, s // tile),
        in_specs=[cur, cur, prev, cur, prev,
                  pl.BlockSpec((1,) + bias.shape[1:], lambda bi, n: (group, 0, 0), pipeline_mode=pl.Buffered(1))],
        out_specs=[out_spec, out_spec],
        out_shape=[out_shape, out_shape],
        compiler_params=_compiler_params(2),
        name=f"dilated_attn_g{group}",
    )(q, k, k, v, v, bias)


def _bias_body(table_ref, bucket_ref, o_ref):
    g = pl.program_id(0)
    bucket = bucket_ref[0]
    for hh in range(HEADS_PER_GROUP):
        acc = jnp.full(bucket.shape, NEG_INF, F32)
        for bkt in range(NUM_BUCKETS):
            acc = jnp.where(bucket == bkt, table_ref[g * HEADS_PER_GROUP + hh, bkt], acc)
        o_ref[0, hh * ATTN_BLOCK:(hh + 1) * ATTN_BLOCK, :] = acc


def _dilated_bias(rel_bias):
    A = ATTN_BLOCK
    m_off = np.arange(A)[:, None] + A - np.arange(2 * A)[None, :]
    band = (m_off >= 0) & (m_off <= A)
    max_exact = NUM_BUCKETS // 2
    buckets = []
    for _, dilation in DILATED_GROUPS:
        dist = np.clip(m_off, 0, A) * dilation
        large = max_exact + (np.log(np.maximum(dist, 1).astype(np.float32) / max_exact)
                             / math.log(MAX_DISTANCE / max_exact) * (NUM_BUCKETS - max_exact)).astype(np.int32)
        bucket = np.where(dist < max_exact, dist, np.minimum(large, NUM_BUCKETS - 1))
        buckets.append(np.where(band, bucket, -1))
    buckets = jnp.asarray(np.stack(buckets), jnp.int32)
    n_groups = len(DILATED_GROUPS)
    return pl.pallas_call(
        _bias_body,
        grid=(n_groups,),
        in_specs=[pl.BlockSpec(memory_space=pltpu.SMEM), pl.BlockSpec((1, A, 2 * A), lambda g: (g, 0, 0))],
        out_specs=pl.BlockSpec((1, HEADS_PER_GROUP * A, 2 * A), lambda g: (g, 0, 0)),
        out_shape=jax.ShapeDtypeStruct((n_groups, HEADS_PER_GROUP * A, 2 * A), F32),
        compiler_params=_compiler_params(1),
        name="dilated_bias",
    )(rel_bias.astype(F32).T, buckets)


def _mixout_body(x_ref, o0_ref, o1_ref, o2_ref, l0_ref, l1_ref, l2_ref, qm_ref, mk_ref, mv_ref, wout_ref,
                 out_ref, hcat_scr):
    wide = lambda ref: jnp.concatenate([ref[0, sl] for sl in range(SLABS)], axis=-1)
    lses = [wide(l0_ref), wide(l1_ref), wide(l2_ref)]
    mx = jnp.maximum(jnp.maximum(lses[0], lses[1]), lses[2])
    es = [jnp.exp(l - mx) for l in lses]
    inv = 1.0 / (es[0] + es[1] + es[2])
    for g, o_ref in enumerate((o0_ref, o1_ref, o2_ref)):
        hcat_scr[:, g * GROUP_WIDTH:(g + 1) * GROUP_WIDTH] = (wide(o_ref) * (es[g] * inv)).astype(BF16)
    hcat_scr[:, DIL_WIDTH:] = _memory_attention(qm_ref[0], mk_ref[0], mv_ref[0]).astype(BF16)
    out_ref[0] = x_ref[0] + _dot(hcat_scr[...], wout_ref[...])


def _mixout(x, outs, lses, q_mem, mem_k, mem_v, w_out, *, ts):
    b, s, d = x.shape
    tile = lambda w: pl.BlockSpec((1, ts, w), lambda i, j: (i, j, 0))
    slab_tile = pl.BlockSpec((1, SLABS, ts, V7X_LANES), lambda i, j: (i, 0, j, 0))
    mem_spec = pl.BlockSpec((1,) + mem_k.shape[1:], lambda i, j: (i, 0, 0))
    return pl.pallas_call(
        _mixout_body,
        grid=(b, s // ts),
        in_specs=[tile(d)] + [slab_tile] * 6 + [tile(MEM_WIDTH), mem_spec, mem_spec, _resident(w_out.shape)],
        out_specs=tile(d),
        out_shape=jax.ShapeDtypeStruct((b, s, d), F32),
        scratch_shapes=[pltpu.VMEM((ts, DIL_WIDTH + MEM_WIDTH), BF16)],
        compiler_params=_compiler_params(2),
        name="dilated_mixer_out",
    )(x, *outs, *lses, q_mem, mem_k, mem_v, w_out.astype(BF16))


def kernel(x, mem, ffn1_norm, ffn1_w_gate, ffn1_w_up, ffn1_w_down, ffn2_norm, ffn2_w_gate, ffn2_w_up, ffn2_w_down, mix_norm, mem_norm, w_mem_kv, mem_q_norm, mem_k_norm, a_w_in, a_conv, a_gate_bias, a_h_norm, a_w_out, b_w_q, b_q_norm, b_w_out, kv_norm, w_kv, kv_k_norm, rel_bias):
    b, s, d = x.shape
    n = b * s
    tm = 1024
    head_id = np.arange(GROUP_WIDTH) // HEAD_DIM
    head_avg = jnp.asarray((head_id[:, None] == head_id[None, :]) / HEAD_DIM, BF16)
    mlstm_chunk = V7X_MXU_DIM
    ltri = jnp.asarray(np.tril(np.ones((mlstm_chunk, mlstm_chunk))), BF16)
    flat = lambda t: t.reshape(n, t.shape[-1])
    ffn1_w_gate, ffn1_w_up, ffn1_w_down, ffn2_w_gate, ffn2_w_up, ffn2_w_down = (
        w.astype(BF16) for w in (ffn1_w_gate, ffn1_w_up, ffn1_w_down, ffn2_w_gate, ffn2_w_up, ffn2_w_down))
    ffn = lambda t, norm, wg, wu, wd, layer: _ffn(flat(t), norm[layer], wg, wu, wd, layer,
                                                   tm=tm).reshape(b, s, d)

    x = ffn(x, ffn1_norm, ffn1_w_gate, ffn1_w_up, ffn1_w_down, 0)
    mem_k, mem_v = _memkv(mem, mem_norm[0], w_mem_kv[0], mem_k_norm[0], head_avg)
    x = _mlstm_layer(x, mix_norm[0], a_w_in[0], a_conv[0], a_gate_bias[0], a_h_norm[0], a_w_out[0],
                     mem_k, mem_v, mem_q_norm[0], head_avg, ltri, ts=256)
    x = ffn(x, ffn2_norm, ffn2_w_gate, ffn2_w_up, ffn2_w_down, 0)

    k_sh, v_sh = _proj(x, kv_norm, w_kv, jnp.tile(kv_k_norm, DIL_WIDTH // HEAD_DIM), head_avg,
                       [(DIL_WIDTH, True), (DIL_WIDTH, True)], DIL_WIDTH, tm=tm, name="shared_kv_proj")

    x = ffn(x, ffn1_norm, ffn1_w_gate, ffn1_w_up, ffn1_w_down, 1)
    q_gain = jnp.concatenate([jnp.tile(b_q_norm[0], DIL_WIDTH // HEAD_DIM),
                              jnp.tile(mem_q_norm[1], MEM_HEADS)]) * ATTN_SCALE
    q_d, q_mem = _proj(x, mix_norm[1], b_w_q[0], q_gain, head_avg, [(DIL_WIDTH, True), (MEM_WIDTH, False)],
                       DIL_WIDTH + MEM_WIDTH, tm=tm, name="dilated_q_proj")
    mem_k, mem_v = _memkv(mem, mem_norm[1], w_mem_kv[1], mem_k_norm[1], head_avg)
    bias = _dilated_bias(rel_bias)
    attn_tile = max(window for window, _ in DILATED_GROUPS)
    outs, lses = [], []
    for g, (window, dilation) in enumerate(DILATED_GROUPS):
        assert window // dilation == ATTN_BLOCK and s % attn_tile == 0 and attn_tile % window == 0
        o_g, l_g = _dilated_attention(q_d, k_sh, v_sh, bias, g, dilation, tile=attn_tile)
        outs.append(o_g)
        lses.append(l_g)
    x = _mixout(x, outs, lses, q_mem, mem_k, mem_v, b_w_out[0], ts=1024)
    x = ffn(x, ffn2_norm, ffn2_w_gate, ffn2_w_up, ffn2_w_down, 1)
    return x
```

```python
import functools
import math

import numpy as np
import jax
import jax.numpy as jnp
from jax import lax
from jax.experimental import pallas as pl
from jax.experimental.pallas import tpu as pltpu

F32 = jnp.float32
BF16 = jnp.bfloat16

HEAD_DIM = 64
MEM_HEADS = 4
MEM_WIDTH = MEM_HEADS * HEAD_DIM
MLSTM_HEADS = 4
MLSTM_HEAD_DIM = 192
MLSTM_WIDTH = MLSTM_HEADS * MLSTM_HEAD_DIM
DILATED_GROUPS = ((128, 1), (512, 4), (2048, 16))
HEADS_PER_GROUP = 4
GROUP_WIDTH = HEADS_PER_GROUP * HEAD_DIM
DIL_WIDTH = GROUP_WIDTH * len(DILATED_GROUPS)
NUM_BUCKETS = 32
MAX_DISTANCE = 2048
RMS_EPS = 1e-6
NEG_INF = -1e30
ATTN_SCALE = HEAD_DIM ** -0.5

V7X_LANES = 128
V7X_SUBLANES = 8
V7X_MXU_DIM = 256
V7X_VMEM_BYTES = 64 * 1024 * 1024
VMEM_LIMIT_BYTES = V7X_VMEM_BYTES - 8 * 1024 * 1024

HEAD_WIN = V7X_MXU_DIM
HEAD_WIN_START = tuple((h * MLSTM_HEAD_DIM) // V7X_LANES * V7X_LANES for h in range(MLSTM_HEADS))
HEAD_WIN_OFFSET = tuple(h * MLSTM_HEAD_DIM - s for h, s in enumerate(HEAD_WIN_START))
assert all(o + MLSTM_HEAD_DIM <= HEAD_WIN for o in HEAD_WIN_OFFSET) and MLSTM_HEAD_DIM < HEAD_WIN
HEAD_DEN_LANE = tuple((o + MLSTM_HEAD_DIM) % HEAD_WIN for o in HEAD_WIN_OFFSET)
ATTN_BLOCK = 128


def _compiler_params(n_axes):
    return pltpu.CompilerParams(dimension_semantics=("arbitrary",) * n_axes,
                                vmem_limit_bytes=VMEM_LIMIT_BYTES)


def _resident(shape):
    zeros = (0,) * len(shape)
    return pl.BlockSpec(shape, lambda *_: zeros, pipeline_mode=pl.Buffered(1))


def _dot(a, b):
    return jnp.dot(a, b, preferred_element_type=F32)


def _dot_nt(a, b):
    return lax.dot_general(a, b, (((1,), (1,)), ((), ())), preferred_element_type=F32)


def _dot_tn(a, b):
    return lax.dot_general(a, b, (((0,), (0,)), ((), ())), preferred_element_type=F32)


def _rms(x, gain):
    ms = jnp.mean(x * x, axis=-1, keepdims=True)
    return x * lax.rsqrt(ms + RMS_EPS) * gain


def _split_bf16(x, terms):
    parts = []
    rest = x
    for _ in range(terms):
        part = rest.astype(BF16)
        parts.append(part)
        rest = rest - part.astype(F32)
    return parts


def _head_mean_sq(x, head_avg):
    return _dot((x * x).astype(BF16), head_avg)


def _head_rms(x, head_avg, gain):
    return x * lax.rsqrt(_head_mean_sq(x, head_avg) + RMS_EPS) * gain


def _silu(x):
    return x * jax.nn.sigmoid(x)


def _log_sigmoid(x):
    return jnp.minimum(x, 0.0) - jnp.log(1.0 + jnp.exp(-jnp.abs(x)))


def _head_lane_mask(head, width=GROUP_WIDTH):
    lane = lax.broadcasted_iota(jnp.int32, (1, width), 1)
    return (lane >= head * HEAD_DIM) & (lane < (head + 1) * HEAD_DIM)


def _memory_attention(qn, mem_k, mem_v):
    out, _ = _stacked_head_attention(qn, mem_k, mem_v, None, None)
    return out


def _stacked_head_attention(q, keys, values, bias, masked_below):
    t = q.shape[0]
    masks = [_head_lane_mask(h) for h in range(HEADS_PER_GROUP)]
    qs = jnp.concatenate([jnp.where(m, q, jnp.zeros_like(q)) for m in masks], axis=0)
    s = _dot_nt(qs, keys)
    if bias is not None:
        s = s + bias
    if masked_below is not None:
        key = lax.broadcasted_iota(jnp.int32, (1, keys.shape[0]), 1)
        s = jnp.where(key < masked_below, NEG_INF, s)
    mx = jnp.max(s, axis=-1, keepdims=True)
    e = jnp.exp(s - mx)
    den = jnp.sum(e, axis=-1, keepdims=True)
    o_all = _dot(e.astype(BF16), values) / den
    l_all = mx + jnp.log(den)
    out = jnp.zeros((t, q.shape[1]), F32)
    lse = jnp.zeros((t, q.shape[1]), F32)
    for h, m in enumerate(masks):
        out = out + jnp.where(m, o_all[h * t:(h + 1) * t], 0.0)
        lse = lse + jnp.where(m, l_all[h * t:(h + 1) * t], 0.0)
    return out, lse


def _ffn_body(x_ref, gain_ref, wg_ref, wu_ref, wd_ref, o_ref, h_scr, acc_scr, *, f_chunk, n_chunks):
    x = x_ref[...]
    h_scr[...] = _rms(x, gain_ref[...]).astype(BF16)
    for c in range(n_chunks):
        cols = slice(c * f_chunk, (c + 1) * f_chunk)
        h = h_scr[...]
        act = (_silu(_dot(h, wg_ref[:, cols])) * _dot(h, wu_ref[:, cols])).astype(BF16)
        down = _dot(act, wd_ref[cols, :])
        if c == 0:
            acc_scr[...] = down
        else:
            acc_scr[...] += down
    o_ref[...] = x + 0.5 * acc_scr[...]


def _ffn(x2d, gain, w_gate, w_up, w_down, layer, *, tm):
    n, d = x2d.shape
    f = w_gate.shape[2]
    f_chunk = V7X_MXU_DIM
    row_spec = pl.BlockSpec((tm, d), lambda i: (i, 0))
    layer_weight = lambda rows, cols: pl.BlockSpec((None, rows, cols), lambda i: (layer, 0, 0),
                                                   pipeline_mode=pl.Buffered(1))
    return pl.pallas_call(
        functools.partial(_ffn_body, f_chunk=f_chunk, n_chunks=f // f_chunk),
        grid=(n // tm,),
        in_specs=[row_spec, _resident((1, d)), layer_weight(d, f), layer_weight(d, f), layer_weight(f, d)],
        out_specs=row_spec,
        out_shape=jax.ShapeDtypeStruct((n, d), F32),
        scratch_shapes=[pltpu.VMEM((tm, d), BF16), pltpu.VMEM((tm, d), F32)],
        compiler_params=_compiler_params(1),
        name="ffn",
    )(x2d, gain.reshape(1, d), w_gate, w_up, w_down)


def _memkv_body(mem_ref, gain_ref, w_ref, kgain_ref, havg_ref, k_ref, v_ref):
    mem_n = _rms(mem_ref[0], gain_ref[...]).astype(BF16)
    kv = _dot(mem_n, w_ref[...])
    k_ref[0] = _head_rms(kv[:, :MEM_WIDTH], havg_ref[...], kgain_ref[...]).astype(BF16)
    v_ref[0] = kv[:, MEM_WIDTH:].astype(BF16)


def _memkv(mem, gain, w_mem_kv, k_gain, head_avg):
    b, m, d = mem.shape
    out_spec = pl.BlockSpec((1, m, MEM_WIDTH), lambda i: (i, 0, 0))
    return pl.pallas_call(
        _memkv_body,
        grid=(b,),
        in_specs=[pl.BlockSpec((1, m, d), lambda i: (i, 0, 0)), _resident((1, d)),
                  _resident((d, 2 * MEM_WIDTH)), _resident((1, MEM_WIDTH)),
                  _resident((MEM_WIDTH, MEM_WIDTH))],
        out_specs=[out_spec, out_spec],
        out_shape=[jax.ShapeDtypeStruct((b, m, MEM_WIDTH), BF16)] * 2,
        compiler_params=_compiler_params(1),
        name="memkv",
    )(mem, gain.reshape(1, d), w_mem_kv.astype(BF16), jnp.tile(k_gain, MEM_HEADS).reshape(1, MEM_WIDTH),
      head_avg)


_Q0, _K0, _V0, _O0 = (i * MLSTM_WIDTH for i in range(4))
_QM0 = 4 * MLSTM_WIDTH
_GI0 = _QM0 + MEM_WIDTH
_GF0 = _GI0 + V7X_LANES
A_IN_PAD_WIDTH = _GF0 + V7X_LANES
CONV_TAIL = V7X_SUBLANES


def _mlstm_body(x_ref, gain_ref, win_ref, conv_ref, bi_ref, bf_ref, hgain_ref, wout_ref,
                mk_ref, mv_ref, mqgain_ref, havg_ref, ltri_ref,
                o_ref, c_scr, m_scr, ext_scr, hcat_scr, *, ts, conv_width):
    @pl.when(pl.program_id(0) == 0)
    def _():
        c_scr[...] = jnp.zeros(c_scr.shape, F32)
        m_scr[...] = jnp.zeros(m_scr.shape, F32)
        ext_scr[:, :, 0:CONV_TAIL, :] = jnp.zeros(ext_scr.shape[:2] + (CONV_TAIL, V7X_LANES), F32)

    nb = x_ref.shape[0]
    projections = [_dot(_rms(x_ref[bi], gain_ref[...]).astype(BF16), win_ref[...]) for bi in range(nb)]
    for bi in range(nb):
        _mlstm_tile(projections[bi], bi, conv_ref, bi_ref, bf_ref, hgain_ref, mk_ref, mv_ref,
                    mqgain_ref, havg_ref, ltri_ref, c_scr, m_scr, ext_scr, hcat_scr, ts=ts,
                    conv_width=conv_width)
        rows = slice(bi * ts, (bi + 1) * ts)
        o_ref[bi] = x_ref[bi] + _dot(hcat_scr[rows, :], wout_ref[...])


def _mlstm_tile(p, bi, conv_ref, bi_ref, bf_ref, hgain_ref, mk_ref, mv_ref, mqgain_ref, havg_ref, ltri_ref,
                c_scr, m_scr, ext_scr, hcat_scr, *, ts, conv_width):
    out_rows = slice(bi * ts, (bi + 1) * ts)
    cw = conv_ref[...]
    conv_slabs = []
    for sl in range(ext_scr.shape[1]):
        lanes = slice(sl * V7X_LANES, (sl + 1) * V7X_LANES)
        pre = p[:, lanes]
        ext_scr[bi, sl, CONV_TAIL:CONV_TAIL + ts, :] = pre
        y = cw[conv_width - 1:conv_width, lanes] * pre
        for back in range(1, conv_width):
            tap = conv_width - 1 - back
            y = y + cw[tap:tap + 1, lanes] * ext_scr[bi, sl, CONV_TAIL - back:CONV_TAIL - back + ts, :]
        ext_scr[bi, sl, 0:CONV_TAIL, :] = pre[ts - CONV_TAIL:ts, :]
        conv_slabs.append(_silu(y))
    qk = jnp.concatenate(conv_slabs, axis=-1)
    q = qk[:, :MLSTM_WIDTH].astype(BF16)
    k = qk[:, MLSTM_WIDTH:] * (MLSTM_HEAD_DIM ** -0.5)
    v = p[:, _V0:_O0].astype(BF16)
    o_gate = p[:, _O0:_QM0]
    gate_i = p[:, _GI0:_GF0] + bi_ref[...]
    log_f = _log_sigmoid(p[:, _GF0:A_IN_PAD_WIDTH] + bf_ref[...])

    L = ltri_ref.shape[0]
    row = lax.broadcasted_iota(jnp.int32, (L, L), 0)
    col = lax.broadcasted_iota(jnp.int32, (L, L), 1)
    causal = col <= row
    win_lane = lax.broadcasted_iota(jnp.int32, (1, HEAD_WIN), 1)
    ltri = ltri_ref[...]
    for c in range(ts // L):
        rows = slice(c * L, (c + 1) * L)
        cells = []
        b = sum(_dot(ltri, part) for part in _split_bf16(log_f[rows], 3))
        b_last = b[L - 1:L, :]
        r = gate_i[rows] - b
        a = b_last + r
        m0 = m_scr[bi]
        m_new = jnp.maximum(b_last + m0, jnp.max(a, axis=0, keepdims=True))
        w = jnp.exp(a - m_new)
        decay = jnp.exp(b_last + m0 - m_new)
        g = b + m0
        r_rows = r.T
        for hh in range(MLSTM_HEADS):
            win = slice(HEAD_WIN_START[hh], HEAD_WIN_START[hh] + HEAD_WIN)
            off, den_lane = HEAD_WIN_OFFSET[hh], HEAD_DEN_LANE[hh]
            in_head = (win_lane >= off) & (win_lane < off + MLSTM_HEAD_DIM)
            qh = jnp.where(in_head, q[rows, win], jnp.zeros((), BF16))
            kh = k[rows, win]
            ones_lane = jnp.where(win_lane == den_lane, 1.0, 0.0).astype(BF16)
            vh = jnp.where(in_head, v[rows, win], ones_lane)
            d_log = jnp.where(causal, b[:, hh:hh + 1] + r_rows[hh:hh + 1, :], NEG_INF)
            g_col = g[:, hh:hh + 1]
            m_t = jnp.maximum(g_col, jnp.max(d_log, axis=-1, keepdims=True))
            wmat = jnp.exp(d_log - m_t) * _dot_nt(qh, kh.astype(BF16))
            c0 = c_scr[bi, hh]
            num = _dot(wmat.astype(BF16), vh) + jnp.exp(g_col - m_t) * _dot(qh, c0.astype(BF16))
            den = num[:, den_lane:den_lane + 1]
            cell = num / jnp.maximum(jnp.abs(den), jnp.exp(-m_t))
            cell = jnp.where(in_head, cell, 0.0)
            ms = jnp.sum(cell * cell, axis=-1, keepdims=True) * (1.0 / MLSTM_HEAD_DIM)
            cells.append(cell * lax.rsqrt(ms + RMS_EPS))
            kw = (kh * w[:, hh:hh + 1]).astype(BF16)
            c_scr[bi, hh] = decay[:, hh:hh + 1] * c0 + _dot_tn(kw, vh)
        m_scr[bi] = m_new
        pieces = {}
        for hh, cell in enumerate(cells):
            for half in range(HEAD_WIN // V7X_LANES):
                tile_idx = HEAD_WIN_START[hh] // V7X_LANES + half
                piece = cell[:, half * V7X_LANES:(half + 1) * V7X_LANES]
                pieces[tile_idx] = piece if tile_idx not in pieces else pieces[tile_idx] + piece
        cell_all = jnp.concatenate([pieces[t] for t in range(MLSTM_WIDTH // V7X_LANES)], axis=-1)
        hcat_scr[bi * ts + c * L:bi * ts + (c + 1) * L, :MLSTM_WIDTH] = (
            jax.nn.sigmoid(o_gate[rows]) * cell_all * hgain_ref[...]).astype(BF16)

    qm = _head_rms(p[:, _QM0:_GI0], havg_ref[...], mqgain_ref[...] * ATTN_SCALE).astype(BF16)
    hcat_scr[out_rows, MLSTM_WIDTH:] = _memory_attention(qm, mk_ref[bi], mv_ref[bi]).astype(BF16)


def _pad_lanes(w):
    return jnp.pad(w, [(0, 0)] * (w.ndim - 1) + [(0, V7X_LANES - w.shape[-1])])


def _mlstm_layer(x, gain, w_in, conv_w, gate_bias, h_gain, w_out, mem_k, mem_v, mq_gain,
                 head_avg, ltri, *, ts):
    b, s, d = x.shape
    W, H = MLSTM_WIDTH, MLSTM_HEADS
    w_in_p = jnp.concatenate([w_in[:, :4 * W], w_in[:, 4 * W + 2 * H:], _pad_lanes(w_in[:, 4 * W:4 * W + H]),
                              _pad_lanes(w_in[:, 4 * W + H:4 * W + 2 * H])], axis=1).astype(BF16)
    conv_p = conv_w
    w_out_p = w_out.astype(BF16)
    conv_width = conv_w.shape[0]
    tile = pl.BlockSpec((b, ts, d), lambda j: (0, j, 0))
    return pl.pallas_call(
        functools.partial(_mlstm_body, ts=ts, conv_width=conv_width),
        grid=(s // ts,),
        in_specs=[tile, _resident((1, d)), _resident(w_in_p.shape), _resident(conv_p.shape),
                  _resident((1, V7X_LANES)), _resident((1, V7X_LANES)), _resident((1, W)),
                  _resident(w_out_p.shape), _resident(mem_k.shape), _resident(mem_v.shape),
                  _resident((1, MEM_WIDTH)), _resident(head_avg.shape), _resident(ltri.shape)],
        out_specs=tile,
        out_shape=jax.ShapeDtypeStruct((b, s, d), F32),
        scratch_shapes=[pltpu.VMEM((b, H, HEAD_WIN, HEAD_WIN), F32), pltpu.VMEM((b, 1, V7X_LANES), F32),
                        pltpu.VMEM((b, 2 * W // V7X_LANES, CONV_TAIL + ts, V7X_LANES), F32),
                        pltpu.VMEM((b * ts, W + MEM_WIDTH), BF16)],
        compiler_params=_compiler_params(1),
        name="mlstm_mixer",
    )(x, gain.reshape(1, d), w_in_p, conv_p, _pad_lanes(gate_bias[:H].reshape(1, H)),
      _pad_lanes(gate_bias[H:].reshape(1, H)), h_gain.reshape(1, W), w_out_p,
      mem_k, mem_v, jnp.tile(mq_gain, MEM_HEADS).reshape(1, MEM_WIDTH), head_avg, ltri)


def _proj_body(x_ref, gain_ref, w_ref, hgain_ref, havg_ref, *out_refs, normed_width):
    h = _rms(x_ref[0], gain_ref[...]).astype(BF16)
    y = _dot(h, w_ref[...])
    pieces = []
    for c in range(y.shape[1] // GROUP_WIDTH):
        cols = slice(c * GROUP_WIDTH, (c + 1) * GROUP_WIDTH)
        if c * GROUP_WIDTH < normed_width:
            pieces.append(_head_rms(y[:, cols], havg_ref[...], hgain_ref[:, cols]))
        else:
            pieces.append(y[:, cols])
    start = 0
    for ref in out_refs:
        if len(ref.shape) == 4:
            n = ref.shape[1] * V7X_LANES // GROUP_WIDTH
            for c in range(n):
                for half in range(GROUP_WIDTH // V7X_LANES):
                    lanes = slice(half * V7X_LANES, (half + 1) * V7X_LANES)
                    ref[0, c * (GROUP_WIDTH // V7X_LANES) + half] = pieces[start + c][:, lanes]
        else:
            n = ref.shape[2] // GROUP_WIDTH
            for c in range(n):
                ref[0, :, c * GROUP_WIDTH:(c + 1) * GROUP_WIDTH] = pieces[start + c].astype(BF16)
        start += n


def _proj(x, gain, w, head_gain, head_avg, outputs, normed_width, *, tm, name):
    b, s, d = x.shape
    width = w.shape[1]
    out_specs, out_shapes = [], []
    for ow, slab_major in outputs:
        if slab_major:
            out_specs.append(pl.BlockSpec((1, ow // V7X_LANES, tm, V7X_LANES), lambda i, j: (i, 0, j, 0)))
            out_shapes.append(jax.ShapeDtypeStruct((b, ow // V7X_LANES, s, V7X_LANES), F32))
        else:
            out_specs.append(pl.BlockSpec((1, tm, ow), lambda i, j: (i, j, 0)))
            out_shapes.append(jax.ShapeDtypeStruct((b, s, ow), BF16))
    return pl.pallas_call(
        functools.partial(_proj_body, normed_width=normed_width),
        grid=(b, s // tm),
        in_specs=[pl.BlockSpec((1, tm, d), lambda i, j: (i, j, 0)), _resident((1, d)), _resident((d, width)),
                  _resident((1, normed_width)), _resident(head_avg.shape)],
        out_specs=out_specs,
        out_shape=out_shapes,
        compiler_params=_compiler_params(2),
        name=name,
    )(x, gain.reshape(1, d), w.astype(BF16), head_gain.reshape(1, normed_width), head_avg)


SLABS = GROUP_WIDTH // V7X_LANES


PRE_STRIDE = 4


def _dilated_body(q_ref, k_ref, v_ref, bias_ref, o_ref, l_ref, kcarry_scr, vcarry_scr, *plane_scr,
                  dilation, nblk):
    A = ATTN_BLOCK
    tile = dilation * nblk * A
    first = pl.program_id(1) == 0
    no_prev_limit = jnp.where(first, A, 0)
    bias = bias_ref[0]

    @pl.when(first)
    def _():
        kcarry_scr[...] = jnp.zeros(kcarry_scr.shape, BF16)
        vcarry_scr[...] = jnp.zeros(vcarry_scr.shape, BF16)

    pre = PRE_STRIDE if dilation > PRE_STRIDE else 1
    post = dilation // pre
    if pre > 1:
        planes, = plane_scr
        for t, ref in enumerate((q_ref, k_ref, v_ref)):
            for sl in range(SLABS):
                for a in range(pre):
                    planes[(t * SLABS + sl) * pre + a] = ref[0, sl, pl.ds(a, tile // pre, stride=pre), :]

    def load(t, ref, r, i):
        if pre > 1:
            rows = pl.ds(i * A * post + r // pre, A, stride=post)
            parts = [planes[(t * SLABS + sl) * pre + r % pre, rows, :] for sl in range(SLABS)]
        else:
            parts = [ref[0, sl, rows_of(r, i), :] for sl in range(SLABS)]
        return jnp.concatenate(parts, axis=-1).astype(BF16)

    def rows_of(r, i):
        start = i * A * dilation + r
        return pl.ds(start, A, stride=dilation) if dilation > 1 else pl.ds(start, A)

    for r in range(dilation):
        carry_rows = slice(r * A, (r + 1) * A)
        k_cur, v_cur = kcarry_scr[carry_rows, :], vcarry_scr[carry_rows, :]
        for i in range(nblk):
            k_prev, v_prev = k_cur, v_cur
            k_cur, v_cur = load(1, k_ref, r, i), load(2, v_ref, r, i)
            out, lse = _stacked_head_attention(
                load(0, q_ref, r, i), jnp.concatenate([k_prev, k_cur], axis=0),
                jnp.concatenate([v_prev, v_cur], axis=0), bias, no_prev_limit if i == 0 else None)
            for sl in range(SLABS):
                lanes = slice(sl * V7X_LANES, (sl + 1) * V7X_LANES)
                o_ref[0, sl, rows_of(r, i), :] = out[:, lanes]
                l_ref[0, sl, rows_of(r, i), :] = lse[:, lanes]
        kcarry_scr[carry_rows, :] = k_cur
        vcarry_scr[carry_rows, :] = v_cur


def _dilated_attention(q, k, v, bias, group, dilation, *, tile):
    b, _, s, _ = q.shape
    span = ATTN_BLOCK * dilation
    nblk = tile // span
    cur = pl.BlockSpec((1, SLABS, tile, V7X_LANES), lambda bi, n: (bi, group, n, 0))
    out_spec = pl.BlockSpec((1, SLABS, tile, V7X_LANES), lambda bi, n: (bi, 0, n, 0))
    out_shape = jax.ShapeDtypeStruct((b, SLABS, s, V7X_LANES), F32)
    carry = pltpu.VMEM((span, GROUP_WIDTH), BF16)
    scratch = [carry, carry]
    if dilation > PRE_STRIDE:
        scratch.append(pltpu.VMEM((3 * SLABS * PRE_STRIDE, tile // PRE_STRIDE, V7X_LANES), F32))
    return pl.pallas_call(
        functools.partial(_dilated_body, dilation=dilation, nblk=nblk),
        grid=(b, s // tile),
        in_specs=[cur, cur, cur,
                  pl.BlockSpec((1,) + bias.shape[1:], lambda bi, n: (group, 0, 0), pipeline_mode=pl.Buffered(1))],
        out_specs=[out_spec, out_spec],
        out_shape=[out_shape, out_shape],
        scratch_shapes=scratch,
        compiler_params=_compiler_params(2),
        name=f"dilated_attn_g{group}",
    )(q, k, v, bias)


def _bias_body(table_ref, bucket_ref, o_ref):
    g = pl.program_id(0)
    bucket = bucket_ref[0]
    for hh in range(HEADS_PER_GROUP):
        acc = jnp.full(bucket.shape, NEG_INF, F32)
        for bkt in range(NUM_BUCKETS):
            acc = jnp.where(bucket == bkt, table_ref[g * HEADS_PER_GROUP + hh, bkt], acc)
        o_ref[0, hh * ATTN_BLOCK:(hh + 1) * ATTN_BLOCK, :] = acc


def _dilated_bias(rel_bias):
    A = ATTN_BLOCK
    m_off = np.arange(A)[:, None] + A - np.arange(2 * A)[None, :]
    band = (m_off >= 0) & (m_off <= A)
    max_exact = NUM_BUCKETS // 2
    buckets = []
    for _, dilation in DILATED_GROUPS:
        dist = np.clip(m_off, 0, A) * dilation
        large = max_exact + (np.log(np.maximum(dist, 1).astype(np.float32) / max_exact)
                             / math.log(MAX_DISTANCE / max_exact) * (NUM_BUCKETS - max_exact)).astype(np.int32)
        bucket = np.where(dist < max_exact, dist, np.minimum(large, NUM_BUCKETS - 1))
        buckets.append(np.where(band, bucket, -1))
    buckets = jnp.asarray(np.stack(buckets), jnp.int32)
    n_groups = len(DILATED_GROUPS)
    return pl.pallas_call(
        _bias_body,
        grid=(n_groups,),
        in_specs=[pl.BlockSpec(memory_space=pltpu.SMEM), pl.BlockSpec((1, A, 2 * A), lambda g: (g, 0, 0))],
        out_specs=pl.BlockSpec((1, HEADS_PER_GROUP * A, 2 * A), lambda g: (g, 0, 0)),
        out_shape=jax.ShapeDtypeStruct((n_groups, HEADS_PER_GROUP * A, 2 * A), F32),
        compiler_params=_compiler_params(1),
        name="dilated_bias",
    )(rel_bias.astype(F32).T, buckets)


def _mixout_body(x_ref, o0_ref, o1_ref, o2_ref, l0_ref, l1_ref, l2_ref, qm_ref, mk_ref, mv_ref, wout_ref,
                 out_ref, hcat_scr):
    wide = lambda ref: jnp.concatenate([ref[0, sl] for sl in range(SLABS)], axis=-1)
    lses = [wide(l0_ref), wide(l1_ref), wide(l2_ref)]
    mx = jnp.maximum(jnp.maximum(lses[0], lses[1]), lses[2])
    es = [jnp.exp(l - mx) for l in lses]
    inv = 1.0 / (es[0] + es[1] + es[2])
    for g, o_ref in enumerate((o0_ref, o1_ref, o2_ref)):
        hcat_scr[:, g * GROUP_WIDTH:(g + 1) * GROUP_WIDTH] = (wide(o_ref) * (es[g] * inv)).astype(BF16)
    hcat_scr[:, DIL_WIDTH:] = _memory_attention(qm_ref[0], mk_ref[0], mv_ref[0]).astype(BF16)
    out_ref[0] = x_ref[0] + _dot(hcat_scr[...], wout_ref[...])


def _mixout(x, outs, lses, q_mem, mem_k, mem_v, w_out, *, ts):
    b, s, d = x.shape
    tile = lambda w: pl.BlockSpec((1, ts, w), lambda i, j: (i, j, 0))
    slab_tile = pl.BlockSpec((1, SLABS, ts, V7X_LANES), lambda i, j: (i, 0, j, 0))
    mem_spec = pl.BlockSpec((1,) + mem_k.shape[1:], lambda i, j: (i, 0, 0))
    return pl.pallas_call(
        _mixout_body,
        grid=(b, s // ts),
        in_specs=[tile(d)] + [slab_tile] * 6 + [tile(MEM_WIDTH), mem_spec, mem_spec, _resident(w_out.shape)],
        out_specs=tile(d),
        out_shape=jax.ShapeDtypeStruct((b, s, d), F32),
        scratch_shapes=[pltpu.VMEM((ts, DIL_WIDTH + MEM_WIDTH), BF16)],
        compiler_params=_compiler_params(2),
        name="dilated_mixer_out",
    )(x, *outs, *lses, q_mem, mem_k, mem_v, w_out.astype(BF16))


def kernel(x, mem, ffn1_norm, ffn1_w_gate, ffn1_w_up, ffn1_w_down, ffn2_norm, ffn2_w_gate, ffn2_w_up, ffn2_w_down, mix_norm, mem_norm, w_mem_kv, mem_q_norm, mem_k_norm, a_w_in, a_conv, a_gate_bias, a_h_norm, a_w_out, b_w_q, b_q_norm, b_w_out, kv_norm, w_kv, kv_k_norm, rel_bias):
    b, s, d = x.shape
    n = b * s
    tm = 1024
    head_id = np.arange(GROUP_WIDTH) // HEAD_DIM
    head_avg = jnp.asarray((head_id[:, None] == head_id[None, :]) / HEAD_DIM, BF16)
    mlstm_chunk = V7X_MXU_DIM
    ltri = jnp.asarray(np.tril(np.ones((mlstm_chunk, mlstm_chunk))), BF16)
    flat = lambda t: t.reshape(n, t.shape[-1])
    ffn1_w_gate, ffn1_w_up, ffn1_w_down, ffn2_w_gate, ffn2_w_up, ffn2_w_down = (
        w.astype(BF16) for w in (ffn1_w_gate, ffn1_w_up, ffn1_w_down, ffn2_w_gate, ffn2_w_up, ffn2_w_down))
    ffn = lambda t, norm, wg, wu, wd, layer: _ffn(flat(t), norm[layer], wg, wu, wd, layer,
                                                   tm=tm).reshape(b, s, d)

    x = ffn(x, ffn1_norm, ffn1_w_gate, ffn1_w_up, ffn1_w_down, 0)
    mem_k, mem_v = _memkv(mem, mem_norm[0], w_mem_kv[0], mem_k_norm[0], head_avg)
    x = _mlstm_layer(x, mix_norm[0], a_w_in[0], a_conv[0], a_gate_bias[0], a_h_norm[0], a_w_out[0],
                     mem_k, mem_v, mem_q_norm[0], head_avg, ltri, ts=256)
    x = ffn(x, ffn2_norm, ffn2_w_gate, ffn2_w_up, ffn2_w_down, 0)

    k_sh, v_sh = _proj(x, kv_norm, w_kv, jnp.tile(kv_k_norm, DIL_WIDTH // HEAD_DIM), head_avg,
                       [(DIL_WIDTH, True), (DIL_WIDTH, True)], DIL_WIDTH, tm=tm, name="shared_kv_proj")

    x = ffn(x, ffn1_norm, ffn1_w_gate, ffn1_w_up, ffn1_w_down, 1)
    q_gain = jnp.concatenate([jnp.tile(b_q_norm[0], DIL_WIDTH // HEAD_DIM),
                              jnp.tile(mem_q_norm[1], MEM_HEADS)]) * ATTN_SCALE
    q_d, q_mem = _proj(x, mix_norm[1], b_w_q[0], q_gain, head_avg, [(DIL_WIDTH, True), (MEM_WIDTH, False)],
                       DIL_WIDTH + MEM_WIDTH, tm=tm, name="dilated_q_proj")
    mem_k, mem_v = _memkv(mem, mem_norm[1], w_mem_kv[1], mem_k_norm[1], head_avg)
    bias = _dilated_bias(rel_bias)
    attn_tile = max(window for window, _ in DILATED_GROUPS)
    outs, lses = [], []
    for g, (window, dilation) in enumerate(DILATED_GROUPS):
        assert window // dilation == ATTN_BLOCK and s % attn_tile == 0 and attn_tile % window == 0
        o_g, l_g = _dilated_attention(q_d, k_sh, v_sh, bias, g, dilation, tile=attn_tile)
        outs.append(o_g)
        lses.append(l_g)
    x = _mixout(x, outs, lses, q_mem, mem_k, mem_v, b_w_out[0], ts=1024)
    x = ffn(x, ffn2_norm, ffn2_w_gate, ffn2_w_up, ffn2_w_down, 1)
    return x
```

```python
import functools
import math

import numpy as np
import jax
import jax.numpy as jnp
from jax import lax
from jax.experimental import pallas as pl
from jax.experimental.pallas import tpu as pltpu

F32 = jnp.float32
BF16 = jnp.bfloat16

HEAD_DIM = 64
MEM_HEADS = 4
MEM_WIDTH = MEM_HEADS * HEAD_DIM
MLSTM_HEADS = 4
MLSTM_HEAD_DIM = 192
MLSTM_WIDTH = MLSTM_HEADS * MLSTM_HEAD_DIM
DILATED_GROUPS = ((128, 1), (512, 4), (2048, 16))
HEADS_PER_GROUP = 4
GROUP_WIDTH = HEADS_PER_GROUP * HEAD_DIM
DIL_WIDTH = GROUP_WIDTH * len(DILATED_GROUPS)
NUM_BUCKETS = 32
MAX_DISTANCE = 2048
RMS_EPS = 1e-6
NEG_INF = -1e30
ATTN_SCALE = HEAD_DIM ** -0.5

V7X_LANES = 128
V7X_SUBLANES = 8
V7X_MXU_DIM = 256
V7X_VMEM_BYTES = 64 * 1024 * 1024
VMEM_LIMIT_BYTES = V7X_VMEM_BYTES - 8 * 1024 * 1024

HEAD_WIN = V7X_MXU_DIM
HEAD_WIN_START = tuple((h * MLSTM_HEAD_DIM) // V7X_LANES * V7X_LANES for h in range(MLSTM_HEADS))
HEAD_WIN_OFFSET = tuple(h * MLSTM_HEAD_DIM - s for h, s in enumerate(HEAD_WIN_START))
assert all(o + MLSTM_HEAD_DIM <= HEAD_WIN for o in HEAD_WIN_OFFSET) and MLSTM_HEAD_DIM < HEAD_WIN
HEAD_DEN_LANE = tuple((o + MLSTM_HEAD_DIM) % HEAD_WIN for o in HEAD_WIN_OFFSET)
ATTN_BLOCK = 128


def _compiler_params(n_axes):
    return pltpu.CompilerParams(dimension_semantics=("arbitrary",) * n_axes,
                                vmem_limit_bytes=VMEM_LIMIT_BYTES)


def _resident(shape):
    zeros = (0,) * len(shape)
    return pl.BlockSpec(shape, lambda *_: zeros, pipeline_mode=pl.Buffered(1))


def _dot(a, b):
    return jnp.dot(a, b, preferred_element_type=F32)


def _dot_nt(a, b):
    return lax.dot_general(a, b, (((1,), (1,)), ((), ())), preferred_element_type=F32)


def _dot_tn(a, b):
    return lax.dot_general(a, b, (((0,), (0,)), ((), ())), preferred_element_type=F32)


def _rms(x, gain):
    ms = jnp.mean(x * x, axis=-1, keepdims=True)
    return x * lax.rsqrt(ms + RMS_EPS) * gain


def _split_bf16(x, terms):
    parts = []
    rest = x
    for _ in range(terms):
        part = rest.astype(BF16)
        parts.append(part)
        rest = rest - part.astype(F32)
    return parts


def _head_mean_sq(x, head_avg):
    return _dot((x * x).astype(BF16), head_avg)


def _head_rms(x, head_avg, gain):
    return x * lax.rsqrt(_head_mean_sq(x, head_avg) + RMS_EPS) * gain


def _silu(x):
    return x * jax.nn.sigmoid(x)


def _log_sigmoid(x):
    return jnp.minimum(x, 0.0) - jnp.log(1.0 + jnp.exp(-jnp.abs(x)))


def _head_lane_mask(head, width=GROUP_WIDTH):
    lane = lax.broadcasted_iota(jnp.int32, (1, width), 1)
    return (lane >= head * HEAD_DIM) & (lane < (head + 1) * HEAD_DIM)


def _memory_attention(qn, mem_k, mem_v):
    out, _ = _stacked_head_attention(qn, mem_k, mem_v, None, None)
    return out


def _stacked_head_attention(q, keys, values, bias, masked_below):
    t = q.shape[0]
    masks = [_head_lane_mask(h) for h in range(HEADS_PER_GROUP)]
    qs = jnp.concatenate([jnp.where(m, q, jnp.zeros_like(q)) for m in masks], axis=0)
    s = _dot_nt(qs, keys)
    if bias is not None:
        s = s + bias
    if masked_below is not None:
        key = lax.broadcasted_iota(jnp.int32, (1, keys.shape[0]), 1)
        s = jnp.where(key < masked_below, NEG_INF, s)
    mx = jnp.max(s, axis=-1, keepdims=True)
    e = jnp.exp(s - mx)
    den = jnp.sum(e, axis=-1, keepdims=True)
    o_all = _dot(e.astype(BF16), values) / den
    l_all = mx + jnp.log(den)
    out = jnp.zeros((t, q.shape[1]), F32)
    lse = jnp.zeros((t, q.shape[1]), F32)
    for h, m in enumerate(masks):
        out = out + jnp.where(m, o_all[h * t:(h + 1) * t], 0.0)
        lse = lse + jnp.where(m, l_all[h * t:(h + 1) * t], 0.0)
    return out, lse


FFN_CHUNK = V7X_MXU_DIM


def _ffn_halfstep(x, gain_ref, wg_ref, wu_ref, wd_ref, h_scr, acc_scr):
    h_scr[...] = _rms(x, gain_ref[...]).astype(BF16)
    for c in range(wg_ref.shape[1] // FFN_CHUNK):
        cols = slice(c * FFN_CHUNK, (c + 1) * FFN_CHUNK)
        h = h_scr[...]
        act = (_silu(_dot(h, wg_ref[:, cols])) * _dot(h, wu_ref[:, cols])).astype(BF16)
        down = _dot(act, wd_ref[cols, :])
        if c == 0:
            acc_scr[...] = down
        else:
            acc_scr[...] += down
    return x + 0.5 * acc_scr[...]


def _ffn_body(x_ref, gain_ref, wg_ref, wu_ref, wd_ref, o_ref, h_scr, acc_scr):
    o_ref[...] = _ffn_halfstep(x_ref[...], gain_ref, wg_ref, wu_ref, wd_ref, h_scr, acc_scr)


def _ffn_weight_specs(d, f, layer):
    index = lambda *_: (layer, 0, 0)
    spec = lambda rows, cols: pl.BlockSpec((None, rows, cols), index, pipeline_mode=pl.Buffered(1))
    return [_resident((1, d)), spec(d, f), spec(d, f), spec(f, d)]


def _ffn(x2d, gain, w_gate, w_up, w_down, layer, *, tm):
    n, d = x2d.shape
    f = w_gate.shape[2]
    row_spec = pl.BlockSpec((tm, d), lambda i: (i, 0))
    return pl.pallas_call(
        _ffn_body,
        grid=(n // tm,),
        in_specs=[row_spec] + _ffn_weight_specs(d, f, layer),
        out_specs=row_spec,
        out_shape=jax.ShapeDtypeStruct((n, d), F32),
        scratch_shapes=[pltpu.VMEM((tm, d), BF16), pltpu.VMEM((tm, d), F32)],
        compiler_params=_compiler_params(1),
        name="ffn",
    )(x2d, gain.reshape(1, d), w_gate, w_up, w_down)


def _memkv_body(mem_ref, gain_ref, w_ref, kgain_ref, havg_ref, k_ref, v_ref):
    mem_n = _rms(mem_ref[0], gain_ref[...]).astype(BF16)
    kv = _dot(mem_n, w_ref[...])
    k_ref[0] = _head_rms(kv[:, :MEM_WIDTH], havg_ref[...], kgain_ref[...]).astype(BF16)
    v_ref[0] = kv[:, MEM_WIDTH:].astype(BF16)


def _memkv(mem, gain, w_mem_kv, k_gain, head_avg):
    b, m, d = mem.shape
    out_spec = pl.BlockSpec((1, m, MEM_WIDTH), lambda i: (i, 0, 0))
    return pl.pallas_call(
        _memkv_body,
        grid=(b,),
        in_specs=[pl.BlockSpec((1, m, d), lambda i: (i, 0, 0)), _resident((1, d)),
                  _resident((d, 2 * MEM_WIDTH)), _resident((1, MEM_WIDTH)),
                  _resident((MEM_WIDTH, MEM_WIDTH))],
        out_specs=[out_spec, out_spec],
        out_shape=[jax.ShapeDtypeStruct((b, m, MEM_WIDTH), BF16)] * 2,
        compiler_params=_compiler_params(1),
        name="memkv",
    )(mem, gain.reshape(1, d), w_mem_kv.astype(BF16), jnp.tile(k_gain, MEM_HEADS).reshape(1, MEM_WIDTH),
      head_avg)


_Q0, _K0, _V0, _O0 = (i * MLSTM_WIDTH for i in range(4))
_QM0 = 4 * MLSTM_WIDTH
_GI0 = _QM0 + MEM_WIDTH
_GF0 = _GI0 + V7X_LANES
A_IN_PAD_WIDTH = _GF0 + V7X_LANES
CONV_TAIL = V7X_SUBLANES


def _mlstm_body(x_ref, gain_ref, win_ref, conv_ref, bi_ref, bf_ref, hgain_ref, wout_ref,
                mk_ref, mv_ref, mqgain_ref, havg_ref, ltri_ref,
                o_ref, c_scr, m_scr, ext_scr, hcat_scr, *, ts, conv_width):
    @pl.when(pl.program_id(0) == 0)
    def _():
        c_scr[...] = jnp.zeros(c_scr.shape, F32)
        m_scr[...] = jnp.zeros(m_scr.shape, F32)
        ext_scr[:, :, 0:CONV_TAIL, :] = jnp.zeros(ext_scr.shape[:2] + (CONV_TAIL, V7X_LANES), F32)

    nb = x_ref.shape[0]
    projections = [_dot(_rms(x_ref[bi], gain_ref[...]).astype(BF16), win_ref[...]) for bi in range(nb)]
    for bi in range(nb):
        _mlstm_tile(projections[bi], bi, conv_ref, bi_ref, bf_ref, hgain_ref, mk_ref, mv_ref,
                    mqgain_ref, havg_ref, ltri_ref, c_scr, m_scr, ext_scr, hcat_scr, ts=ts,
                    conv_width=conv_width)
        rows = slice(bi * ts, (bi + 1) * ts)
        o_ref[bi] = x_ref[bi] + _dot(hcat_scr[rows, :], wout_ref[...])


def _mlstm_tile(p, bi, conv_ref, bi_ref, bf_ref, hgain_ref, mk_ref, mv_ref, mqgain_ref, havg_ref, ltri_ref,
                c_scr, m_scr, ext_scr, hcat_scr, *, ts, conv_width):
    out_rows = slice(bi * ts, (bi + 1) * ts)
    cw = conv_ref[...]
    conv_slabs = []
    for sl in range(ext_scr.shape[1]):
        lanes = slice(sl * V7X_LANES, (sl + 1) * V7X_LANES)
        pre = p[:, lanes]
        ext_scr[bi, sl, CONV_TAIL:CONV_TAIL + ts, :] = pre
        y = cw[conv_width - 1:conv_width, lanes] * pre
        for back in range(1, conv_width):
            tap = conv_width - 1 - back
            y = y + cw[tap:tap + 1, lanes] * ext_scr[bi, sl, CONV_TAIL - back:CONV_TAIL - back + ts, :]
        ext_scr[bi, sl, 0:CONV_TAIL, :] = pre[ts - CONV_TAIL:ts, :]
        conv_slabs.append(_silu(y))
    qk = jnp.concatenate(conv_slabs, axis=-1)
    q = qk[:, :MLSTM_WIDTH].astype(BF16)
    k = qk[:, MLSTM_WIDTH:] * (MLSTM_HEAD_DIM ** -0.5)
    v = p[:, _V0:_O0].astype(BF16)
    o_gate = p[:, _O0:_QM0]
    gate_i = p[:, _GI0:_GF0] + bi_ref[...]
    log_f = _log_sigmoid(p[:, _GF0:A_IN_PAD_WIDTH] + bf_ref[...])

    L = ltri_ref.shape[0]
    row = lax.broadcasted_iota(jnp.int32, (L, L), 0)
    col = lax.broadcasted_iota(jnp.int32, (L, L), 1)
    causal = col <= row
    win_lane = lax.broadcasted_iota(jnp.int32, (1, HEAD_WIN), 1)
    ltri = ltri_ref[...]
    for c in range(ts // L):
        rows = slice(c * L, (c + 1) * L)
        cells = []
        b = sum(_dot(ltri, part) for part in _split_bf16(log_f[rows], 3))
        b_last = b[L - 1:L, :]
        r = gate_i[rows] - b
        a = b_last + r
        m0 = m_scr[bi]
        m_new = jnp.maximum(b_last + m0, jnp.max(a, axis=0, keepdims=True))
        w = jnp.exp(a - m_new)
        decay = jnp.exp(b_last + m0 - m_new)
        g = b + m0
        r_rows = r.T
        for hh in range(MLSTM_HEADS):
            win = slice(HEAD_WIN_START[hh], HEAD_WIN_START[hh] + HEAD_WIN)
            off, den_lane = HEAD_WIN_OFFSET[hh], HEAD_DEN_LANE[hh]
            in_head = (win_lane >= off) & (win_lane < off + MLSTM_HEAD_DIM)
            qh = jnp.where(in_head, q[rows, win], jnp.zeros((), BF16))
            kh = k[rows, win]
            ones_lane = jnp.where(win_lane == den_lane, 1.0, 0.0).astype(BF16)
            vh = jnp.where(in_head, v[rows, win], ones_lane)
            d_log = jnp.where(causal, b[:, hh:hh + 1] + r_rows[hh:hh + 1, :], NEG_INF)
            g_col = g[:, hh:hh + 1]
            m_t = jnp.maximum(g_col, jnp.max(d_log, axis=-1, keepdims=True))
            wmat = jnp.exp(d_log - m_t) * _dot_nt(qh, kh.astype(BF16))
            c0 = c_scr[bi, hh]
            num = _dot(wmat.astype(BF16), vh) + jnp.exp(g_col - m_t) * _dot(qh, c0.astype(BF16))
            den = num[:, den_lane:den_lane + 1]
            cell = num / jnp.maximum(jnp.abs(den), jnp.exp(-m_t))
            cell = jnp.where(in_head, cell, 0.0)
            ms = jnp.sum(cell * cell, axis=-1, keepdims=True) * (1.0 / MLSTM_HEAD_DIM)
            cells.append(cell * lax.rsqrt(ms + RMS_EPS))
            kw = (kh * w[:, hh:hh + 1]).astype(BF16)
            c_scr[bi, hh] = decay[:, hh:hh + 1] * c0 + _dot_tn(kw, vh)
        m_scr[bi] = m_new
        pieces = {}
        for hh, cell in enumerate(cells):
            for half in range(HEAD_WIN // V7X_LANES):
                tile_idx = HEAD_WIN_START[hh] // V7X_LANES + half
                piece = cell[:, half * V7X_LANES:(half + 1) * V7X_LANES]
                pieces[tile_idx] = piece if tile_idx not in pieces else pieces[tile_idx] + piece
        cell_all = jnp.concatenate([pieces[t] for t in range(MLSTM_WIDTH // V7X_LANES)], axis=-1)
        hcat_scr[bi * ts + c * L:bi * ts + (c + 1) * L, :MLSTM_WIDTH] = (
            jax.nn.sigmoid(o_gate[rows]) * cell_all * hgain_ref[...]).astype(BF16)

    qm = _head_rms(p[:, _QM0:_GI0], havg_ref[...], mqgain_ref[...] * ATTN_SCALE).astype(BF16)
    hcat_scr[out_rows, MLSTM_WIDTH:] = _memory_attention(qm, mk_ref[bi], mv_ref[bi]).astype(BF16)


def _pad_lanes(w):
    return jnp.pad(w, [(0, 0)] * (w.ndim - 1) + [(0, V7X_LANES - w.shape[-1])])


def _mlstm_layer(x, gain, w_in, conv_w, gate_bias, h_gain, w_out, mem_k, mem_v, mq_gain,
                 head_avg, ltri, *, ts):
    b, s, d = x.shape
    W, H = MLSTM_WIDTH, MLSTM_HEADS
    w_in_p = jnp.concatenate([w_in[:, :4 * W], w_in[:, 4 * W + 2 * H:], _pad_lanes(w_in[:, 4 * W:4 * W + H]),
                              _pad_lanes(w_in[:, 4 * W + H:4 * W + 2 * H])], axis=1).astype(BF16)
    conv_p = conv_w
    w_out_p = w_out.astype(BF16)
    conv_width = conv_w.shape[0]
    tile = pl.BlockSpec((b, ts, d), lambda j: (0, j, 0))
    return pl.pallas_call(
        functools.partial(_mlstm_body, ts=ts, conv_width=conv_width),
        grid=(s // ts,),
        in_specs=[tile, _resident((1, d)), _resident(w_in_p.shape), _resident(conv_p.shape),
                  _resident((1, V7X_LANES)), _resident((1, V7X_LANES)), _resident((1, W)),
                  _resident(w_out_p.shape), _resident(mem_k.shape), _resident(mem_v.shape),
                  _resident((1, MEM_WIDTH)), _resident(head_avg.shape), _resident(ltri.shape)],
        out_specs=tile,
        out_shape=jax.ShapeDtypeStruct((b, s, d), F32),
        scratch_shapes=[pltpu.VMEM((b, H, HEAD_WIN, HEAD_WIN), F32), pltpu.VMEM((b, 1, V7X_LANES), F32),
                        pltpu.VMEM((b, 2 * W // V7X_LANES, CONV_TAIL + ts, V7X_LANES), F32),
                        pltpu.VMEM((b * ts, W + MEM_WIDTH), BF16)],
        compiler_params=_compiler_params(1),
        name="mlstm_mixer",
    )(x, gain.reshape(1, d), w_in_p, conv_p, _pad_lanes(gate_bias[:H].reshape(1, H)),
      _pad_lanes(gate_bias[H:].reshape(1, H)), h_gain.reshape(1, W), w_out_p,
      mem_k, mem_v, jnp.tile(mq_gain, MEM_HEADS).reshape(1, MEM_WIDTH), head_avg, ltri)


def _proj_body(x_ref, gain_ref, w_ref, hgain_ref, havg_ref, *out_refs, normed_width):
    h = _rms(x_ref[0], gain_ref[...]).astype(BF16)
    y = _dot(h, w_ref[...])
    pieces = []
    for c in range(y.shape[1] // GROUP_WIDTH):
        cols = slice(c * GROUP_WIDTH, (c + 1) * GROUP_WIDTH)
        if c * GROUP_WIDTH < normed_width:
            pieces.append(_head_rms(y[:, cols], havg_ref[...], hgain_ref[:, cols]))
        else:
            pieces.append(y[:, cols])
    start = 0
    for ref in out_refs:
        if len(ref.shape) == 4:
            n = ref.shape[1] * V7X_LANES // GROUP_WIDTH
            for c in range(n):
                for half in range(GROUP_WIDTH // V7X_LANES):
                    lanes = slice(half * V7X_LANES, (half + 1) * V7X_LANES)
                    ref[0, c * (GROUP_WIDTH // V7X_LANES) + half] = pieces[start + c][:, lanes]
        else:
            n = ref.shape[2] // GROUP_WIDTH
            for c in range(n):
                ref[0, :, c * GROUP_WIDTH:(c + 1) * GROUP_WIDTH] = pieces[start + c].astype(BF16)
        start += n


def _proj(x, gain, w, head_gain, head_avg, outputs, normed_width, *, tm, name):
    b, s, d = x.shape
    width = w.shape[1]
    out_specs, out_shapes = [], []
    for ow, slab_major in outputs:
        if slab_major:
            out_specs.append(pl.BlockSpec((1, ow // V7X_LANES, tm, V7X_LANES), lambda i, j: (i, 0, j, 0)))
            out_shapes.append(jax.ShapeDtypeStruct((b, ow // V7X_LANES, s, V7X_LANES), F32))
        else:
            out_specs.append(pl.BlockSpec((1, tm, ow), lambda i, j: (i, j, 0)))
            out_shapes.append(jax.ShapeDtypeStruct((b, s, ow), BF16))
    return pl.pallas_call(
        functools.partial(_proj_body, normed_width=normed_width),
        grid=(b, s // tm),
        in_specs=[pl.BlockSpec((1, tm, d), lambda i, j: (i, j, 0)), _resident((1, d)), _resident((d, width)),
                  _resident((1, normed_width)), _resident(head_avg.shape)],
        out_specs=out_specs,
        out_shape=out_shapes,
        compiler_params=_compiler_params(2),
        name=name,
    )(x, gain.reshape(1, d), w.astype(BF16), head_gain.reshape(1, normed_width), head_avg)


SLABS = GROUP_WIDTH // V7X_LANES


PRE_STRIDE = 4


def _dilated_body(q_ref, k_ref, v_ref, bias_ref, o_ref, l_ref, kcarry_scr, vcarry_scr, *plane_scr,
                  dilation, nblk):
    A = ATTN_BLOCK
    tile = dilation * nblk * A
    first = pl.program_id(1) == 0
    no_prev_limit = jnp.where(first, A, 0)
    bias = bias_ref[0]

    @pl.when(first)
    def _():
        kcarry_scr[...] = jnp.zeros(kcarry_scr.shape, BF16)
        vcarry_scr[...] = jnp.zeros(vcarry_scr.shape, BF16)

    pre = PRE_STRIDE if dilation > PRE_STRIDE else 1
    post = dilation // pre
    if pre > 1:
        planes, = plane_scr
        for t, ref in enumerate((q_ref, k_ref, v_ref)):
            for sl in range(SLABS):
                for a in range(pre):
                    planes[(t * SLABS + sl) * pre + a] = ref[0, sl, pl.ds(a, tile // pre, stride=pre), :]

    def load(t, ref, r, i):
        if pre > 1:
            rows = pl.ds(i * A * post + r // pre, A, stride=post)
            parts = [planes[(t * SLABS + sl) * pre + r % pre, rows, :] for sl in range(SLABS)]
        else:
            parts = [ref[0, sl, rows_of(r, i), :] for sl in range(SLABS)]
        return jnp.concatenate(parts, axis=-1).astype(BF16)

    def rows_of(r, i):
        start = i * A * dilation + r
        return pl.ds(start, A, stride=dilation) if dilation > 1 else pl.ds(start, A)

    for r in range(dilation):
        carry_rows = slice(r * A, (r + 1) * A)
        k_cur, v_cur = kcarry_scr[carry_rows, :], vcarry_scr[carry_rows, :]
        for i in range(nblk):
            k_prev, v_prev = k_cur, v_cur
            k_cur, v_cur = load(1, k_ref, r, i), load(2, v_ref, r, i)
            out, lse = _stacked_head_attention(
                load(0, q_ref, r, i), jnp.concatenate([k_prev, k_cur], axis=0),
                jnp.concatenate([v_prev, v_cur], axis=0), bias, no_prev_limit if i == 0 else None)
            for sl in range(SLABS):
                lanes = slice(sl * V7X_LANES, (sl + 1) * V7X_LANES)
                o_ref[0, sl, rows_of(r, i), :] = out[:, lanes]
                l_ref[0, sl, rows_of(r, i), :] = lse[:, lanes]
        kcarry_scr[carry_rows, :] = k_cur
        vcarry_scr[carry_rows, :] = v_cur


def _dilated_attention(q, k, v, bias, group, dilation, *, tile):
    b, _, s, _ = q.shape
    span = ATTN_BLOCK * dilation
    nblk = tile // span
    cur = pl.BlockSpec((1, SLABS, tile, V7X_LANES), lambda bi, n: (bi, group, n, 0))
    out_spec = pl.BlockSpec((1, SLABS, tile, V7X_LANES), lambda bi, n: (bi, 0, n, 0))
    out_shape = jax.ShapeDtypeStruct((b, SLABS, s, V7X_LANES), F32)
    carry = pltpu.VMEM((span, GROUP_WIDTH), BF16)
    scratch = [carry, carry]
    if dilation > PRE_STRIDE:
        scratch.append(pltpu.VMEM((3 * SLABS * PRE_STRIDE, tile // PRE_STRIDE, V7X_LANES), F32))
    return pl.pallas_call(
        functools.partial(_dilated_body, dilation=dilation, nblk=nblk),
        grid=(b, s // tile),
        in_specs=[cur, cur, cur,
                  pl.BlockSpec((1,) + bias.shape[1:], lambda bi, n: (group, 0, 0), pipeline_mode=pl.Buffered(1))],
        out_specs=[out_spec, out_spec],
        out_shape=[out_shape, out_shape],
        scratch_shapes=scratch,
        compiler_params=_compiler_params(2),
        name=f"dilated_attn_g{group}",
    )(q, k, v, bias)


def _bias_body(table_ref, bucket_ref, o_ref):
    g = pl.program_id(0)
    bucket = bucket_ref[0]
    for hh in range(HEADS_PER_GROUP):
        acc = jnp.full(bucket.shape, NEG_INF, F32)
        for bkt in range(NUM_BUCKETS):
            acc = jnp.where(bucket == bkt, table_ref[g * HEADS_PER_GROUP + hh, bkt], acc)
        o_ref[0, hh * ATTN_BLOCK:(hh + 1) * ATTN_BLOCK, :] = acc


def _dilated_bias(rel_bias):
    A = ATTN_BLOCK
    m_off = np.arange(A)[:, None] + A - np.arange(2 * A)[None, :]
    band = (m_off >= 0) & (m_off <= A)
    max_exact = NUM_BUCKETS // 2
    buckets = []
    for _, dilation in DILATED_GROUPS:
        dist = np.clip(m_off, 0, A) * dilation
        large = max_exact + (np.log(np.maximum(dist, 1).astype(np.float32) / max_exact)
                             / math.log(MAX_DISTANCE / max_exact) * (NUM_BUCKETS - max_exact)).astype(np.int32)
        bucket = np.where(dist < max_exact, dist, np.minimum(large, NUM_BUCKETS - 1))
        buckets.append(np.where(band, bucket, -1))
    buckets = jnp.asarray(np.stack(buckets), jnp.int32)
    n_groups = len(DILATED_GROUPS)
    return pl.pallas_call(
        _bias_body,
        grid=(n_groups,),
        in_specs=[pl.BlockSpec(memory_space=pltpu.SMEM), pl.BlockSpec((1, A, 2 * A), lambda g: (g, 0, 0))],
        out_specs=pl.BlockSpec((1, HEADS_PER_GROUP * A, 2 * A), lambda g: (g, 0, 0)),
        out_shape=jax.ShapeDtypeStruct((n_groups, HEADS_PER_GROUP * A, 2 * A), F32),
        compiler_params=_compiler_params(1),
        name="dilated_bias",
    )(rel_bias.astype(F32).T, buckets)


def _mixout_body(x_ref, o0_ref, o1_ref, o2_ref, l0_ref, l1_ref, l2_ref, qm_ref, mk_ref, mv_ref, wout_ref,
                 gain_ref, wg_ref, wu_ref, wd_ref, out_ref, hcat_scr, h_scr, acc_scr):
    wide = lambda ref: jnp.concatenate([ref[0, sl] for sl in range(SLABS)], axis=-1)
    lses = [wide(l0_ref), wide(l1_ref), wide(l2_ref)]
    mx = jnp.maximum(jnp.maximum(lses[0], lses[1]), lses[2])
    es = [jnp.exp(l - mx) for l in lses]
    inv = 1.0 / (es[0] + es[1] + es[2])
    for g, o_ref in enumerate((o0_ref, o1_ref, o2_ref)):
        hcat_scr[:, g * GROUP_WIDTH:(g + 1) * GROUP_WIDTH] = (wide(o_ref) * (es[g] * inv)).astype(BF16)
    hcat_scr[:, DIL_WIDTH:] = _memory_attention(qm_ref[0], mk_ref[0], mv_ref[0]).astype(BF16)
    mixed = x_ref[0] + _dot(hcat_scr[...], wout_ref[...])
    out_ref[0] = _ffn_halfstep(mixed, gain_ref, wg_ref, wu_ref, wd_ref, h_scr, acc_scr)


def _mixout_ffn(x, outs, lses, q_mem, mem_k, mem_v, w_out, ffn_gain, w_gate, w_up, w_down, layer, *, ts):
    b, s, d = x.shape
    f = w_gate.shape[2]
    tile = lambda w: pl.BlockSpec((1, ts, w), lambda i, j: (i, j, 0))
    slab_tile = pl.BlockSpec((1, SLABS, ts, V7X_LANES), lambda i, j: (i, 0, j, 0))
    mem_spec = pl.BlockSpec((1,) + mem_k.shape[1:], lambda i, j: (i, 0, 0))
    return pl.pallas_call(
        _mixout_body,
        grid=(b, s // ts),
        in_specs=([tile(d)] + [slab_tile] * 6 + [tile(MEM_WIDTH), mem_spec, mem_spec, _resident(w_out.shape)]
                  + _ffn_weight_specs(d, f, layer)),
        out_specs=tile(d),
        out_shape=jax.ShapeDtypeStruct((b, s, d), F32),
        scratch_shapes=[pltpu.VMEM((ts, DIL_WIDTH + MEM_WIDTH), BF16), pltpu.VMEM((ts, d), BF16),
                        pltpu.VMEM((ts, d), F32)],
        compiler_params=_compiler_params(2),
        name="dilated_mixer_out_ffn",
    )(x, *outs, *lses, q_mem, mem_k, mem_v, w_out.astype(BF16), ffn_gain.reshape(1, d), w_gate, w_up, w_down)


def kernel(x, mem, ffn1_norm, ffn1_w_gate, ffn1_w_up, ffn1_w_down, ffn2_norm, ffn2_w_gate, ffn2_w_up, ffn2_w_down, mix_norm, mem_norm, w_mem_kv, mem_q_norm, mem_k_norm, a_w_in, a_conv, a_gate_bias, a_h_norm, a_w_out, b_w_q, b_q_norm, b_w_out, kv_norm, w_kv, kv_k_norm, rel_bias):
    b, s, d = x.shape
    n = b * s
    tm = 1024
    head_id = np.arange(GROUP_WIDTH) // HEAD_DIM
    head_avg = jnp.asarray((head_id[:, None] == head_id[None, :]) / HEAD_DIM, BF16)
    mlstm_chunk = V7X_MXU_DIM
    ltri = jnp.asarray(np.tril(np.ones((mlstm_chunk, mlstm_chunk))), BF16)
    flat = lambda t: t.reshape(n, t.shape[-1])
    ffn1_w_gate, ffn1_w_up, ffn1_w_down, ffn2_w_gate, ffn2_w_up, ffn2_w_down = (
        w.astype(BF16) for w in (ffn1_w_gate, ffn1_w_up, ffn1_w_down, ffn2_w_gate, ffn2_w_up, ffn2_w_down))
    ffn = lambda t, norm, wg, wu, wd, layer: _ffn(flat(t), norm[layer], wg, wu, wd, layer,
                                                   tm=tm).reshape(b, s, d)

    x = ffn(x, ffn1_norm, ffn1_w_gate, ffn1_w_up, ffn1_w_down, 0)
    mem_k, mem_v = _memkv(mem, mem_norm[0], w_mem_kv[0], mem_k_norm[0], head_avg)
    x = _mlstm_layer(x, mix_norm[0], a_w_in[0], a_conv[0], a_gate_bias[0], a_h_norm[0], a_w_out[0],
                     mem_k, mem_v, mem_q_norm[0], head_avg, ltri, ts=256)
    x = ffn(x, ffn2_norm, ffn2_w_gate, ffn2_w_up, ffn2_w_down, 0)

    k_sh, v_sh = _proj(x, kv_norm, w_kv, jnp.tile(kv_k_norm, DIL_WIDTH // HEAD_DIM), head_avg,
                       [(DIL_WIDTH, True), (DIL_WIDTH, True)], DIL_WIDTH, tm=tm, name="shared_kv_proj")

    x = ffn(x, ffn1_norm, ffn1_w_gate, ffn1_w_up, ffn1_w_down, 1)
    q_gain = jnp.concatenate([jnp.tile(b_q_norm[0], DIL_WIDTH // HEAD_DIM),
                              jnp.tile(mem_q_norm[1], MEM_HEADS)]) * ATTN_SCALE
    q_d, q_mem = _proj(x, mix_norm[1], b_w_q[0], q_gain, head_avg, [(DIL_WIDTH, True), (MEM_WIDTH, False)],
                       DIL_WIDTH + MEM_WIDTH, tm=tm, name="dilated_q_proj")
    mem_k, mem_v = _memkv(mem, mem_norm[1], w_mem_kv[1], mem_k_norm[1], head_avg)
    bias = _dilated_bias(rel_bias)
    attn_tile = max(window for window, _ in DILATED_GROUPS)
    outs, lses = [], []
    for g, (window, dilation) in enumerate(DILATED_GROUPS):
        assert window // dilation == ATTN_BLOCK and s % attn_tile == 0 and attn_tile % window == 0
        o_g, l_g = _dilated_attention(q_d, k_sh, v_sh, bias, g, dilation, tile=attn_tile)
        outs.append(o_g)
        lses.append(l_g)
    return _mixout_ffn(x, outs, lses, q_mem, mem_k, mem_v, b_w_out[0], ffn2_norm[1], ffn2_w_gate, ffn2_w_up,
                       ffn2_w_down, 1, ts=512)
```

```python
import functools
import math

import numpy as np
import jax
import jax.numpy as jnp
from jax import lax
from jax.experimental import pallas as pl
from jax.experimental.pallas import tpu as pltpu

F32 = jnp.float32
BF16 = jnp.bfloat16

HEAD_DIM = 64
MEM_HEADS = 4
MEM_WIDTH = MEM_HEADS * HEAD_DIM
MLSTM_HEADS = 4
MLSTM_HEAD_DIM = 192
MLSTM_WIDTH = MLSTM_HEADS * MLSTM_HEAD_DIM
DILATED_GROUPS = ((128, 1), (512, 4), (2048, 16))
HEADS_PER_GROUP = 4
GROUP_WIDTH = HEADS_PER_GROUP * HEAD_DIM
DIL_WIDTH = GROUP_WIDTH * len(DILATED_GROUPS)
NUM_BUCKETS = 32
MAX_DISTANCE = 2048
RMS_EPS = 1e-6
NEG_INF = -1e30
ATTN_SCALE = HEAD_DIM ** -0.5
LOG2E = math.log2(math.e)
LN2 = math.log(2.0)
Q_SCALE = ATTN_SCALE * LOG2E

V7X_LANES = 128
V7X_SUBLANES = 8
V7X_MXU_DIM = 256
V7X_VMEM_BYTES = 64 * 1024 * 1024
VMEM_LIMIT_BYTES = V7X_VMEM_BYTES - 8 * 1024 * 1024

HEAD_WIN = V7X_MXU_DIM
HEAD_WIN_START = tuple((h * MLSTM_HEAD_DIM) // V7X_LANES * V7X_LANES for h in range(MLSTM_HEADS))
HEAD_WIN_OFFSET = tuple(h * MLSTM_HEAD_DIM - s for h, s in enumerate(HEAD_WIN_START))
assert all(o + MLSTM_HEAD_DIM <= HEAD_WIN for o in HEAD_WIN_OFFSET) and MLSTM_HEAD_DIM < HEAD_WIN
HEAD_DEN_LANE = tuple((o + MLSTM_HEAD_DIM) % HEAD_WIN for o in HEAD_WIN_OFFSET)
ATTN_BLOCK = 128


def _compiler_params(n_axes):
    return pltpu.CompilerParams(dimension_semantics=("arbitrary",) * n_axes,
                                vmem_limit_bytes=VMEM_LIMIT_BYTES)


def _resident(shape):
    zeros = (0,) * len(shape)
    return pl.BlockSpec(shape, lambda *_: zeros, pipeline_mode=pl.Buffered(1))


def _dot(a, b):
    return jnp.dot(a, b, preferred_element_type=F32)


def _dot_nt(a, b):
    return lax.dot_general(a, b, (((1,), (1,)), ((), ())), preferred_element_type=F32)


def _dot_tn(a, b):
    return lax.dot_general(a, b, (((0,), (0,)), ((), ())), preferred_element_type=F32)


def _rms(x, gain):
    ms = jnp.mean(x * x, axis=-1, keepdims=True)
    return x * lax.rsqrt(ms + RMS_EPS) * gain


def _split_bf16(x, terms):
    parts = []
    rest = x
    for _ in range(terms):
        part = rest.astype(BF16)
        parts.append(part)
        rest = rest - part.astype(F32)
    return parts


def _head_mean_sq(x, head_avg):
    return _dot((x * x).astype(BF16), head_avg)


def _head_rms(x, head_avg, gain):
    return x * lax.rsqrt(_head_mean_sq(x, head_avg) + RMS_EPS) * gain


def _silu(x):
    return x * jax.nn.sigmoid(x)


def _log_sigmoid(x):
    return jnp.minimum(x, 0.0) - jnp.log(1.0 + jnp.exp(-jnp.abs(x)))


def _head_lane_mask(head, width=GROUP_WIDTH):
    lane = lax.broadcasted_iota(jnp.int32, (1, width), 1)
    return (lane >= head * HEAD_DIM) & (lane < (head + 1) * HEAD_DIM)


def _memory_attention(qn, mem_k, mem_v):
    out, _ = _stacked_head_attention(qn, mem_k, mem_v, None, None)
    return out


def _stacked_head_attention(q, keys, values, bias, masked_below):
    t = q.shape[0]
    masks = [_head_lane_mask(h) for h in range(HEADS_PER_GROUP)]
    qs = jnp.concatenate([jnp.where(m, q, jnp.zeros_like(q)) for m in masks], axis=0)
    s = _dot_nt(qs, keys)
    if bias is not None:
        s = s + bias
    if masked_below is not None:
        key = lax.broadcasted_iota(jnp.int32, (1, keys.shape[0]), 1)
        s = jnp.where(key < masked_below, NEG_INF, s)
    mx = jnp.max(s, axis=-1, keepdims=True)
    e = jnp.exp2(s - mx)
    den = jnp.sum(e, axis=-1, keepdims=True)
    pv = _dot(e.astype(BF16), values)
    inv = 1.0 / den
    l_all = mx * LN2 + jnp.log(den)
    first_head = lax.broadcasted_iota(jnp.int32, (1, V7X_LANES), 1) < HEAD_DIM
    out_slabs, lse_slabs = [], []
    for sl in range(q.shape[1] // V7X_LANES):
        lanes = slice(sl * V7X_LANES, (sl + 1) * V7X_LANES)
        ra, rb = slice(2 * sl * t, (2 * sl + 1) * t), slice((2 * sl + 1) * t, (2 * sl + 2) * t)
        out_slabs.append(jnp.where(first_head, pv[ra, lanes] * inv[ra], pv[rb, lanes] * inv[rb]))
        lse_slabs.append(jnp.where(first_head, l_all[ra], l_all[rb]))
    return jnp.concatenate(out_slabs, axis=-1), jnp.concatenate(lse_slabs, axis=-1)


FFN_CHUNK = V7X_MXU_DIM


def _ffn_halfstep(x, gain_ref, wg_ref, wu_ref, wd_ref, h_scr, acc_scr):
    h_scr[...] = _rms(x, gain_ref[...]).astype(BF16)
    for c in range(wg_ref.shape[1] // FFN_CHUNK):
        cols = slice(c * FFN_CHUNK, (c + 1) * FFN_CHUNK)
        h = h_scr[...]
        act = (_silu(_dot(h, wg_ref[:, cols])) * _dot(h, wu_ref[:, cols])).astype(BF16)
        down = _dot(act, wd_ref[cols, :])
        if c == 0:
            acc_scr[...] = down
        else:
            acc_scr[...] += down
    return x + 0.5 * acc_scr[...]


def _ffn_body(x_ref, gain_ref, wg_ref, wu_ref, wd_ref, o_ref, h_scr, acc_scr):
    o_ref[...] = _ffn_halfstep(x_ref[...], gain_ref, wg_ref, wu_ref, wd_ref, h_scr, acc_scr)


def _ffn_weight_specs(d, f, layer):
    index = lambda *_: (layer, 0, 0)
    spec = lambda rows, cols: pl.BlockSpec((None, rows, cols), index, pipeline_mode=pl.Buffered(1))
    return [_resident((1, d)), spec(d, f), spec(d, f), spec(f, d)]


def _ffn(x2d, gain, w_gate, w_up, w_down, layer, *, tm):
    n, d = x2d.shape
    f = w_gate.shape[2]
    row_spec = pl.BlockSpec((tm, d), lambda i: (i, 0))
    return pl.pallas_call(
        _ffn_body,
        grid=(n // tm,),
        in_specs=[row_spec] + _ffn_weight_specs(d, f, layer),
        out_specs=row_spec,
        out_shape=jax.ShapeDtypeStruct((n, d), F32),
        scratch_shapes=[pltpu.VMEM((tm, d), BF16), pltpu.VMEM((tm, d), F32)],
        compiler_params=_compiler_params(1),
        name="ffn",
    )(x2d, gain.reshape(1, d), w_gate, w_up, w_down)


def _memkv_body(mem_ref, gain_ref, w_ref, kgain_ref, havg_ref, k_ref, v_ref):
    mem_n = _rms(mem_ref[0], gain_ref[...]).astype(BF16)
    kv = _dot(mem_n, w_ref[...])
    k_ref[0] = _head_rms(kv[:, :MEM_WIDTH], havg_ref[...], kgain_ref[...]).astype(BF16)
    v_ref[0] = kv[:, MEM_WIDTH:].astype(BF16)


def _memkv(mem, gain, w_mem_kv, k_gain, head_avg):
    b, m, d = mem.shape
    out_spec = pl.BlockSpec((1, m, MEM_WIDTH), lambda i: (i, 0, 0))
    return pl.pallas_call(
        _memkv_body,
        grid=(b,),
        in_specs=[pl.BlockSpec((1, m, d), lambda i: (i, 0, 0)), _resident((1, d)),
                  _resident((d, 2 * MEM_WIDTH)), _resident((1, MEM_WIDTH)),
                  _resident((MEM_WIDTH, MEM_WIDTH))],
        out_specs=[out_spec, out_spec],
        out_shape=[jax.ShapeDtypeStruct((b, m, MEM_WIDTH), BF16)] * 2,
        compiler_params=_compiler_params(1),
        name="memkv",
    )(mem, gain.reshape(1, d), w_mem_kv.astype(BF16), jnp.tile(k_gain, MEM_HEADS).reshape(1, MEM_WIDTH),
      head_avg)


_Q0, _K0, _V0, _O0 = (i * MLSTM_WIDTH for i in range(4))
_QM0 = 4 * MLSTM_WIDTH
_GI0 = _QM0 + MEM_WIDTH
_GF0 = _GI0 + V7X_LANES
A_IN_PAD_WIDTH = _GF0 + V7X_LANES
CONV_TAIL = V7X_SUBLANES


def _mlstm_body(x_ref, gain_ref, win_ref, conv_ref, bi_ref, bf_ref, hgain_ref, wout_ref,
                mk_ref, mv_ref, mqgain_ref, havg_ref, ltri_ref,
                o_ref, c_scr, m_scr, ext_scr, hcat_scr, *, ts, conv_width):
    @pl.when(pl.program_id(0) == 0)
    def _():
        c_scr[...] = jnp.zeros(c_scr.shape, F32)
        m_scr[...] = jnp.zeros(m_scr.shape, F32)
        ext_scr[:, :, 0:CONV_TAIL, :] = jnp.zeros(ext_scr.shape[:2] + (CONV_TAIL, V7X_LANES), F32)

    nb = x_ref.shape[0]
    projections = [_dot(_rms(x_ref[bi], gain_ref[...]).astype(BF16), win_ref[...]) for bi in range(nb)]
    for bi in range(nb):
        _mlstm_tile(projections[bi], bi, conv_ref, bi_ref, bf_ref, hgain_ref, mk_ref, mv_ref,
                    mqgain_ref, havg_ref, ltri_ref, c_scr, m_scr, ext_scr, hcat_scr, ts=ts,
                    conv_width=conv_width)
        rows = slice(bi * ts, (bi + 1) * ts)
        o_ref[bi] = x_ref[bi] + _dot(hcat_scr[rows, :], wout_ref[...])


def _mlstm_tile(p, bi, conv_ref, bi_ref, bf_ref, hgain_ref, mk_ref, mv_ref, mqgain_ref, havg_ref, ltri_ref,
                c_scr, m_scr, ext_scr, hcat_scr, *, ts, conv_width):
    out_rows = slice(bi * ts, (bi + 1) * ts)
    cw = conv_ref[...]
    conv_slabs = []
    for sl in range(ext_scr.shape[1]):
        lanes = slice(sl * V7X_LANES, (sl + 1) * V7X_LANES)
        pre = p[:, lanes]
        ext_scr[bi, sl, CONV_TAIL:CONV_TAIL + ts, :] = pre
        y = cw[conv_width - 1:conv_width, lanes] * pre
        for back in range(1, conv_width):
            tap = conv_width - 1 - back
            y = y + cw[tap:tap + 1, lanes] * ext_scr[bi, sl, CONV_TAIL - back:CONV_TAIL - back + ts, :]
        ext_scr[bi, sl, 0:CONV_TAIL, :] = pre[ts - CONV_TAIL:ts, :]
        conv_slabs.append(_silu(y))
    qk = jnp.concatenate(conv_slabs, axis=-1)
    q = qk[:, :MLSTM_WIDTH].astype(BF16)
    k = qk[:, MLSTM_WIDTH:] * (MLSTM_HEAD_DIM ** -0.5)
    v = p[:, _V0:_O0].astype(BF16)
    o_gate = p[:, _O0:_QM0]
    gate_i = p[:, _GI0:_GF0] + bi_ref[...]
    log_f = _log_sigmoid(p[:, _GF0:A_IN_PAD_WIDTH] + bf_ref[...])

    L = ltri_ref.shape[0]
    row = lax.broadcasted_iota(jnp.int32, (L, L), 0)
    col = lax.broadcasted_iota(jnp.int32, (L, L), 1)
    causal = col <= row
    win_lane = lax.broadcasted_iota(jnp.int32, (1, HEAD_WIN), 1)
    ltri = ltri_ref[...]
    for c in range(ts // L):
        rows = slice(c * L, (c + 1) * L)
        cells = []
        b = sum(_dot(ltri, part) for part in _split_bf16(log_f[rows], 3))
        b_last = b[L - 1:L, :]
        r = gate_i[rows] - b
        a = b_last + r
        m0 = m_scr[bi]
        m_new = jnp.maximum(b_last + m0, jnp.max(a, axis=0, keepdims=True))
        w = jnp.exp(a - m_new)
        decay = jnp.exp(b_last + m0 - m_new)
        g = b + m0
        r_rows = r.T
        for hh in range(MLSTM_HEADS):
            win = slice(HEAD_WIN_START[hh], HEAD_WIN_START[hh] + HEAD_WIN)
            off, den_lane = HEAD_WIN_OFFSET[hh], HEAD_DEN_LANE[hh]
            in_head = (win_lane >= off) & (win_lane < off + MLSTM_HEAD_DIM)
            qh = jnp.where(in_head, q[rows, win], jnp.zeros((), BF16))
            kh = k[rows, win]
            ones_lane = jnp.where(win_lane == den_lane, 1.0, 0.0).astype(BF16)
            vh = jnp.where(in_head, v[rows, win], ones_lane)
            d_log = jnp.where(causal, b[:, hh:hh + 1] + r_rows[hh:hh + 1, :], NEG_INF)
            g_col = g[:, hh:hh + 1]
            m_t = jnp.maximum(g_col, jnp.max(d_log, axis=-1, keepdims=True))
            wmat = jnp.exp(d_log - m_t) * _dot_nt(qh, kh.astype(BF16))
            c0 = c_scr[bi, hh]
            num = _dot(wmat.astype(BF16), vh) + jnp.exp(g_col - m_t) * _dot(qh, c0.astype(BF16))
            den = num[:, den_lane:den_lane + 1]
            cell = num / jnp.maximum(jnp.abs(den), jnp.exp(-m_t))
            cell = jnp.where(in_head, cell, 0.0)
            ms = jnp.sum(cell * cell, axis=-1, keepdims=True) * (1.0 / MLSTM_HEAD_DIM)
            cells.append(cell * lax.rsqrt(ms + RMS_EPS))
            kw = (kh * w[:, hh:hh + 1]).astype(BF16)
            c_scr[bi, hh] = decay[:, hh:hh + 1] * c0 + _dot_tn(kw, vh)
        m_scr[bi] = m_new
        pieces = {}
        for hh, cell in enumerate(cells):
            for half in range(HEAD_WIN // V7X_LANES):
                tile_idx = HEAD_WIN_START[hh] // V7X_LANES + half
                piece = cell[:, half * V7X_LANES:(half + 1) * V7X_LANES]
                pieces[tile_idx] = piece if tile_idx not in pieces else pieces[tile_idx] + piece
        cell_all = jnp.concatenate([pieces[t] for t in range(MLSTM_WIDTH // V7X_LANES)], axis=-1)
        hcat_scr[bi * ts + c * L:bi * ts + (c + 1) * L, :MLSTM_WIDTH] = (
            jax.nn.sigmoid(o_gate[rows]) * cell_all * hgain_ref[...]).astype(BF16)

    qm = _head_rms(p[:, _QM0:_GI0], havg_ref[...], mqgain_ref[...] * Q_SCALE).astype(BF16)
    hcat_scr[out_rows, MLSTM_WIDTH:] = _memory_attention(qm, mk_ref[bi], mv_ref[bi]).astype(BF16)


def _pad_lanes(w):
    return jnp.pad(w, [(0, 0)] * (w.ndim - 1) + [(0, V7X_LANES - w.shape[-1])])


def _mlstm_layer(x, gain, w_in, conv_w, gate_bias, h_gain, w_out, mem_k, mem_v, mq_gain,
                 head_avg, ltri, *, ts):
    b, s, d = x.shape
    W, H = MLSTM_WIDTH, MLSTM_HEADS
    w_in_p = jnp.concatenate([w_in[:, :4 * W], w_in[:, 4 * W + 2 * H:], _pad_lanes(w_in[:, 4 * W:4 * W + H]),
                              _pad_lanes(w_in[:, 4 * W + H:4 * W + 2 * H])], axis=1).astype(BF16)
    conv_p = conv_w
    w_out_p = w_out.astype(BF16)
    conv_width = conv_w.shape[0]
    tile = pl.BlockSpec((b, ts, d), lambda j: (0, j, 0))
    return pl.pallas_call(
        functools.partial(_mlstm_body, ts=ts, conv_width=conv_width),
        grid=(s // ts,),
        in_specs=[tile, _resident((1, d)), _resident(w_in_p.shape), _resident(conv_p.shape),
                  _resident((1, V7X_LANES)), _resident((1, V7X_LANES)), _resident((1, W)),
                  _resident(w_out_p.shape), _resident(mem_k.shape), _resident(mem_v.shape),
                  _resident((1, MEM_WIDTH)), _resident(head_avg.shape), _resident(ltri.shape)],
        out_specs=tile,
        out_shape=jax.ShapeDtypeStruct((b, s, d), F32),
        scratch_shapes=[pltpu.VMEM((b, H, HEAD_WIN, HEAD_WIN), F32), pltpu.VMEM((b, 1, V7X_LANES), F32),
                        pltpu.VMEM((b, 2 * W // V7X_LANES, CONV_TAIL + ts, V7X_LANES), F32),
                        pltpu.VMEM((b * ts, W + MEM_WIDTH), BF16)],
        compiler_params=_compiler_params(1),
        name="mlstm_mixer",
    )(x, gain.reshape(1, d), w_in_p, conv_p, _pad_lanes(gate_bias[:H].reshape(1, H)),
      _pad_lanes(gate_bias[H:].reshape(1, H)), h_gain.reshape(1, W), w_out_p,
      mem_k, mem_v, jnp.tile(mq_gain, MEM_HEADS).reshape(1, MEM_WIDTH), head_avg, ltri)


def _proj_body(x_ref, gain_ref, w_ref, hgain_ref, havg_ref, *out_refs, normed_width):
    h = _rms(x_ref[0], gain_ref[...]).astype(BF16)
    y = _dot(h, w_ref[...])
    pieces = []
    for c in range(y.shape[1] // GROUP_WIDTH):
        cols = slice(c * GROUP_WIDTH, (c + 1) * GROUP_WIDTH)
        if c * GROUP_WIDTH < normed_width:
            pieces.append(_head_rms(y[:, cols], havg_ref[...], hgain_ref[:, cols]))
        else:
            pieces.append(y[:, cols])
    start = 0
    for ref in out_refs:
        if len(ref.shape) == 4:
            n = ref.shape[1] * V7X_LANES // GROUP_WIDTH
            for c in range(n):
                for half in range(GROUP_WIDTH // V7X_LANES):
                    lanes = slice(half * V7X_LANES, (half + 1) * V7X_LANES)
                    ref[0, c * (GROUP_WIDTH // V7X_LANES) + half] = pieces[start + c][:, lanes]
        else:
            n = ref.shape[2] // GROUP_WIDTH
            for c in range(n):
                ref[0, :, c * GROUP_WIDTH:(c + 1) * GROUP_WIDTH] = pieces[start + c].astype(BF16)
        start += n


def _proj(x, gain, w, head_gain, head_avg, outputs, normed_width, *, tm, name):
    b, s, d = x.shape
    width = w.shape[1]
    out_specs, out_shapes = [], []
    for ow, slab_major in outputs:
        if slab_major:
            out_specs.append(pl.BlockSpec((1, ow // V7X_LANES, tm, V7X_LANES), lambda i, j: (i, 0, j, 0)))
            out_shapes.append(jax.ShapeDtypeStruct((b, ow // V7X_LANES, s, V7X_LANES), F32))
        else:
            out_specs.append(pl.BlockSpec((1, tm, ow), lambda i, j: (i, j, 0)))
            out_shapes.append(jax.ShapeDtypeStruct((b, s, ow), BF16))
    return pl.pallas_call(
        functools.partial(_proj_body, normed_width=normed_width),
        grid=(b, s // tm),
        in_specs=[pl.BlockSpec((1, tm, d), lambda i, j: (i, j, 0)), _resident((1, d)), _resident((d, width)),
                  _resident((1, normed_width)), _resident(head_avg.shape)],
        out_specs=out_specs,
        out_shape=out_shapes,
        compiler_params=_compiler_params(2),
        name=name,
    )(x, gain.reshape(1, d), w.astype(BF16), head_gain.reshape(1, normed_width), head_avg)


SLABS = GROUP_WIDTH // V7X_LANES


PRE_STRIDE = 4


def _dilated_body(q_ref, k_ref, v_ref, bias_ref, o_ref, l_ref, kcarry_scr, vcarry_scr, *plane_scr,
                  dilation, nblk):
    A = ATTN_BLOCK
    tile = dilation * nblk * A
    first = pl.program_id(1) == 0
    no_prev_limit = jnp.where(first, A, 0)
    bias = bias_ref[0]

    @pl.when(first)
    def _():
        kcarry_scr[...] = jnp.zeros(kcarry_scr.shape, BF16)
        vcarry_scr[...] = jnp.zeros(vcarry_scr.shape, BF16)

    pre = PRE_STRIDE if dilation > PRE_STRIDE else 1
    post = dilation // pre
    if pre > 1:
        planes, = plane_scr
        for t, ref in enumerate((q_ref, k_ref, v_ref)):
            for sl in range(SLABS):
                for a in range(pre):
                    planes[(t * SLABS + sl) * pre + a] = ref[0, sl, pl.ds(a, tile // pre, stride=pre), :]

    def load(t, ref, r, i):
        if pre > 1:
            rows = pl.ds(i * A * post + r // pre, A, stride=post)
            parts = [planes[(t * SLABS + sl) * pre + r % pre, rows, :] for sl in range(SLABS)]
        else:
            parts = [ref[0, sl, rows_of(r, i), :] for sl in range(SLABS)]
        return jnp.concatenate(parts, axis=-1).astype(BF16)

    def rows_of(r, i):
        start = i * A * dilation + r
        return pl.ds(start, A, stride=dilation) if dilation > 1 else pl.ds(start, A)

    for r in range(dilation):
        carry_rows = slice(r * A, (r + 1) * A)
        k_cur, v_cur = kcarry_scr[carry_rows, :], vcarry_scr[carry_rows, :]
        for i in range(nblk):
            k_prev, v_prev = k_cur, v_cur
            k_cur, v_cur = load(1, k_ref, r, i), load(2, v_ref, r, i)
            out, lse = _stacked_head_attention(
                load(0, q_ref, r, i), jnp.concatenate([k_prev, k_cur], axis=0),
                jnp.concatenate([v_prev, v_cur], axis=0), bias, no_prev_limit if i == 0 else None)
            for sl in range(SLABS):
                lanes = slice(sl * V7X_LANES, (sl + 1) * V7X_LANES)
                o_ref[0, sl, rows_of(r, i), :] = out[:, lanes]
                l_ref[0, sl, rows_of(r, i), :] = lse[:, lanes]
        kcarry_scr[carry_rows, :] = k_cur
        vcarry_scr[carry_rows, :] = v_cur


def _dilated_attention(q, k, v, bias, group, dilation, *, tile):
    b, _, s, _ = q.shape
    span = ATTN_BLOCK * dilation
    nblk = tile // span
    cur = pl.BlockSpec((1, SLABS, tile, V7X_LANES), lambda bi, n: (bi, group, n, 0))
    out_spec = pl.BlockSpec((1, SLABS, tile, V7X_LANES), lambda bi, n: (bi, 0, n, 0))
    out_shape = jax.ShapeDtypeStruct((b, SLABS, s, V7X_LANES), F32)
    carry = pltpu.VMEM((span, GROUP_WIDTH), BF16)
    scratch = [carry, carry]
    if dilation > PRE_STRIDE:
        scratch.append(pltpu.VMEM((3 * SLABS * PRE_STRIDE, tile // PRE_STRIDE, V7X_LANES), F32))
    return pl.pallas_call(
        functools.partial(_dilated_body, dilation=dilation, nblk=nblk),
        grid=(b, s // tile),
        in_specs=[cur, cur, cur,
                  pl.BlockSpec((1,) + bias.shape[1:], lambda bi, n: (group, 0, 0), pipeline_mode=pl.Buffered(1))],
        out_specs=[out_spec, out_spec],
        out_shape=[out_shape, out_shape],
        scratch_shapes=scratch,
        compiler_params=_compiler_params(2),
        name=f"dilated_attn_g{group}",
    )(q, k, v, bias)


def _bias_body(table_ref, bucket_ref, o_ref):
    g = pl.program_id(0)
    bucket = bucket_ref[0]
    for hh in range(HEADS_PER_GROUP):
        acc = jnp.full(bucket.shape, NEG_INF, F32)
        for bkt in range(NUM_BUCKETS):
            acc = jnp.where(bucket == bkt, table_ref[g * HEADS_PER_GROUP + hh, bkt] * LOG2E, acc)
        o_ref[0, hh * ATTN_BLOCK:(hh + 1) * ATTN_BLOCK, :] = acc


def _dilated_bias(rel_bias):
    A = ATTN_BLOCK
    m_off = np.arange(A)[:, None] + A - np.arange(2 * A)[None, :]
    band = (m_off >= 0) & (m_off <= A)
    max_exact = NUM_BUCKETS // 2
    buckets = []
    for _, dilation in DILATED_GROUPS:
        dist = np.clip(m_off, 0, A) * dilation
        large = max_exact + (np.log(np.maximum(dist, 1).astype(np.float32) / max_exact)
                             / math.log(MAX_DISTANCE / max_exact) * (NUM_BUCKETS - max_exact)).astype(np.int32)
        bucket = np.where(dist < max_exact, dist, np.minimum(large, NUM_BUCKETS - 1))
        buckets.append(np.where(band, bucket, -1))
    buckets = jnp.asarray(np.stack(buckets), jnp.int32)
    n_groups = len(DILATED_GROUPS)
    return pl.pallas_call(
        _bias_body,
        grid=(n_groups,),
        in_specs=[pl.BlockSpec(memory_space=pltpu.SMEM), pl.BlockSpec((1, A, 2 * A), lambda g: (g, 0, 0))],
        out_specs=pl.BlockSpec((1, HEADS_PER_GROUP * A, 2 * A), lambda g: (g, 0, 0)),
        out_shape=jax.ShapeDtypeStruct((n_groups, HEADS_PER_GROUP * A, 2 * A), F32),
        compiler_params=_compiler_params(1),
        name="dilated_bias",
    )(rel_bias.astype(F32).T, buckets)


def _mixout_body(x_ref, o0_ref, o1_ref, o2_ref, l0_ref, l1_ref, l2_ref, qm_ref, mk_ref, mv_ref, wout_ref,
                 gain_ref, wg_ref, wu_ref, wd_ref, out_ref, hcat_scr, h_scr, acc_scr):
    wide = lambda ref: jnp.concatenate([ref[0, sl] for sl in range(SLABS)], axis=-1)
    lses = [wide(l0_ref), wide(l1_ref), wide(l2_ref)]
    mx = jnp.maximum(jnp.maximum(lses[0], lses[1]), lses[2])
    es = [jnp.exp(l - mx) for l in lses]
    inv = 1.0 / (es[0] + es[1] + es[2])
    for g, o_ref in enumerate((o0_ref, o1_ref, o2_ref)):
        hcat_scr[:, g * GROUP_WIDTH:(g + 1) * GROUP_WIDTH] = (wide(o_ref) * (es[g] * inv)).astype(BF16)
    hcat_scr[:, DIL_WIDTH:] = _memory_attention(qm_ref[0], mk_ref[0], mv_ref[0]).astype(BF16)
    mixed = x_ref[0] + _dot(hcat_scr[...], wout_ref[...])
    out_ref[0] = _ffn_halfstep(mixed, gain_ref, wg_ref, wu_ref, wd_ref, h_scr, acc_scr)


def _mixout_ffn(x, outs, lses, q_mem, mem_k, mem_v, w_out, ffn_gain, w_gate, w_up, w_down, layer, *, ts):
    b, s, d = x.shape
    f = w_gate.shape[2]
    tile = lambda w: pl.BlockSpec((1, ts, w), lambda i, j: (i, j, 0))
    slab_tile = pl.BlockSpec((1, SLABS, ts, V7X_LANES), lambda i, j: (i, 0, j, 0))
    mem_spec = pl.BlockSpec((1,) + mem_k.shape[1:], lambda i, j: (i, 0, 0))
    return pl.pallas_call(
        _mixout_body,
        grid=(b, s // ts),
        in_specs=([tile(d)] + [slab_tile] * 6 + [tile(MEM_WIDTH), mem_spec, mem_spec, _resident(w_out.shape)]
                  + _ffn_weight_specs(d, f, layer)),
        out_specs=tile(d),
        out_shape=jax.ShapeDtypeStruct((b, s, d), F32),
        scratch_shapes=[pltpu.VMEM((ts, DIL_WIDTH + MEM_WIDTH), BF16), pltpu.VMEM((ts, d), BF16),
                        pltpu.VMEM((ts, d), F32)],
        compiler_params=_compiler_params(2),
        name="dilated_mixer_out_ffn",
    )(x, *outs, *lses, q_mem, mem_k, mem_v, w_out.astype(BF16), ffn_gain.reshape(1, d), w_gate, w_up, w_down)


def kernel(x, mem, ffn1_norm, ffn1_w_gate, ffn1_w_up, ffn1_w_down, ffn2_norm, ffn2_w_gate, ffn2_w_up, ffn2_w_down, mix_norm, mem_norm, w_mem_kv, mem_q_norm, mem_k_norm, a_w_in, a_conv, a_gate_bias, a_h_norm, a_w_out, b_w_q, b_q_norm, b_w_out, kv_norm, w_kv, kv_k_norm, rel_bias):
    b, s, d = x.shape
    n = b * s
    tm = 1024
    head_id = np.arange(GROUP_WIDTH) // HEAD_DIM
    head_avg = jnp.asarray((head_id[:, None] == head_id[None, :]) / HEAD_DIM, BF16)
    mlstm_chunk = V7X_MXU_DIM
    ltri = jnp.asarray(np.tril(np.ones((mlstm_chunk, mlstm_chunk))), BF16)
    flat = lambda t: t.reshape(n, t.shape[-1])
    ffn1_w_gate, ffn1_w_up, ffn1_w_down, ffn2_w_gate, ffn2_w_up, ffn2_w_down = (
        w.astype(BF16) for w in (ffn1_w_gate, ffn1_w_up, ffn1_w_down, ffn2_w_gate, ffn2_w_up, ffn2_w_down))
    ffn = lambda t, norm, wg, wu, wd, layer: _ffn(flat(t), norm[layer], wg, wu, wd, layer,
                                                   tm=tm).reshape(b, s, d)

    x = ffn(x, ffn1_norm, ffn1_w_gate, ffn1_w_up, ffn1_w_down, 0)
    mem_k, mem_v = _memkv(mem, mem_norm[0], w_mem_kv[0], mem_k_norm[0], head_avg)
    x = _mlstm_layer(x, mix_norm[0], a_w_in[0], a_conv[0], a_gate_bias[0], a_h_norm[0], a_w_out[0],
                     mem_k, mem_v, mem_q_norm[0], head_avg, ltri, ts=256)
    x = ffn(x, ffn2_norm, ffn2_w_gate, ffn2_w_up, ffn2_w_down, 0)

    k_sh, v_sh = _proj(x, kv_norm, w_kv, jnp.tile(kv_k_norm, DIL_WIDTH // HEAD_DIM), head_avg,
                       [(DIL_WIDTH, True), (DIL_WIDTH, True)], DIL_WIDTH, tm=tm, name="shared_kv_proj")

    x = ffn(x, ffn1_norm, ffn1_w_gate, ffn1_w_up, ffn1_w_down, 1)
    q_gain = jnp.concatenate([jnp.tile(b_q_norm[0], DIL_WIDTH // HEAD_DIM),
                              jnp.tile(mem_q_norm[1], MEM_HEADS)]) * Q_SCALE
    q_d, q_mem = _proj(x, mix_norm[1], b_w_q[0], q_gain, head_avg, [(DIL_WIDTH, True), (MEM_WIDTH, False)],
                       DIL_WIDTH + MEM_WIDTH, tm=tm, name="dilated_q_proj")
    mem_k, mem_v = _memkv(mem, mem_norm[1], w_mem_kv[1], mem_k_norm[1], head_avg)
    bias = _dilated_bias(rel_bias)
    attn_tile = max(window for window, _ in DILATED_GROUPS)
    outs, lses = [], []
    for g, (window, dilation) in enumerate(DILATED_GROUPS):
        assert window // dilation == ATTN_BLOCK and s % attn_tile == 0 and attn_tile % window == 0
        o_g, l_g = _dilated_attention(q_d, k_sh, v_sh, bias, g, dilation, tile=attn_tile)
        outs.append(o_g)
        lses.append(l_g)
    return _mixout_ffn(x, outs, lses, q_mem, mem_k, mem_v, b_w_out[0], ffn2_norm[1], ffn2_w_gate, ffn2_w_up,
                       ffn2_w_down, 1, ts=512)
```

```python
import functools
import math

import numpy as np
import jax
import jax.numpy as jnp
from jax import lax
from jax.experimental import pallas as pl
from jax.experimental.pallas import tpu as pltpu

F32 = jnp.float32
BF16 = jnp.bfloat16

HEAD_DIM = 64
MEM_HEADS = 4
MEM_WIDTH = MEM_HEADS * HEAD_DIM
MLSTM_HEADS = 4
MLSTM_HEAD_DIM = 192
MLSTM_WIDTH = MLSTM_HEADS * MLSTM_HEAD_DIM
DILATED_GROUPS = ((128, 1), (512, 4), (2048, 16))
HEADS_PER_GROUP = 4
GROUP_WIDTH = HEADS_PER_GROUP * HEAD_DIM
DIL_WIDTH = GROUP_WIDTH * len(DILATED_GROUPS)
NUM_BUCKETS = 32
MAX_DISTANCE = 2048
RMS_EPS = 1e-6
NEG_INF = -1e30
ATTN_SCALE = HEAD_DIM ** -0.5
LOG2E = math.log2(math.e)
LN2 = math.log(2.0)
Q_SCALE = ATTN_SCALE * LOG2E

V7X_LANES = 128
V7X_SUBLANES = 8
V7X_MXU_DIM = 256
V7X_VMEM_BYTES = 64 * 1024 * 1024
VMEM_LIMIT_BYTES = V7X_VMEM_BYTES - 8 * 1024 * 1024

HEAD_WIN = V7X_MXU_DIM
HEAD_WIN_START = tuple((h * MLSTM_HEAD_DIM) // V7X_LANES * V7X_LANES for h in range(MLSTM_HEADS))
HEAD_WIN_OFFSET = tuple(h * MLSTM_HEAD_DIM - s for h, s in enumerate(HEAD_WIN_START))
assert all(o + MLSTM_HEAD_DIM <= HEAD_WIN for o in HEAD_WIN_OFFSET) and MLSTM_HEAD_DIM < HEAD_WIN
HEAD_DEN_LANE = tuple((o + MLSTM_HEAD_DIM) % HEAD_WIN for o in HEAD_WIN_OFFSET)
ATTN_BLOCK = 128
LSE_LANES = V7X_LANES // HEADS_PER_GROUP


def _compiler_params(n_axes):
    return pltpu.CompilerParams(dimension_semantics=("arbitrary",) * n_axes,
                                vmem_limit_bytes=VMEM_LIMIT_BYTES)


def _resident(shape):
    zeros = (0,) * len(shape)
    return pl.BlockSpec(shape, lambda *_: zeros, pipeline_mode=pl.Buffered(1))


def _dot(a, b):
    return jnp.dot(a, b, preferred_element_type=F32)


def _dot_nt(a, b):
    return lax.dot_general(a, b, (((1,), (1,)), ((), ())), preferred_element_type=F32)


def _dot_tn(a, b):
    return lax.dot_general(a, b, (((0,), (0,)), ((), ())), preferred_element_type=F32)


def _rms(x, gain):
    ms = jnp.mean(x * x, axis=-1, keepdims=True)
    return x * lax.rsqrt(ms + RMS_EPS) * gain


def _split_bf16(x, terms):
    parts = []
    rest = x
    for _ in range(terms):
        part = rest.astype(BF16)
        parts.append(part)
        rest = rest - part.astype(F32)
    return parts


def _head_mean_sq(x, head_avg):
    return _dot((x * x).astype(BF16), head_avg)


def _head_rms(x, head_avg, gain):
    return x * lax.rsqrt(_head_mean_sq(x, head_avg) + RMS_EPS) * gain


def _silu(x):
    return x * jax.nn.sigmoid(x)


def _log_sigmoid(x):
    return jnp.minimum(x, 0.0) - jnp.log(1.0 + jnp.exp(-jnp.abs(x)))


def _head_lane_mask(head, width=GROUP_WIDTH):
    lane = lax.broadcasted_iota(jnp.int32, (1, width), 1)
    return (lane >= head * HEAD_DIM) & (lane < (head + 1) * HEAD_DIM)


def _memory_attention(qn, mem_k, mem_v):
    out, _ = _stacked_head_attention(qn, mem_k, mem_v, None, None)
    return out


def _stacked_head_attention(q, keys, values, bias, masked_below):
    t = q.shape[0]
    masks = [_head_lane_mask(h) for h in range(HEADS_PER_GROUP)]
    qs = jnp.concatenate([jnp.where(m, q, jnp.zeros_like(q)) for m in masks], axis=0)
    s = _dot_nt(qs, keys)
    if bias is not None:
        s = s + bias
    if masked_below is not None:
        key = lax.broadcasted_iota(jnp.int32, (1, keys.shape[0]), 1)
        s = jnp.where(key < masked_below, NEG_INF, s)
    mx = jnp.max(s, axis=-1, keepdims=True)
    e = jnp.exp2(s - mx)
    den = jnp.sum(e, axis=-1, keepdims=True)
    pv = _dot(e.astype(BF16), values)
    inv = 1.0 / den
    l_all = mx * LN2 + jnp.log(den)
    lane = lax.broadcasted_iota(jnp.int32, (1, V7X_LANES), 1)
    out_slabs = []
    for sl in range(q.shape[1] // V7X_LANES):
        lanes = slice(sl * V7X_LANES, (sl + 1) * V7X_LANES)
        ra, rb = slice(2 * sl * t, (2 * sl + 1) * t), slice((2 * sl + 1) * t, (2 * sl + 2) * t)
        out_slabs.append(jnp.where(lane < HEAD_DIM, pv[ra, lanes] * inv[ra], pv[rb, lanes] * inv[rb]))
    lse = l_all[(HEADS_PER_GROUP - 1) * t:]
    for h in reversed(range(HEADS_PER_GROUP - 1)):
        lse = jnp.where(lane < (h + 1) * LSE_LANES, l_all[h * t:(h + 1) * t], lse)
    return jnp.concatenate(out_slabs, axis=-1), lse


FFN_CHUNK = V7X_MXU_DIM


def _ffn_halfstep(x, gain_ref, wg_ref, wu_ref, wd_ref, h_scr, acc_scr):
    h_scr[...] = _rms(x, gain_ref[...]).astype(BF16)
    for c in range(wg_ref.shape[1] // FFN_CHUNK):
        cols = slice(c * FFN_CHUNK, (c + 1) * FFN_CHUNK)
        h = h_scr[...]
        act = (_silu(_dot(h, wg_ref[:, cols])) * _dot(h, wu_ref[:, cols])).astype(BF16)
        down = _dot(act, wd_ref[cols, :])
        if c == 0:
            acc_scr[...] = down
        else:
            acc_scr[...] += down
    return x + 0.5 * acc_scr[...]


def _ffn_body(x_ref, gain_ref, wg_ref, wu_ref, wd_ref, o_ref, h_scr, acc_scr):
    o_ref[...] = _ffn_halfstep(x_ref[...], gain_ref, wg_ref, wu_ref, wd_ref, h_scr, acc_scr)


def _ffn_weight_specs(d, f, layer):
    index = lambda *_: (layer, 0, 0)
    spec = lambda rows, cols: pl.BlockSpec((None, rows, cols), index, pipeline_mode=pl.Buffered(1))
    return [_resident((1, d)), spec(d, f), spec(d, f), spec(f, d)]


def _ffn(x2d, gain, w_gate, w_up, w_down, layer, *, tm):
    n, d = x2d.shape
    f = w_gate.shape[2]
    row_spec = pl.BlockSpec((tm, d), lambda i: (i, 0))
    return pl.pallas_call(
        _ffn_body,
        grid=(n // tm,),
        in_specs=[row_spec] + _ffn_weight_specs(d, f, layer),
        out_specs=row_spec,
        out_shape=jax.ShapeDtypeStruct((n, d), F32),
        scratch_shapes=[pltpu.VMEM((tm, d), BF16), pltpu.VMEM((tm, d), F32)],
        compiler_params=_compiler_params(1),
        name="ffn",
    )(x2d, gain.reshape(1, d), w_gate, w_up, w_down)


def _memkv_body(mem_ref, gain_ref, w_ref, kgain_ref, havg_ref, k_ref, v_ref):
    mem_n = _rms(mem_ref[0], gain_ref[...]).astype(BF16)
    kv = _dot(mem_n, w_ref[...])
    k_ref[0] = _head_rms(kv[:, :MEM_WIDTH], havg_ref[...], kgain_ref[...]).astype(BF16)
    v_ref[0] = kv[:, MEM_WIDTH:].astype(BF16)


def _memkv(mem, gain, w_mem_kv, k_gain, head_avg):
    b, m, d = mem.shape
    out_spec = pl.BlockSpec((1, m, MEM_WIDTH), lambda i: (i, 0, 0))
    return pl.pallas_call(
        _memkv_body,
        grid=(b,),
        in_specs=[pl.BlockSpec((1, m, d), lambda i: (i, 0, 0)), _resident((1, d)),
                  _resident((d, 2 * MEM_WIDTH)), _resident((1, MEM_WIDTH)),
                  _resident((MEM_WIDTH, MEM_WIDTH))],
        out_specs=[out_spec, out_spec],
        out_shape=[jax.ShapeDtypeStruct((b, m, MEM_WIDTH), BF16)] * 2,
        compiler_params=_compiler_params(1),
        name="memkv",
    )(mem, gain.reshape(1, d), w_mem_kv.astype(BF16), jnp.tile(k_gain, MEM_HEADS).reshape(1, MEM_WIDTH),
      head_avg)


_Q0, _K0, _V0, _O0 = (i * MLSTM_WIDTH for i in range(4))
_QM0 = 4 * MLSTM_WIDTH
_GI0 = _QM0 + MEM_WIDTH
_GF0 = _GI0 + V7X_LANES
A_IN_PAD_WIDTH = _GF0 + V7X_LANES
CONV_TAIL = V7X_SUBLANES


def _mlstm_body(x_ref, gain_ref, win_ref, conv_ref, bi_ref, bf_ref, hgain_ref, wout_ref,
                mk_ref, mv_ref, mqgain_ref, havg_ref, ltri_ref,
                o_ref, c_scr, m_scr, ext_scr, hcat_scr, *, ts, conv_width):
    @pl.when(pl.program_id(0) == 0)
    def _():
        c_scr[...] = jnp.zeros(c_scr.shape, F32)
        m_scr[...] = jnp.zeros(m_scr.shape, F32)
        ext_scr[:, :, 0:CONV_TAIL, :] = jnp.zeros(ext_scr.shape[:2] + (CONV_TAIL, V7X_LANES), F32)

    nb = x_ref.shape[0]
    projections = [_dot(_rms(x_ref[bi], gain_ref[...]).astype(BF16), win_ref[...]) for bi in range(nb)]
    for bi in range(nb):
        _mlstm_tile(projections[bi], bi, conv_ref, bi_ref, bf_ref, hgain_ref, mk_ref, mv_ref,
                    mqgain_ref, havg_ref, ltri_ref, c_scr, m_scr, ext_scr, hcat_scr, ts=ts,
                    conv_width=conv_width)
        rows = slice(bi * ts, (bi + 1) * ts)
        o_ref[bi] = x_ref[bi] + _dot(hcat_scr[rows, :], wout_ref[...])


def _mlstm_tile(p, bi, conv_ref, bi_ref, bf_ref, hgain_ref, mk_ref, mv_ref, mqgain_ref, havg_ref, ltri_ref,
                c_scr, m_scr, ext_scr, hcat_scr, *, ts, conv_width):
    out_rows = slice(bi * ts, (bi + 1) * ts)
    cw = conv_ref[...]
    conv_slabs = []
    for sl in range(ext_scr.shape[1]):
        lanes = slice(sl * V7X_LANES, (sl + 1) * V7X_LANES)
        pre = p[:, lanes]
        ext_scr[bi, sl, CONV_TAIL:CONV_TAIL + ts, :] = pre
        y = cw[conv_width - 1:conv_width, lanes] * pre
        for back in range(1, conv_width):
            tap = conv_width - 1 - back
            y = y + cw[tap:tap + 1, lanes] * ext_scr[bi, sl, CONV_TAIL - back:CONV_TAIL - back + ts, :]
        ext_scr[bi, sl, 0:CONV_TAIL, :] = pre[ts - CONV_TAIL:ts, :]
        conv_slabs.append(_silu(y))
    qk = jnp.concatenate(conv_slabs, axis=-1)
    q = qk[:, :MLSTM_WIDTH].astype(BF16)
    k = qk[:, MLSTM_WIDTH:] * (MLSTM_HEAD_DIM ** -0.5)
    v = p[:, _V0:_O0].astype(BF16)
    o_gate = p[:, _O0:_QM0]
    gate_i = p[:, _GI0:_GF0] + bi_ref[...]
    log_f = _log_sigmoid(p[:, _GF0:A_IN_PAD_WIDTH] + bf_ref[...])

    L = ltri_ref.shape[0]
    row = lax.broadcasted_iota(jnp.int32, (L, L), 0)
    col = lax.broadcasted_iota(jnp.int32, (L, L), 1)
    causal = col <= row
    win_lane = lax.broadcasted_iota(jnp.int32, (1, HEAD_WIN), 1)
    ltri = ltri_ref[...]
    for c in range(ts // L):
        rows = slice(c * L, (c + 1) * L)
        cells = []
        b = sum(_dot(ltri, part) for part in _split_bf16(log_f[rows], 3))
        b_last = b[L - 1:L, :]
        r = gate_i[rows] - b
        a = b_last + r
        m0 = m_scr[bi]
        m_new = jnp.maximum(b_last + m0, jnp.max(a, axis=0, keepdims=True))
        w = jnp.exp(a - m_new)
        decay = jnp.exp(b_last + m0 - m_new)
        g = b + m0
        r_rows = r.T
        for hh in range(MLSTM_HEADS):
            win = slice(HEAD_WIN_START[hh], HEAD_WIN_START[hh] + HEAD_WIN)
            off, den_lane = HEAD_WIN_OFFSET[hh], HEAD_DEN_LANE[hh]
            in_head = (win_lane >= off) & (win_lane < off + MLSTM_HEAD_DIM)
            qh = jnp.where(in_head, q[rows, win], jnp.zeros((), BF16))
            kh = k[rows, win]
            ones_lane = jnp.where(win_lane == den_lane, 1.0, 0.0).astype(BF16)
            vh = jnp.where(in_head, v[rows, win], ones_lane)
            d_log = jnp.where(causal, b[:, hh:hh + 1] + r_rows[hh:hh + 1, :], NEG_INF)
            g_col = g[:, hh:hh + 1]
            m_t = jnp.maximum(g_col, jnp.max(d_log, axis=-1, keepdims=True))
            wmat = jnp.exp(d_log - m_t) * _dot_nt(qh, kh.astype(BF16))
            c0 = c_scr[bi, hh]
            num = _dot(wmat.astype(BF16), vh) + jnp.exp(g_col - m_t) * _dot(qh, c0.astype(BF16))
            den = num[:, den_lane:den_lane + 1]
            cell = num / jnp.maximum(jnp.abs(den), jnp.exp(-m_t))
            cell = jnp.where(in_head, cell, 0.0)
            ms = jnp.sum(cell * cell, axis=-1, keepdims=True) * (1.0 / MLSTM_HEAD_DIM)
            cells.append(cell * lax.rsqrt(ms + RMS_EPS))
            kw = (kh * w[:, hh:hh + 1]).astype(BF16)
            c_scr[bi, hh] = decay[:, hh:hh + 1] * c0 + _dot_tn(kw, vh)
        m_scr[bi] = m_new
        pieces = {}
        for hh, cell in enumerate(cells):
            for half in range(HEAD_WIN // V7X_LANES):
                tile_idx = HEAD_WIN_START[hh] // V7X_LANES + half
                piece = cell[:, half * V7X_LANES:(half + 1) * V7X_LANES]
                pieces[tile_idx] = piece if tile_idx not in pieces else pieces[tile_idx] + piece
        cell_all = jnp.concatenate([pieces[t] for t in range(MLSTM_WIDTH // V7X_LANES)], axis=-1)
        hcat_scr[bi * ts + c * L:bi * ts + (c + 1) * L, :MLSTM_WIDTH] = (
            jax.nn.sigmoid(o_gate[rows]) * cell_all * hgain_ref[...]).astype(BF16)

    qm = _head_rms(p[:, _QM0:_GI0], havg_ref[...], mqgain_ref[...] * Q_SCALE).astype(BF16)
    hcat_scr[out_rows, MLSTM_WIDTH:] = _memory_attention(qm, mk_ref[bi], mv_ref[bi]).astype(BF16)


def _pad_lanes(w):
    return jnp.pad(w, [(0, 0)] * (w.ndim - 1) + [(0, V7X_LANES - w.shape[-1])])


def _mlstm_layer(x, gain, w_in, conv_w, gate_bias, h_gain, w_out, mem_k, mem_v, mq_gain,
                 head_avg, ltri, *, ts):
    b, s, d = x.shape
    W, H = MLSTM_WIDTH, MLSTM_HEADS
    w_in_p = jnp.concatenate([w_in[:, :4 * W], w_in[:, 4 * W + 2 * H:], _pad_lanes(w_in[:, 4 * W:4 * W + H]),
                              _pad_lanes(w_in[:, 4 * W + H:4 * W + 2 * H])], axis=1).astype(BF16)
    conv_p = conv_w
    w_out_p = w_out.astype(BF16)
    conv_width = conv_w.shape[0]
    tile = pl.BlockSpec((b, ts, d), lambda j: (0, j, 0))
    return pl.pallas_call(
        functools.partial(_mlstm_body, ts=ts, conv_width=conv_width),
        grid=(s // ts,),
        in_specs=[tile, _resident((1, d)), _resident(w_in_p.shape), _resident(conv_p.shape),
                  _resident((1, V7X_LANES)), _resident((1, V7X_LANES)), _resident((1, W)),
                  _resident(w_out_p.shape), _resident(mem_k.shape), _resident(mem_v.shape),
                  _resident((1, MEM_WIDTH)), _resident(head_avg.shape), _resident(ltri.shape)],
        out_specs=tile,
        out_shape=jax.ShapeDtypeStruct((b, s, d), F32),
        scratch_shapes=[pltpu.VMEM((b, H, HEAD_WIN, HEAD_WIN), F32), pltpu.VMEM((b, 1, V7X_LANES), F32),
                        pltpu.VMEM((b, 2 * W // V7X_LANES, CONV_TAIL + ts, V7X_LANES), F32),
                        pltpu.VMEM((b * ts, W + MEM_WIDTH), BF16)],
        compiler_params=_compiler_params(1),
        name="mlstm_mixer",
    )(x, gain.reshape(1, d), w_in_p, conv_p, _pad_lanes(gate_bias[:H].reshape(1, H)),
      _pad_lanes(gate_bias[H:].reshape(1, H)), h_gain.reshape(1, W), w_out_p,
      mem_k, mem_v, jnp.tile(mq_gain, MEM_HEADS).reshape(1, MEM_WIDTH), head_avg, ltri)


U32 = jnp.uint32


def _pack_bf16_pair(x):
    hi = pltpu.bitcast(x[:, :V7X_LANES].astype(BF16).astype(F32), U32)
    lo = pltpu.bitcast(x[:, V7X_LANES:].astype(BF16).astype(F32), U32)
    return hi | (lo >> 16)


def _unpack_bf16_pair(w):
    return pltpu.bitcast(w & U32(0xFFFF0000), F32), pltpu.bitcast(w << 16, F32)


def _proj_body(x_ref, gain_ref, w_ref, hgain_ref, havg_ref, *out_refs, normed_width):
    h = _rms(x_ref[0], gain_ref[...]).astype(BF16)
    y = _dot(h, w_ref[...])
    pieces = []
    for c in range(y.shape[1] // GROUP_WIDTH):
        cols = slice(c * GROUP_WIDTH, (c + 1) * GROUP_WIDTH)
        if c * GROUP_WIDTH < normed_width:
            pieces.append(_head_rms(y[:, cols], havg_ref[...], hgain_ref[:, cols]))
        else:
            pieces.append(y[:, cols])
    start = 0
    for ref in out_refs:
        if len(ref.shape) == 4:
            n = ref.shape[1]
            for c in range(n):
                ref[0, c] = _pack_bf16_pair(pieces[start + c])
        else:
            n = ref.shape[2] // GROUP_WIDTH
            for c in range(n):
                ref[0, :, c * GROUP_WIDTH:(c + 1) * GROUP_WIDTH] = pieces[start + c].astype(BF16)
        start += n


def _proj(x, gain, w, head_gain, head_avg, outputs, normed_width, *, tm, name):
    b, s, d = x.shape
    width = w.shape[1]
    out_specs, out_shapes = [], []
    for ow, packed in outputs:
        if packed:
            out_specs.append(pl.BlockSpec((1, ow // GROUP_WIDTH, tm, V7X_LANES), lambda i, j: (i, 0, j, 0)))
            out_shapes.append(jax.ShapeDtypeStruct((b, ow // GROUP_WIDTH, s, V7X_LANES), U32))
        else:
            out_specs.append(pl.BlockSpec((1, tm, ow), lambda i, j: (i, j, 0)))
            out_shapes.append(jax.ShapeDtypeStruct((b, s, ow), BF16))
    return pl.pallas_call(
        functools.partial(_proj_body, normed_width=normed_width),
        grid=(b, s // tm),
        in_specs=[pl.BlockSpec((1, tm, d), lambda i, j: (i, j, 0)), _resident((1, d)), _resident((d, width)),
                  _resident((1, normed_width)), _resident(head_avg.shape)],
        out_specs=out_specs,
        out_shape=out_shapes,
        compiler_params=_compiler_params(2),
        name=name,
    )(x, gain.reshape(1, d), w.astype(BF16), head_gain.reshape(1, normed_width), head_avg)


PRE_STRIDE = 4


def _dilated_body(q_ref, k_ref, v_ref, bias_ref, o_ref, l_ref, kcarry_scr, vcarry_scr, *plane_scr,
                  dilation, nblk):
    A = ATTN_BLOCK
    tile = dilation * nblk * A
    first = pl.program_id(1) == 0
    no_prev_limit = jnp.where(first, A, 0)
    bias = bias_ref[0]

    @pl.when(first)
    def _():
        kcarry_scr[...] = jnp.zeros(kcarry_scr.shape, BF16)
        vcarry_scr[...] = jnp.zeros(vcarry_scr.shape, BF16)

    pre = PRE_STRIDE if dilation > PRE_STRIDE else 1
    post = dilation // pre
    if pre > 1:
        planes, = plane_scr
        for t, ref in enumerate((q_ref, k_ref, v_ref)):
            for a in range(pre):
                planes[t * pre + a] = ref[0, 0, pl.ds(a, tile // pre, stride=pre), :]

    def load(t, ref, r, i):
        if pre > 1:
            words = planes[t * pre + r % pre, pl.ds(i * A * post + r // pre, A, stride=post), :]
        else:
            words = ref[0, 0, rows_of(r, i), :]
        return jnp.concatenate(_unpack_bf16_pair(words), axis=-1).astype(BF16)

    def rows_of(r, i):
        start = i * A * dilation + r
        return pl.ds(start, A, stride=dilation) if dilation > 1 else pl.ds(start, A)

    for r in range(dilation):
        carry_rows = slice(r * A, (r + 1) * A)
        k_cur, v_cur = kcarry_scr[carry_rows, :], vcarry_scr[carry_rows, :]
        for i in range(nblk):
            k_prev, v_prev = k_cur, v_cur
            k_cur, v_cur = load(1, k_ref, r, i), load(2, v_ref, r, i)
            out, lse = _stacked_head_attention(
                load(0, q_ref, r, i), jnp.concatenate([k_prev, k_cur], axis=0),
                jnp.concatenate([v_prev, v_cur], axis=0), bias, no_prev_limit if i == 0 else None)
            o_ref[0, 0, rows_of(r, i), :] = _pack_bf16_pair(out)
            l_ref[0, 0, rows_of(r, i), :] = lse
        kcarry_scr[carry_rows, :] = k_cur
        vcarry_scr[carry_rows, :] = v_cur


def _dilated_attention(q, k, v, bias, group, dilation, *, tile):
    b, _, s, _ = q.shape
    span = ATTN_BLOCK * dilation
    nblk = tile // span
    cur = pl.BlockSpec((1, 1, tile, V7X_LANES), lambda bi, n: (bi, group, n, 0))
    out_spec = pl.BlockSpec((1, 1, tile, V7X_LANES), lambda bi, n: (bi, 0, n, 0))
    carry = pltpu.VMEM((span, GROUP_WIDTH), BF16)
    scratch = [carry, carry]
    if dilation > PRE_STRIDE:
        scratch.append(pltpu.VMEM((3 * PRE_STRIDE, tile // PRE_STRIDE, V7X_LANES), U32))
    return pl.pallas_call(
        functools.partial(_dilated_body, dilation=dilation, nblk=nblk),
        grid=(b, s // tile),
        in_specs=[cur, cur, cur,
                  pl.BlockSpec((1,) + bias.shape[1:], lambda bi, n: (group, 0, 0), pipeline_mode=pl.Buffered(1))],
        out_specs=[out_spec, out_spec],
        out_shape=[jax.ShapeDtypeStruct((b, 1, s, V7X_LANES), U32),
                   jax.ShapeDtypeStruct((b, 1, s, V7X_LANES), F32)],
        scratch_shapes=scratch,
        compiler_params=_compiler_params(2),
        name=f"dilated_attn_g{group}",
    )(q, k, v, bias)


def _bias_body(table_ref, bucket_ref, o_ref):
    g = pl.program_id(0)
    bucket = bucket_ref[0]
    for hh in range(HEADS_PER_GROUP):
        acc = jnp.full(bucket.shape, NEG_INF, F32)
        for bkt in range(NUM_BUCKETS):
            acc = jnp.where(bucket == bkt, table_ref[g * HEADS_PER_GROUP + hh, bkt] * LOG2E, acc)
        o_ref[0, hh * ATTN_BLOCK:(hh + 1) * ATTN_BLOCK, :] = acc


def _dilated_bias(rel_bias):
    A = ATTN_BLOCK
    m_off = np.arange(A)[:, None] + A - np.arange(2 * A)[None, :]
    band = (m_off >= 0) & (m_off <= A)
    max_exact = NUM_BUCKETS // 2
    buckets = []
    for _, dilation in DILATED_GROUPS:
        dist = np.clip(m_off, 0, A) * dilation
        large = max_exact + (np.log(np.maximum(dist, 1).astype(np.float32) / max_exact)
                             / math.log(MAX_DISTANCE / max_exact) * (NUM_BUCKETS - max_exact)).astype(np.int32)
        bucket = np.where(dist < max_exact, dist, np.minimum(large, NUM_BUCKETS - 1))
        buckets.append(np.where(band, bucket, -1))
    buckets = jnp.asarray(np.stack(buckets), jnp.int32)
    n_groups = len(DILATED_GROUPS)
    return pl.pallas_call(
        _bias_body,
        grid=(n_groups,),
        in_specs=[pl.BlockSpec(memory_space=pltpu.SMEM), pl.BlockSpec((1, A, 2 * A), lambda g: (g, 0, 0))],
        out_specs=pl.BlockSpec((1, HEADS_PER_GROUP * A, 2 * A), lambda g: (g, 0, 0)),
        out_shape=jax.ShapeDtypeStruct((n_groups, HEADS_PER_GROUP * A, 2 * A), F32),
        compiler_params=_compiler_params(1),
        name="dilated_bias",
    )(rel_bias.astype(F32).T, buckets)


def _mixout_body(x_ref, o0_ref, o1_ref, o2_ref, l0_ref, l1_ref, l2_ref, qm_ref, mk_ref, mv_ref, wout_ref,
                 gain_ref, wg_ref, wu_ref, wd_ref, out_ref, hcat_scr, h_scr, acc_scr):
    lses = [l0_ref[0, 0], l1_ref[0, 0], l2_ref[0, 0]]
    mx = jnp.maximum(jnp.maximum(lses[0], lses[1]), lses[2])
    es = [jnp.exp(l - mx) for l in lses]
    inv = 1.0 / (es[0] + es[1] + es[2])
    lane = lax.broadcasted_iota(jnp.int32, (1, V7X_LANES), 1)
    for g, o_ref in enumerate((o0_ref, o1_ref, o2_ref)):
        alpha = es[g] * inv
        r1, r2, r3 = (pltpu.roll(alpha, k * LSE_LANES, 1) for k in (1, 2, 3))
        alpha_lo = jnp.where(lane < LSE_LANES, alpha, jnp.where(lane < 3 * LSE_LANES, r1, r2))
        alpha_hi = jnp.where(lane < LSE_LANES, r2, jnp.where(lane < 3 * LSE_LANES, r3, alpha))
        out_lo, out_hi = _unpack_bf16_pair(o_ref[0, 0])
        base = g * GROUP_WIDTH
        hcat_scr[:, base:base + V7X_LANES] = (out_lo * alpha_lo).astype(BF16)
        hcat_scr[:, base + V7X_LANES:base + GROUP_WIDTH] = (out_hi * alpha_hi).astype(BF16)
    hcat_scr[:, DIL_WIDTH:] = _memory_attention(qm_ref[0], mk_ref[0], mv_ref[0]).astype(BF16)
    mixed = x_ref[0] + _dot(hcat_scr[...], wout_ref[...])
    out_ref[0] = _ffn_halfstep(mixed, gain_ref, wg_ref, wu_ref, wd_ref, h_scr, acc_scr)


def _mixout_ffn(x, outs, lses, q_mem, mem_k, mem_v, w_out, ffn_gain, w_gate, w_up, w_down, layer, *, ts):
    b, s, d = x.shape
    f = w_gate.shape[2]
    tile = lambda w: pl.BlockSpec((1, ts, w), lambda i, j: (i, j, 0))
    slab_tile = pl.BlockSpec((1, 1, ts, V7X_LANES), lambda i, j: (i, 0, j, 0))
    mem_spec = pl.BlockSpec((1,) + mem_k.shape[1:], lambda i, j: (i, 0, 0))
    return pl.pallas_call(
        _mixout_body,
        grid=(b, s // ts),
        in_specs=([tile(d)] + [slab_tile] * 6 + [tile(MEM_WIDTH), mem_spec, mem_spec, _resident(w_out.shape)]
                  + _ffn_weight_specs(d, f, layer)),
        out_specs=tile(d),
        out_shape=jax.ShapeDtypeStruct((b, s, d), F32),
        scratch_shapes=[pltpu.VMEM((ts, DIL_WIDTH + MEM_WIDTH), BF16), pltpu.VMEM((ts, d), BF16),
                        pltpu.VMEM((ts, d), F32)],
        compiler_params=_compiler_params(2),
        name="dilated_mixer_out_ffn",
    )(x, *outs, *lses, q_mem, mem_k, mem_v, w_out.astype(BF16), ffn_gain.reshape(1, d), w_gate, w_up, w_down)


def kernel(x, mem, ffn1_norm, ffn1_w_gate, ffn1_w_up, ffn1_w_down, ffn2_norm, ffn2_w_gate, ffn2_w_up, ffn2_w_down, mix_norm, mem_norm, w_mem_kv, mem_q_norm, mem_k_norm, a_w_in, a_conv, a_gate_bias, a_h_norm, a_w_out, b_w_q, b_q_norm, b_w_out, kv_norm, w_kv, kv_k_norm, rel_bias):
    b, s, d = x.shape
    n = b * s
    tm = 1024
    head_id = np.arange(GROUP_WIDTH) // HEAD_DIM
    head_avg = jnp.asarray((head_id[:, None] == head_id[None, :]) / HEAD_DIM, BF16)
    mlstm_chunk = V7X_MXU_DIM
    ltri = jnp.asarray(np.tril(np.ones((mlstm_chunk, mlstm_chunk))), BF16)
    flat = lambda t: t.reshape(n, t.shape[-1])
    ffn1_w_gate, ffn1_w_up, ffn1_w_down, ffn2_w_gate, ffn2_w_up, ffn2_w_down = (
        w.astype(BF16) for w in (ffn1_w_gate, ffn1_w_up, ffn1_w_down, ffn2_w_gate, ffn2_w_up, ffn2_w_down))
    ffn = lambda t, norm, wg, wu, wd, layer: _ffn(flat(t), norm[layer], wg, wu, wd, layer,
                                                   tm=tm).reshape(b, s, d)

    x = ffn(x, ffn1_norm, ffn1_w_gate, ffn1_w_up, ffn1_w_down, 0)
    mem_k, mem_v = _memkv(mem, mem_norm[0], w_mem_kv[0], mem_k_norm[0], head_avg)
    x = _mlstm_layer(x, mix_norm[0], a_w_in[0], a_conv[0], a_gate_bias[0], a_h_norm[0], a_w_out[0],
                     mem_k, mem_v, mem_q_norm[0], head_avg, ltri, ts=256)
    x = ffn(x, ffn2_norm, ffn2_w_gate, ffn2_w_up, ffn2_w_down, 0)

    k_sh, v_sh = _proj(x, kv_norm, w_kv, jnp.tile(kv_k_norm, DIL_WIDTH // HEAD_DIM), head_avg,
                       [(DIL_WIDTH, True), (DIL_WIDTH, True)], DIL_WIDTH, tm=tm, name="shared_kv_proj")

    x = ffn(x, ffn1_norm, ffn1_w_gate, ffn1_w_up, ffn1_w_down, 1)
    q_gain = jnp.concatenate([jnp.tile(b_q_norm[0], DIL_WIDTH // HEAD_DIM),
                              jnp.tile(mem_q_norm[1], MEM_HEADS)]) * Q_SCALE
    q_d, q_mem = _proj(x, mix_norm[1], b_w_q[0], q_gain, head_avg, [(DIL_WIDTH, True), (MEM_WIDTH, False)],
                       DIL_WIDTH + MEM_WIDTH, tm=tm, name="dilated_q_proj")
    mem_k, mem_v = _memkv(mem, mem_norm[1], w_mem_kv[1], mem_k_norm[1], head_avg)
    bias = _dilated_bias(rel_bias)
    attn_tile = max(window for window, _ in DILATED_GROUPS)
    outs, lses = [], []
    for g, (window, dilation) in enumerate(DILATED_GROUPS):
        assert window // dilation == ATTN_BLOCK and s % attn_tile == 0 and attn_tile % window == 0
        o_g, l_g = _dilated_attention(q_d, k_sh, v_sh, bias, g, dilation, tile=attn_tile)
        outs.append(o_g)
        lses.append(l_g)
    return _mixout_ffn(x, outs, lses, q_mem, mem_k, mem_v, b_w_out[0], ffn2_norm[1], ffn2_w_gate, ffn2_w_up,
                       ffn2_w_down, 1, ts=512)
```

```python
import functools
import math

import numpy as np
import jax
import jax.numpy as jnp
from jax import lax
from jax.experimental import pallas as pl
from jax.experimental.pallas import tpu as pltpu

F32 = jnp.float32
BF16 = jnp.bfloat16

HEAD_DIM = 64
MEM_HEADS = 4
MEM_WIDTH = MEM_HEADS * HEAD_DIM
MLSTM_HEADS = 4
MLSTM_HEAD_DIM = 192
MLSTM_WIDTH = MLSTM_HEADS * MLSTM_HEAD_DIM
DILATED_GROUPS = ((128, 1), (512, 4), (2048, 16))
HEADS_PER_GROUP = 4
GROUP_WIDTH = HEADS_PER_GROUP * HEAD_DIM
DIL_WIDTH = GROUP_WIDTH * len(DILATED_GROUPS)
NUM_BUCKETS = 32
MAX_DISTANCE = 2048
RMS_EPS = 1e-6
NEG_INF = -1e30
ATTN_SCALE = HEAD_DIM ** -0.5
LOG2E = math.log2(math.e)
LN2 = math.log(2.0)
Q_SCALE = ATTN_SCALE * LOG2E

V7X_LANES = 128
V7X_SUBLANES = 8
V7X_MXU_DIM = 256
V7X_VMEM_BYTES = 64 * 1024 * 1024
VMEM_LIMIT_BYTES = V7X_VMEM_BYTES - 8 * 1024 * 1024

HEAD_WIN = V7X_MXU_DIM
HEAD_WIN_START = tuple((h * MLSTM_HEAD_DIM) // V7X_LANES * V7X_LANES for h in range(MLSTM_HEADS))
HEAD_WIN_OFFSET = tuple(h * MLSTM_HEAD_DIM - s for h, s in enumerate(HEAD_WIN_START))
assert all(o + MLSTM_HEAD_DIM <= HEAD_WIN for o in HEAD_WIN_OFFSET) and MLSTM_HEAD_DIM < HEAD_WIN
HEAD_DEN_LANE = tuple((o + MLSTM_HEAD_DIM) % HEAD_WIN for o in HEAD_WIN_OFFSET)
ATTN_BLOCK = 128


def _compiler_params(n_axes):
    return pltpu.CompilerParams(dimension_semantics=("arbitrary",) * n_axes,
                                vmem_limit_bytes=VMEM_LIMIT_BYTES)


def _resident(shape):
    zeros = (0,) * len(shape)
    return pl.BlockSpec(shape, lambda *_: zeros, pipeline_mode=pl.Buffered(1))


def _dot(a, b):
    return jnp.dot(a, b, preferred_element_type=F32)


def _dot_nt(a, b):
    return lax.dot_general(a, b, (((1,), (1,)), ((), ())), preferred_element_type=F32)


def _dot_tn(a, b):
    return lax.dot_general(a, b, (((0,), (0,)), ((), ())), preferred_element_type=F32)


def _rms(x, gain):
    ms = jnp.mean(x * x, axis=-1, keepdims=True)
    return x * lax.rsqrt(ms + RMS_EPS) * gain


def _split_bf16(x, terms):
    parts = []
    rest = x
    for _ in range(terms):
        part = rest.astype(BF16)
        parts.append(part)
        rest = rest - part.astype(F32)
    return parts


def _head_mean_sq(x, head_avg):
    return _dot((x * x).astype(BF16), head_avg)


def _head_rms(x, head_avg, gain):
    return x * lax.rsqrt(_head_mean_sq(x, head_avg) + RMS_EPS) * gain


def _silu(x):
    return x * jax.nn.sigmoid(x)


def _log_sigmoid(x):
    return jnp.minimum(x, 0.0) - jnp.log(1.0 + jnp.exp(-jnp.abs(x)))


def _head_lane_mask(head, width=GROUP_WIDTH):
    lane = lax.broadcasted_iota(jnp.int32, (1, width), 1)
    return (lane >= head * HEAD_DIM) & (lane < (head + 1) * HEAD_DIM)


def _memory_attention(qn, mem_k, mem_v):
    out, _ = _stacked_head_attention(qn, mem_k, mem_v, None, None)
    return out


def _stacked_head_attention(q, keys, values, bias, masked_below):
    t = q.shape[0]
    masks = [_head_lane_mask(h) for h in range(HEADS_PER_GROUP)]
    qs = jnp.concatenate([jnp.where(m, q, jnp.zeros_like(q)) for m in masks], axis=0)
    s = _dot_nt(qs, keys)
    if bias is not None:
        s = s + bias
    if masked_below is not None:
        key = lax.broadcasted_iota(jnp.int32, (1, keys.shape[0]), 1)
        s = jnp.where(key < masked_below, NEG_INF, s)
    mx = jnp.max(s, axis=-1, keepdims=True)
    e = jnp.exp2(s - mx)
    den = jnp.sum(e, axis=-1, keepdims=True)
    pv = _dot(e.astype(BF16), values)
    inv = 1.0 / den
    l_all = mx * LN2 + jnp.log(den)
    first_head = lax.broadcasted_iota(jnp.int32, (1, V7X_LANES), 1) < HEAD_DIM
    out_slabs, lse_slabs = [], []
    for sl in range(q.shape[1] // V7X_LANES):
        lanes = slice(sl * V7X_LANES, (sl + 1) * V7X_LANES)
        ra, rb = slice(2 * sl * t, (2 * sl + 1) * t), slice((2 * sl + 1) * t, (2 * sl + 2) * t)
        out_slabs.append(jnp.where(first_head, pv[ra, lanes] * inv[ra], pv[rb, lanes] * inv[rb]))
        lse_slabs.append(jnp.where(first_head, l_all[ra], l_all[rb]))
    return jnp.concatenate(out_slabs, axis=-1), jnp.concatenate(lse_slabs, axis=-1)


FFN_CHUNK = V7X_MXU_DIM


STAGE_SLOTS = 2


def _ffn_halfstep(x, gain_ref, w_hbm, w_scr, stage, sems, h_scr, acc_scr, *, layer, load_weights):
    wg_scr, wu_scr, wd_scr = w_scr
    n_chunks = wg_scr.shape[1] // FFN_CHUNK

    def chunk_copies(c, slot):
        cols = pl.ds(c * FFN_CHUNK, FFN_CHUNK)
        srcs = (w_hbm[0].at[layer, :, cols], w_hbm[1].at[layer, :, cols], w_hbm[2].at[layer, cols, :])
        return [pltpu.make_async_copy(src, stage[k].at[slot], sems.at[slot, k]) for k, src in enumerate(srcs)]

    h_scr[...] = _rms(x, gain_ref[...]).astype(BF16)
    if load_weights:
        for copy in chunk_copies(0, 0):
            copy.start()
    for c in range(n_chunks):
        cols = slice(c * FFN_CHUNK, (c + 1) * FFN_CHUNK)
        if load_weights:
            slot = c % STAGE_SLOTS
            if c + 1 < n_chunks:
                for copy in chunk_copies(c + 1, (c + 1) % STAGE_SLOTS):
                    copy.start()
            for copy in chunk_copies(c, slot):
                copy.wait()
            wg_scr[:, cols] = stage[0][slot].astype(BF16)
            wu_scr[:, cols] = stage[1][slot].astype(BF16)
            wd_scr[cols, :] = stage[2][slot].astype(BF16)
        h = h_scr[...]
        act = (_silu(_dot(h, wg_scr[:, cols])) * _dot(h, wu_scr[:, cols])).astype(BF16)
        down = _dot(act, wd_scr[cols, :])
        if c == 0:
            acc_scr[...] = down
        else:
            acc_scr[...] += down
    return x + 0.5 * acc_scr[...]


def _ffn_first_or_later(first, write, x, gain_ref, ffn_refs, *, layer):
    w_hbm, w_scr, stage, sems, h_scr, acc_scr = ffn_refs[:3], ffn_refs[3:6], ffn_refs[6:9], *ffn_refs[9:]

    @pl.when(first)
    def _():
        write(_ffn_halfstep(x, gain_ref, w_hbm, w_scr, stage, sems, h_scr, acc_scr, layer=layer,
                            load_weights=True))

    @pl.when(jnp.logical_not(first))
    def _():
        write(_ffn_halfstep(x, gain_ref, w_hbm, w_scr, stage, sems, h_scr, acc_scr, layer=layer,
                            load_weights=False))


def _ffn_body(x_ref, gain_ref, wg_hbm, wu_hbm, wd_hbm, o_ref, *scratch, layer):
    def write(value):
        o_ref[...] = value
    _ffn_first_or_later(pl.program_id(0) == 0, write, x_ref[...], gain_ref,
                        (wg_hbm, wu_hbm, wd_hbm) + scratch, layer=layer)


def _ffn_weight_specs(d):
    return [_resident((1, d))] + [pl.BlockSpec(memory_space=pl.ANY)] * 3


def _ffn_scratch(tm, d, f):
    return [pltpu.VMEM((d, f), BF16), pltpu.VMEM((d, f), BF16), pltpu.VMEM((f, d), BF16),
            pltpu.VMEM((STAGE_SLOTS, d, FFN_CHUNK), F32), pltpu.VMEM((STAGE_SLOTS, d, FFN_CHUNK), F32),
            pltpu.VMEM((STAGE_SLOTS, FFN_CHUNK, d), F32), pltpu.SemaphoreType.DMA((STAGE_SLOTS, 3)),
            pltpu.VMEM((tm, d), BF16), pltpu.VMEM((tm, d), F32)]


def _ffn(x2d, gain, w_gate, w_up, w_down, layer, *, tm):
    n, d = x2d.shape
    f = w_gate.shape[2]
    row_spec = pl.BlockSpec((tm, d), lambda i: (i, 0))
    return pl.pallas_call(
        functools.partial(_ffn_body, layer=layer),
        grid=(n // tm,),
        in_specs=[row_spec] + _ffn_weight_specs(d),
        out_specs=row_spec,
        out_shape=jax.ShapeDtypeStruct((n, d), F32),
        scratch_shapes=_ffn_scratch(tm, d, f),
        compiler_params=_compiler_params(1),
        name="ffn",
    )(x2d, gain.reshape(1, d), w_gate, w_up, w_down)


def _memkv_body(mem_ref, gain_ref, w_ref, kgain_ref, havg_ref, k_ref, v_ref):
    mem_n = _rms(mem_ref[0], gain_ref[...]).astype(BF16)
    kv = _dot(mem_n, w_ref[...])
    k_ref[0] = _head_rms(kv[:, :MEM_WIDTH], havg_ref[...], kgain_ref[...]).astype(BF16)
    v_ref[0] = kv[:, MEM_WIDTH:].astype(BF16)


def _memkv(mem, gain, w_mem_kv, k_gain, head_avg):
    b, m, d = mem.shape
    out_spec = pl.BlockSpec((1, m, MEM_WIDTH), lambda i: (i, 0, 0))
    return pl.pallas_call(
        _memkv_body,
        grid=(b,),
        in_specs=[pl.BlockSpec((1, m, d), lambda i: (i, 0, 0)), _resident((1, d)),
                  _resident((d, 2 * MEM_WIDTH)), _resident((1, MEM_WIDTH)),
                  _resident((MEM_WIDTH, MEM_WIDTH))],
        out_specs=[out_spec, out_spec],
        out_shape=[jax.ShapeDtypeStruct((b, m, MEM_WIDTH), BF16)] * 2,
        compiler_params=_compiler_params(1),
        name="memkv",
    )(mem, gain.reshape(1, d), w_mem_kv.astype(BF16), jnp.tile(k_gain, MEM_HEADS).reshape(1, MEM_WIDTH),
      head_avg)


_Q0, _K0, _V0, _O0 = (i * MLSTM_WIDTH for i in range(4))
_QM0 = 4 * MLSTM_WIDTH
_GI0 = _QM0 + MEM_WIDTH
_GF0 = _GI0 + V7X_LANES
A_IN_PAD_WIDTH = _GF0 + V7X_LANES
CONV_TAIL = V7X_SUBLANES


def _mlstm_body(x_ref, gain_ref, win_ref, conv_ref, bi_ref, bf_ref, hgain_ref, wout_ref,
                mk_ref, mv_ref, mqgain_ref, havg_ref, ltri_ref,
                o_ref, c_scr, m_scr, ext_scr, hcat_scr, *, ts, conv_width):
    @pl.when(pl.program_id(0) == 0)
    def _():
        c_scr[...] = jnp.zeros(c_scr.shape, F32)
        m_scr[...] = jnp.zeros(m_scr.shape, F32)
        ext_scr[:, :, 0:CONV_TAIL, :] = jnp.zeros(ext_scr.shape[:2] + (CONV_TAIL, V7X_LANES), F32)

    nb = x_ref.shape[0]
    projections = [_dot(_rms(x_ref[bi], gain_ref[...]).astype(BF16), win_ref[...]) for bi in range(nb)]
    for bi in range(nb):
        _mlstm_tile(projections[bi], bi, conv_ref, bi_ref, bf_ref, hgain_ref, mk_ref, mv_ref,
                    mqgain_ref, havg_ref, ltri_ref, c_scr, m_scr, ext_scr, hcat_scr, ts=ts,
                    conv_width=conv_width)
        rows = slice(bi * ts, (bi + 1) * ts)
        o_ref[bi] = x_ref[bi] + _dot(hcat_scr[rows, :], wout_ref[...])


def _mlstm_tile(p, bi, conv_ref, bi_ref, bf_ref, hgain_ref, mk_ref, mv_ref, mqgain_ref, havg_ref, ltri_ref,
                c_scr, m_scr, ext_scr, hcat_scr, *, ts, conv_width):
    out_rows = slice(bi * ts, (bi + 1) * ts)
    cw = conv_ref[...]
    conv_slabs = []
    for sl in range(ext_scr.shape[1]):
        lanes = slice(sl * V7X_LANES, (sl + 1) * V7X_LANES)
        pre = p[:, lanes]
        ext_scr[bi, sl, CONV_TAIL:CONV_TAIL + ts, :] = pre
        y = cw[conv_width - 1:conv_width, lanes] * pre
        for back in range(1, conv_width):
            tap = conv_width - 1 - back
            y = y + cw[tap:tap + 1, lanes] * ext_scr[bi, sl, CONV_TAIL - back:CONV_TAIL - back + ts, :]
        ext_scr[bi, sl, 0:CONV_TAIL, :] = pre[ts - CONV_TAIL:ts, :]
        conv_slabs.append(_silu(y))
    qk = jnp.concatenate(conv_slabs, axis=-1)
    q = qk[:, :MLSTM_WIDTH].astype(BF16)
    k = qk[:, MLSTM_WIDTH:] * (MLSTM_HEAD_DIM ** -0.5)
    v = p[:, _V0:_O0].astype(BF16)
    o_gate = p[:, _O0:_QM0]
    gate_i = p[:, _GI0:_GF0] + bi_ref[...]
    log_f = _log_sigmoid(p[:, _GF0:A_IN_PAD_WIDTH] + bf_ref[...])

    L = ltri_ref.shape[0]
    row = lax.broadcasted_iota(jnp.int32, (L, L), 0)
    col = lax.broadcasted_iota(jnp.int32, (L, L), 1)
    causal = col <= row
    win_lane = lax.broadcasted_iota(jnp.int32, (1, HEAD_WIN), 1)
    ltri = ltri_ref[...]
    for c in range(ts // L):
        rows = slice(c * L, (c + 1) * L)
        cells = []
        b = sum(_dot(ltri, part) for part in _split_bf16(log_f[rows], 3))
        b_last = b[L - 1:L, :]
        r = gate_i[rows] - b
        a = b_last + r
        m0 = m_scr[bi]
        m_new = jnp.maximum(b_last + m0, jnp.max(a, axis=0, keepdims=True))
        w = jnp.exp(a - m_new)
        decay = jnp.exp(b_last + m0 - m_new)
        g = b + m0
        r_rows = r.T
        for hh in range(MLSTM_HEADS):
            win = slice(HEAD_WIN_START[hh], HEAD_WIN_START[hh] + HEAD_WIN)
            off, den_lane = HEAD_WIN_OFFSET[hh], HEAD_DEN_LANE[hh]
            in_head = (win_lane >= off) & (win_lane < off + MLSTM_HEAD_DIM)
            qh = jnp.where(in_head, q[rows, win], jnp.zeros((), BF16))
            kh = k[rows, win]
            ones_lane = jnp.where(win_lane == den_lane, 1.0, 0.0).astype(BF16)
            vh = jnp.where(in_head, v[rows, win], ones_lane)
            d_log = jnp.where(causal, b[:, hh:hh + 1] + r_rows[hh:hh + 1, :], NEG_INF)
            g_col = g[:, hh:hh + 1]
            m_t = jnp.maximum(g_col, jnp.max(d_log, axis=-1, keepdims=True))
            wmat = jnp.exp(d_log - m_t) * _dot_nt(qh, kh.astype(BF16))
            c0 = c_scr[bi, hh]
            num = _dot(wmat.astype(BF16), vh) + jnp.exp(g_col - m_t) * _dot(qh, c0.astype(BF16))
            den = num[:, den_lane:den_lane + 1]
            cell = num / jnp.maximum(jnp.abs(den), jnp.exp(-m_t))
            cell = jnp.where(in_head, cell, 0.0)
            ms = jnp.sum(cell * cell, axis=-1, keepdims=True) * (1.0 / MLSTM_HEAD_DIM)
            cells.append(cell * lax.rsqrt(ms + RMS_EPS))
            kw = (kh * w[:, hh:hh + 1]).astype(BF16)
            c_scr[bi, hh] = decay[:, hh:hh + 1] * c0 + _dot_tn(kw, vh)
        m_scr[bi] = m_new
        pieces = {}
        for hh, cell in enumerate(cells):
            for half in range(HEAD_WIN // V7X_LANES):
                tile_idx = HEAD_WIN_START[hh] // V7X_LANES + half
                piece = cell[:, half * V7X_LANES:(half + 1) * V7X_LANES]
                pieces[tile_idx] = piece if tile_idx not in pieces else pieces[tile_idx] + piece
        cell_all = jnp.concatenate([pieces[t] for t in range(MLSTM_WIDTH // V7X_LANES)], axis=-1)
        hcat_scr[bi * ts + c * L:bi * ts + (c + 1) * L, :MLSTM_WIDTH] = (
            jax.nn.sigmoid(o_gate[rows]) * cell_all * hgain_ref[...]).astype(BF16)

    qm = _head_rms(p[:, _QM0:_GI0], havg_ref[...], mqgain_ref[...] * Q_SCALE).astype(BF16)
    hcat_scr[out_rows, MLSTM_WIDTH:] = _memory_attention(qm, mk_ref[bi], mv_ref[bi]).astype(BF16)


def _pad_lanes(w):
    return jnp.pad(w, [(0, 0)] * (w.ndim - 1) + [(0, V7X_LANES - w.shape[-1])])


def _mlstm_layer(x, gain, w_in, conv_w, gate_bias, h_gain, w_out, mem_k, mem_v, mq_gain,
                 head_avg, ltri, *, ts):
    b, s, d = x.shape
    W, H = MLSTM_WIDTH, MLSTM_HEADS
    w_in_p = jnp.concatenate([w_in[:, :4 * W], w_in[:, 4 * W + 2 * H:], _pad_lanes(w_in[:, 4 * W:4 * W + H]),
                              _pad_lanes(w_in[:, 4 * W + H:4 * W + 2 * H])], axis=1).astype(BF16)
    w_out_p = w_out.astype(BF16)
    tile = pl.BlockSpec((b, ts, d), lambda j: (0, j, 0))
    return pl.pallas_call(
        functools.partial(_mlstm_body, ts=ts, conv_width=conv_w.shape[0]),
        grid=(s // ts,),
        in_specs=[tile, _resident((1, d)), _resident(w_in_p.shape), _resident(conv_w.shape),
                  _resident((1, V7X_LANES)), _resident((1, V7X_LANES)), _resident((1, W)),
                  _resident(w_out_p.shape), _resident(mem_k.shape), _resident(mem_v.shape),
                  _resident((1, MEM_WIDTH)), _resident(head_avg.shape), _resident(ltri.shape)],
        out_specs=tile,
        out_shape=jax.ShapeDtypeStruct((b, s, d), F32),
        scratch_shapes=[pltpu.VMEM((b, H, HEAD_WIN, HEAD_WIN), F32), pltpu.VMEM((b, 1, V7X_LANES), F32),
                        pltpu.VMEM((b, 2 * W // V7X_LANES, CONV_TAIL + ts, V7X_LANES), F32),
                        pltpu.VMEM((b * ts, W + MEM_WIDTH), BF16)],
        compiler_params=_compiler_params(1),
        name="mlstm_mixer",
    )(x, gain.reshape(1, d), w_in_p, conv_w, _pad_lanes(gate_bias[:H].reshape(1, H)),
      _pad_lanes(gate_bias[H:].reshape(1, H)), h_gain.reshape(1, W), w_out_p,
      mem_k, mem_v, jnp.tile(mq_gain, MEM_HEADS).reshape(1, MEM_WIDTH), head_avg, ltri)


def _proj_body(x_ref, gain_ref, w_ref, hgain_ref, havg_ref, *out_refs, normed_width):
    h = _rms(x_ref[0], gain_ref[...]).astype(BF16)
    y = _dot(h, w_ref[...])
    pieces = []
    for c in range(y.shape[1] // GROUP_WIDTH):
        cols = slice(c * GROUP_WIDTH, (c + 1) * GROUP_WIDTH)
        if c * GROUP_WIDTH < normed_width:
            pieces.append(_head_rms(y[:, cols], havg_ref[...], hgain_ref[:, cols]))
        else:
            pieces.append(y[:, cols])
    start = 0
    for ref in out_refs:
        if len(ref.shape) == 4:
            n = ref.shape[1] * V7X_LANES // GROUP_WIDTH
            for c in range(n):
                for half in range(GROUP_WIDTH // V7X_LANES):
                    lanes = slice(half * V7X_LANES, (half + 1) * V7X_LANES)
                    ref[0, c * (GROUP_WIDTH // V7X_LANES) + half] = pieces[start + c][:, lanes]
        else:
            n = ref.shape[2] // GROUP_WIDTH
            for c in range(n):
                ref[0, :, c * GROUP_WIDTH:(c + 1) * GROUP_WIDTH] = pieces[start + c].astype(BF16)
        start += n


def _proj(x, gain, w, head_gain, head_avg, outputs, normed_width, *, tm, name):
    b, s, d = x.shape
    width = w.shape[1]
    out_specs, out_shapes = [], []
    for ow, slab_major in outputs:
        if slab_major:
            out_specs.append(pl.BlockSpec((1, ow // V7X_LANES, tm, V7X_LANES), lambda i, j: (i, 0, j, 0)))
            out_shapes.append(jax.ShapeDtypeStruct((b, ow // V7X_LANES, s, V7X_LANES), F32))
        else:
            out_specs.append(pl.BlockSpec((1, tm, ow), lambda i, j: (i, j, 0)))
            out_shapes.append(jax.ShapeDtypeStruct((b, s, ow), BF16))
    return pl.pallas_call(
        functools.partial(_proj_body, normed_width=normed_width),
        grid=(b, s // tm),
        in_specs=[pl.BlockSpec((1, tm, d), lambda i, j: (i, j, 0)), _resident((1, d)), _resident((d, width)),
                  _resident((1, normed_width)), _resident(head_avg.shape)],
        out_specs=out_specs,
        out_shape=out_shapes,
        compiler_params=_compiler_params(2),
        name=name,
    )(x, gain.reshape(1, d), w.astype(BF16), head_gain.reshape(1, normed_width), head_avg)


SLABS = GROUP_WIDTH // V7X_LANES
N_OUTPUTS = 2


PRE_STRIDE = 4


def _dilated_body(q_ref, k_ref, v_ref, bias_ref, o_ref, l_ref, kcarry_scr, vcarry_scr, *plane_scr,
                  dilation, nblk):
    A = ATTN_BLOCK
    tile = dilation * nblk * A
    first = pl.program_id(1) == 0
    no_prev_limit = jnp.where(first, A, 0)
    bias = bias_ref[0]

    @pl.when(first)
    def _():
        kcarry_scr[...] = jnp.zeros(kcarry_scr.shape, BF16)
        vcarry_scr[...] = jnp.zeros(vcarry_scr.shape, BF16)

    pre = PRE_STRIDE if dilation > PRE_STRIDE else 1
    post = dilation // pre
    if pre > 1:
        in_planes, out_planes = plane_scr
        for t, ref in enumerate((q_ref, k_ref, v_ref)):
            for sl in range(SLABS):
                for a in range(pre):
                    in_planes[(t * SLABS + sl) * pre + a] = ref[0, sl, pl.ds(a, tile // pre, stride=pre), :]

    def plane_rows(r, i):
        return pl.ds(i * A * post + r // pre, A, stride=post)

    def rows_of(r, i):
        start = i * A * dilation + r
        return pl.ds(start, A, stride=dilation) if dilation > 1 else pl.ds(start, A)

    def load(t, ref, r, i):
        if pre > 1:
            parts = [in_planes[(t * SLABS + sl) * pre + r % pre, plane_rows(r, i), :] for sl in range(SLABS)]
        else:
            parts = [ref[0, sl, rows_of(r, i), :] for sl in range(SLABS)]
        return jnp.concatenate(parts, axis=-1).astype(BF16)

    def store(o, ref, r, i, value):
        for sl in range(SLABS):
            piece = value[:, sl * V7X_LANES:(sl + 1) * V7X_LANES]
            if pre > 1:
                out_planes[(o * SLABS + sl) * pre + r % pre, plane_rows(r, i), :] = piece
            else:
                ref[0, sl, rows_of(r, i), :] = piece

    for r in range(dilation):
        carry_rows = slice(r * A, (r + 1) * A)
        k_cur, v_cur = kcarry_scr[carry_rows, :], vcarry_scr[carry_rows, :]
        for i in range(nblk):
            k_prev, v_prev = k_cur, v_cur
            k_cur, v_cur = load(1, k_ref, r, i), load(2, v_ref, r, i)
            out, lse = _stacked_head_attention(
                load(0, q_ref, r, i), jnp.concatenate([k_prev, k_cur], axis=0),
                jnp.concatenate([v_prev, v_cur], axis=0), bias, no_prev_limit if i == 0 else None)
            store(0, o_ref, r, i, out)
            store(1, l_ref, r, i, lse)
        kcarry_scr[carry_rows, :] = k_cur
        vcarry_scr[carry_rows, :] = v_cur

    if pre > 1:
        for o, ref in enumerate((o_ref, l_ref)):
            for sl in range(SLABS):
                for a in range(pre):
                    ref[0, sl, pl.ds(a, tile // pre, stride=pre), :] = out_planes[(o * SLABS + sl) * pre + a]


def _dilated_attention(q, k, v, bias, group, dilation, *, tile):
    b, _, s, _ = q.shape
    span = ATTN_BLOCK * dilation
    nblk = tile // span
    cur = pl.BlockSpec((1, SLABS, tile, V7X_LANES), lambda bi, n: (bi, group, n, 0))
    out_spec = pl.BlockSpec((1, SLABS, tile, V7X_LANES), lambda bi, n: (bi, 0, n, 0))
    out_shape = jax.ShapeDtypeStruct((b, SLABS, s, V7X_LANES), F32)
    carry = pltpu.VMEM((span, GROUP_WIDTH), BF16)
    scratch = [carry, carry]
    if dilation > PRE_STRIDE:
        plane = lambda n_tensors: pltpu.VMEM((n_tensors * SLABS * PRE_STRIDE, tile // PRE_STRIDE, V7X_LANES), F32)
        scratch += [plane(3), plane(N_OUTPUTS)]
    return pl.pallas_call(
        functools.partial(_dilated_body, dilation=dilation, nblk=nblk),
        grid=(b, s // tile),
        in_specs=[cur, cur, cur,
                  pl.BlockSpec((1,) + bias.shape[1:], lambda bi, n: (group, 0, 0), pipeline_mode=pl.Buffered(1))],
        out_specs=[out_spec] * N_OUTPUTS,
        out_shape=[out_shape] * N_OUTPUTS,
        scratch_shapes=scratch,
        compiler_params=_compiler_params(2),
        name=f"dilated_attn_g{group}",
    )(q, k, v, bias)


def _bias_body(table_ref, bucket_ref, o_ref):
    g = pl.program_id(0)
    bucket = bucket_ref[0]
    for hh in range(HEADS_PER_GROUP):
        acc = jnp.full(bucket.shape, NEG_INF, F32)
        for bkt in range(NUM_BUCKETS):
            acc = jnp.where(bucket == bkt, table_ref[g * HEADS_PER_GROUP + hh, bkt] * LOG2E, acc)
        o_ref[0, hh * ATTN_BLOCK:(hh + 1) * ATTN_BLOCK, :] = acc


def _dilated_bias(rel_bias):
    A = ATTN_BLOCK
    m_off = np.arange(A)[:, None] + A - np.arange(2 * A)[None, :]
    band = (m_off >= 0) & (m_off <= A)
    max_exact = NUM_BUCKETS // 2
    buckets = []
    for _, dilation in DILATED_GROUPS:
        dist = np.clip(m_off, 0, A) * dilation
        large = max_exact + (np.log(np.maximum(dist, 1).astype(np.float32) / max_exact)
                             / math.log(MAX_DISTANCE / max_exact) * (NUM_BUCKETS - max_exact)).astype(np.int32)
        bucket = np.where(dist < max_exact, dist, np.minimum(large, NUM_BUCKETS - 1))
        buckets.append(np.where(band, bucket, -1))
    buckets = jnp.asarray(np.stack(buckets), jnp.int32)
    n_groups = len(DILATED_GROUPS)
    return pl.pallas_call(
        _bias_body,
        grid=(n_groups,),
        in_specs=[pl.BlockSpec(memory_space=pltpu.SMEM), pl.BlockSpec((1, A, 2 * A), lambda g: (g, 0, 0))],
        out_specs=pl.BlockSpec((1, HEADS_PER_GROUP * A, 2 * A), lambda g: (g, 0, 0)),
        out_shape=jax.ShapeDtypeStruct((n_groups, HEADS_PER_GROUP * A, 2 * A), F32),
        compiler_params=_compiler_params(1),
        name="dilated_bias",
    )(rel_bias.astype(F32).T, buckets)


def _mixout_body(x_ref, o0_ref, o1_ref, o2_ref, l0_ref, l1_ref, l2_ref, qm_ref, mk_ref, mv_ref, wout_ref,
                 gain_ref, wg_hbm, wu_hbm, wd_hbm, out_ref, hcat_scr, *ffn_scratch, layer):
    wide = lambda ref: jnp.concatenate([ref[0, sl] for sl in range(SLABS)], axis=-1)
    lses = [wide(l0_ref), wide(l1_ref), wide(l2_ref)]
    mx = jnp.maximum(jnp.maximum(lses[0], lses[1]), lses[2])
    es = [jnp.exp(l - mx) for l in lses]
    inv = 1.0 / (es[0] + es[1] + es[2])
    for g, o_ref in enumerate((o0_ref, o1_ref, o2_ref)):
        hcat_scr[:, g * GROUP_WIDTH:(g + 1) * GROUP_WIDTH] = (wide(o_ref) * (es[g] * inv)).astype(BF16)
    hcat_scr[:, DIL_WIDTH:] = _memory_attention(qm_ref[0], mk_ref[0], mv_ref[0]).astype(BF16)
    mixed = x_ref[0] + _dot(hcat_scr[...], wout_ref[...])

    def write(value):
        out_ref[0] = value
    first = (pl.program_id(0) == 0) & (pl.program_id(1) == 0)
    _ffn_first_or_later(first, write, mixed, gain_ref, (wg_hbm, wu_hbm, wd_hbm) + ffn_scratch, layer=layer)


def _mixout_ffn(x, outs, lses, q_mem, mem_k, mem_v, w_out, ffn_gain, w_gate, w_up, w_down, layer, *, ts):
    b, s, d = x.shape
    f = w_gate.shape[2]
    tile = lambda w: pl.BlockSpec((1, ts, w), lambda i, j: (i, j, 0))
    slab_tile = pl.BlockSpec((1, SLABS, ts, V7X_LANES), lambda i, j: (i, 0, j, 0))
    mem_spec = pl.BlockSpec((1,) + mem_k.shape[1:], lambda i, j: (i, 0, 0))
    return pl.pallas_call(
        functools.partial(_mixout_body, layer=layer),
        grid=(b, s // ts),
        in_specs=([tile(d)] + [slab_tile] * 6 + [tile(MEM_WIDTH), mem_spec, mem_spec, _resident(w_out.shape)]
                  + _ffn_weight_specs(d)),
        out_specs=tile(d),
        out_shape=jax.ShapeDtypeStruct((b, s, d), F32),
        scratch_shapes=[pltpu.VMEM((ts, DIL_WIDTH + MEM_WIDTH), BF16)] + _ffn_scratch(ts, d, f),
        compiler_params=_compiler_params(2),
        name="dilated_mixer_out_ffn",
    )(x, *outs, *lses, q_mem, mem_k, mem_v, w_out.astype(BF16), ffn_gain.reshape(1, d), w_gate, w_up, w_down)


def kernel(x, mem, ffn1_norm, ffn1_w_gate, ffn1_w_up, ffn1_w_down, ffn2_norm, ffn2_w_gate, ffn2_w_up, ffn2_w_down, mix_norm, mem_norm, w_mem_kv, mem_q_norm, mem_k_norm, a_w_in, a_conv, a_gate_bias, a_h_norm, a_w_out, b_w_q, b_q_norm, b_w_out, kv_norm, w_kv, kv_k_norm, rel_bias):
    b, s, d = x.shape
    n = b * s
    tm = 1024
    head_id = np.arange(GROUP_WIDTH) // HEAD_DIM
    head_avg = jnp.asarray((head_id[:, None] == head_id[None, :]) / HEAD_DIM, BF16)
    mlstm_chunk = V7X_MXU_DIM
    ltri = jnp.asarray(np.tril(np.ones((mlstm_chunk, mlstm_chunk))), BF16)
    flat = lambda t: t.reshape(n, t.shape[-1])
    ffn = lambda t, norm, wg, wu, wd, layer: _ffn(flat(t), norm[layer], wg, wu, wd, layer,
                                                   tm=tm).reshape(b, s, d)

    x = ffn(x, ffn1_norm, ffn1_w_gate, ffn1_w_up, ffn1_w_down, 0)
    mem_k, mem_v = _memkv(mem, mem_norm[0], w_mem_kv[0], mem_k_norm[0], head_avg)
    x = _mlstm_layer(x, mix_norm[0], a_w_in[0], a_conv[0], a_gate_bias[0], a_h_norm[0], a_w_out[0],
                     mem_k, mem_v, mem_q_norm[0], head_avg, ltri, ts=mlstm_chunk)
    x = ffn(x, ffn2_norm, ffn2_w_gate, ffn2_w_up, ffn2_w_down, 0)

    k_sh, v_sh = _proj(x, kv_norm, w_kv, jnp.tile(kv_k_norm, DIL_WIDTH // HEAD_DIM), head_avg,
                       [(DIL_WIDTH, True), (DIL_WIDTH, True)], DIL_WIDTH, tm=tm, name="shared_kv_proj")

    x = ffn(x, ffn1_norm, ffn1_w_gate, ffn1_w_up, ffn1_w_down, 1)
    q_gain = jnp.concatenate([jnp.tile(b_q_norm[0], DIL_WIDTH // HEAD_DIM),
                              jnp.tile(mem_q_norm[1], MEM_HEADS)]) * Q_SCALE
    q_d, q_mem = _proj(x, mix_norm[1], b_w_q[0], q_gain, head_avg, [(DIL_WIDTH, True), (MEM_WIDTH, False)],
                       DIL_WIDTH + MEM_WIDTH, tm=tm, name="dilated_q_proj")
    mem_k, mem_v = _memkv(mem, mem_norm[1], w_mem_kv[1], mem_k_norm[1], head_avg)
    bias = _dilated_bias(rel_bias)
    attn_tile = max(window for window, _ in DILATED_GROUPS)
    outs, lses = [], []
    for g, (window, dilation) in enumerate(DILATED_GROUPS):
        assert window // dilation == ATTN_BLOCK and s % attn_tile == 0 and attn_tile % window == 0
        o_g, l_g = _dilated_attention(q_d, k_sh, v_sh, bias, g, dilation, tile=attn_tile)
        outs.append(o_g)
        lses.append(l_g)
    return _mixout_ffn(x, outs, lses, q_mem, mem_k, mem_v, b_w_out[0], ffn2_norm[1], ffn2_w_gate, ffn2_w_up,
                       ffn2_w_down, 1, ts=512)
```

```python
import functools
import math

import numpy as np
import jax
import jax.numpy as jnp
from jax import lax
from jax.experimental import pallas as pl
from jax.experimental.pallas import tpu as pltpu

F32 = jnp.float32
BF16 = jnp.bfloat16

HEAD_DIM = 64
MEM_HEADS = 4
MEM_WIDTH = MEM_HEADS * HEAD_DIM
MLSTM_HEADS = 4
MLSTM_HEAD_DIM = 192
MLSTM_WIDTH = MLSTM_HEADS * MLSTM_HEAD_DIM
DILATED_GROUPS = ((128, 1), (512, 4), (2048, 16))
HEADS_PER_GROUP = 4
GROUP_WIDTH = HEADS_PER_GROUP * HEAD_DIM
DIL_WIDTH = GROUP_WIDTH * len(DILATED_GROUPS)
NUM_BUCKETS = 32
MAX_DISTANCE = 2048
RMS_EPS = 1e-6
NEG_INF = -1e30
ATTN_SCALE = HEAD_DIM ** -0.5
LOG2E = math.log2(math.e)
LN2 = math.log(2.0)
Q_SCALE = ATTN_SCALE * LOG2E

V7X_LANES = 128
V7X_SUBLANES = 8
V7X_MXU_DIM = 256
V7X_VMEM_BYTES = 64 * 1024 * 1024
VMEM_LIMIT_BYTES = V7X_VMEM_BYTES - 8 * 1024 * 1024

HEAD_WIN = V7X_MXU_DIM
HEAD_WIN_START = tuple((h * MLSTM_HEAD_DIM) // V7X_LANES * V7X_LANES for h in range(MLSTM_HEADS))
HEAD_WIN_OFFSET = tuple(h * MLSTM_HEAD_DIM - s for h, s in enumerate(HEAD_WIN_START))
assert all(o + MLSTM_HEAD_DIM <= HEAD_WIN for o in HEAD_WIN_OFFSET) and MLSTM_HEAD_DIM < HEAD_WIN
HEAD_DEN_LANE = tuple((o + MLSTM_HEAD_DIM) % HEAD_WIN for o in HEAD_WIN_OFFSET)
ATTN_BLOCK = 128


def _compiler_params(n_axes):
    return pltpu.CompilerParams(dimension_semantics=("arbitrary",) * n_axes,
                                vmem_limit_bytes=VMEM_LIMIT_BYTES)


def _resident(shape):
    zeros = (0,) * len(shape)
    return pl.BlockSpec(shape, lambda *_: zeros, pipeline_mode=pl.Buffered(1))


def _dot(a, b):
    return jnp.dot(a, b, preferred_element_type=F32)


def _dot_nt(a, b):
    return lax.dot_general(a, b, (((1,), (1,)), ((), ())), preferred_element_type=F32)


def _dot_tn(a, b):
    return lax.dot_general(a, b, (((0,), (0,)), ((), ())), preferred_element_type=F32)


def _rms(x, gain):
    ms = jnp.mean(x * x, axis=-1, keepdims=True)
    return x * lax.rsqrt(ms + RMS_EPS) * gain


def _split_bf16(x, terms):
    parts = []
    rest = x
    for _ in range(terms):
        part = rest.astype(BF16)
        parts.append(part)
        rest = rest - part.astype(F32)
    return parts


def _head_mean_sq(x, head_avg):
    return _dot((x * x).astype(BF16), head_avg)


def _head_rms(x, head_avg, gain):
    return x * lax.rsqrt(_head_mean_sq(x, head_avg) + RMS_EPS) * gain


def _silu(x):
    return x * jax.nn.sigmoid(x)


def _log_sigmoid(x):
    return jnp.minimum(x, 0.0) - jnp.log(1.0 + jnp.exp(-jnp.abs(x)))


def _head_lane_mask(head, width=GROUP_WIDTH):
    lane = lax.broadcasted_iota(jnp.int32, (1, width), 1)
    return (lane >= head * HEAD_DIM) & (lane < (head + 1) * HEAD_DIM)


def _memory_attention(qn, mem_k, mem_v):
    out, _ = _stacked_head_attention(qn, mem_k, mem_v, None, None)
    return out


def _stacked_head_attention(q, keys, values, bias, masked_below):
    t = q.shape[0]
    masks = [_head_lane_mask(h) for h in range(HEADS_PER_GROUP)]
    qs = jnp.concatenate([jnp.where(m, q, jnp.zeros_like(q)) for m in masks], axis=0)
    s = _dot_nt(qs, keys)
    if bias is not None:
        s = s + bias
    if masked_below is not None:
        key = lax.broadcasted_iota(jnp.int32, (1, keys.shape[0]), 1)
        s = jnp.where(key < masked_below, NEG_INF, s)
    mx = jnp.max(s, axis=-1, keepdims=True)
    e = jnp.exp2(s - mx)
    den = jnp.sum(e, axis=-1, keepdims=True)
    pv = _dot(e.astype(BF16), values)
    inv = 1.0 / den
    l_all = mx * LN2 + jnp.log(den)
    first_head = lax.broadcasted_iota(jnp.int32, (1, V7X_LANES), 1) < HEAD_DIM
    out_slabs, lse_slabs = [], []
    for sl in range(q.shape[1] // V7X_LANES):
        lanes = slice(sl * V7X_LANES, (sl + 1) * V7X_LANES)
        ra, rb = slice(2 * sl * t, (2 * sl + 1) * t), slice((2 * sl + 1) * t, (2 * sl + 2) * t)
        out_slabs.append(jnp.where(first_head, pv[ra, lanes] * inv[ra], pv[rb, lanes] * inv[rb]))
        lse_slabs.append(jnp.where(first_head, l_all[ra], l_all[rb]))
    return jnp.concatenate(out_slabs, axis=-1), jnp.concatenate(lse_slabs, axis=-1)


FFN_CHUNK = V7X_MXU_DIM


def _ffn_halfstep(x, gain_ref, wg_ref, wu_ref, wd_ref, h_scr, acc_scr):
    h_scr[...] = _rms(x, gain_ref[...]).astype(BF16)
    for c in range(wg_ref.shape[1] // FFN_CHUNK):
        cols = slice(c * FFN_CHUNK, (c + 1) * FFN_CHUNK)
        h = h_scr[...]
        act = (_silu(_dot(h, wg_ref[:, cols])) * _dot(h, wu_ref[:, cols])).astype(BF16)
        down = _dot(act, wd_ref[cols, :])
        if c == 0:
            acc_scr[...] = down
        else:
            acc_scr[...] += down
    return x + 0.5 * acc_scr[...]


def _ffn_body(x_ref, gain_ref, wg_ref, wu_ref, wd_ref, o_ref, h_scr, acc_scr):
    o_ref[...] = _ffn_halfstep(x_ref[...], gain_ref, wg_ref, wu_ref, wd_ref, h_scr, acc_scr)


def _ffn_weight_specs(d, f, layer):
    index = lambda *_: (layer, 0, 0)
    spec = lambda rows, cols: pl.BlockSpec((None, rows, cols), index, pipeline_mode=pl.Buffered(1))
    return [_resident((1, d)), spec(d, f), spec(d, f), spec(f, d)]


def _ffn(x2d, gain, w_gate, w_up, w_down, layer, *, tm):
    n, d = x2d.shape
    f = w_gate.shape[2]
    row_spec = pl.BlockSpec((tm, d), lambda i: (i, 0))
    return pl.pallas_call(
        _ffn_body,
        grid=(n // tm,),
        in_specs=[row_spec] + _ffn_weight_specs(d, f, layer),
        out_specs=row_spec,
        out_shape=jax.ShapeDtypeStruct((n, d), F32),
        scratch_shapes=[pltpu.VMEM((tm, d), BF16), pltpu.VMEM((tm, d), F32)],
        compiler_params=_compiler_params(1),
        name="ffn",
    )(x2d, gain.reshape(1, d), w_gate, w_up, w_down)


def _memkv_body(mem_ref, gain_ref, w_ref, kgain_ref, havg_ref, k_ref, v_ref):
    mem_n = _rms(mem_ref[0], gain_ref[...]).astype(BF16)
    kv = _dot(mem_n, w_ref[...])
    k_ref[0] = _head_rms(kv[:, :MEM_WIDTH], havg_ref[...], kgain_ref[...]).astype(BF16)
    v_ref[0] = kv[:, MEM_WIDTH:].astype(BF16)


def _memkv(mem, gain, w_mem_kv, k_gain, head_avg):
    b, m, d = mem.shape
    out_spec = pl.BlockSpec((1, m, MEM_WIDTH), lambda i: (i, 0, 0))
    return pl.pallas_call(
        _memkv_body,
        grid=(b,),
        in_specs=[pl.BlockSpec((1, m, d), lambda i: (i, 0, 0)), _resident((1, d)),
                  _resident((d, 2 * MEM_WIDTH)), _resident((1, MEM_WIDTH)),
                  _resident((MEM_WIDTH, MEM_WIDTH))],
        out_specs=[out_spec, out_spec],
        out_shape=[jax.ShapeDtypeStruct((b, m, MEM_WIDTH), BF16)] * 2,
        compiler_params=_compiler_params(1),
        name="memkv",
    )(mem, gain.reshape(1, d), w_mem_kv.astype(BF16), jnp.tile(k_gain, MEM_HEADS).reshape(1, MEM_WIDTH),
      head_avg)


_Q0, _K0, _V0, _O0 = (i * MLSTM_WIDTH for i in range(4))
_QM0 = 4 * MLSTM_WIDTH
_GI0 = _QM0 + MEM_WIDTH
_GF0 = _GI0 + V7X_LANES
A_IN_PAD_WIDTH = _GF0 + V7X_LANES
CONV_TAIL = V7X_SUBLANES


def _mlstm_body(x_ref, gain_ref, win_ref, conv_ref, bi_ref, bf_ref, hgain_ref, wout_ref,
                mk_ref, mv_ref, mqgain_ref, havg_ref, ltri_ref,
                o_ref, c_scr, m_scr, ext_scr, hcat_scr, *, ts, conv_width):
    @pl.when(pl.program_id(0) == 0)
    def _():
        c_scr[...] = jnp.zeros(c_scr.shape, F32)
        m_scr[...] = jnp.zeros(m_scr.shape, F32)
        ext_scr[:, :, 0:CONV_TAIL, :] = jnp.zeros(ext_scr.shape[:2] + (CONV_TAIL, V7X_LANES), F32)

    nb = x_ref.shape[0]
    projections = [_dot(_rms(x_ref[bi], gain_ref[...]).astype(BF16), win_ref[...]) for bi in range(nb)]
    for bi in range(nb):
        _mlstm_tile(projections[bi], bi, conv_ref, bi_ref, bf_ref, hgain_ref, mk_ref, mv_ref,
                    mqgain_ref, havg_ref, ltri_ref, c_scr, m_scr, ext_scr, hcat_scr, ts=ts,
                    conv_width=conv_width)
        rows = slice(bi * ts, (bi + 1) * ts)
        o_ref[bi] = x_ref[bi] + _dot(hcat_scr[rows, :], wout_ref[...])


def _mlstm_tile(p, bi, conv_ref, bi_ref, bf_ref, hgain_ref, mk_ref, mv_ref, mqgain_ref, havg_ref, ltri_ref,
                c_scr, m_scr, ext_scr, hcat_scr, *, ts, conv_width):
    out_rows = slice(bi * ts, (bi + 1) * ts)
    cw = conv_ref[...]
    conv_slabs = []
    for sl in range(ext_scr.shape[1]):
        lanes = slice(sl * V7X_LANES, (sl + 1) * V7X_LANES)
        pre = p[:, lanes]
        ext_scr[bi, sl, CONV_TAIL:CONV_TAIL + ts, :] = pre
        y = cw[conv_width - 1:conv_width, lanes] * pre
        for back in range(1, conv_width):
            tap = conv_width - 1 - back
            y = y + cw[tap:tap + 1, lanes] * ext_scr[bi, sl, CONV_TAIL - back:CONV_TAIL - back + ts, :]
        ext_scr[bi, sl, 0:CONV_TAIL, :] = pre[ts - CONV_TAIL:ts, :]
        conv_slabs.append(_silu(y))
    qk = jnp.concatenate(conv_slabs, axis=-1)
    q = qk[:, :MLSTM_WIDTH].astype(BF16)
    k = qk[:, MLSTM_WIDTH:] * (MLSTM_HEAD_DIM ** -0.5)
    v = p[:, _V0:_O0].astype(BF16)
    o_gate = p[:, _O0:_QM0]
    gate_i = p[:, _GI0:_GF0] + bi_ref[...]
    log_f = _log_sigmoid(p[:, _GF0:A_IN_PAD_WIDTH] + bf_ref[...])

    L = ltri_ref.shape[0]
    row = lax.broadcasted_iota(jnp.int32, (L, L), 0)
    col = lax.broadcasted_iota(jnp.int32, (L, L), 1)
    causal = col <= row
    win_lane = lax.broadcasted_iota(jnp.int32, (1, HEAD_WIN), 1)
    ltri = ltri_ref[...]
    for c in range(ts // L):
        rows = slice(c * L, (c + 1) * L)
        cells = []
        b = sum(_dot(ltri, part) for part in _split_bf16(log_f[rows], 3))
        b_last = b[L - 1:L, :]
        r = gate_i[rows] - b
        a = b_last + r
        m0 = m_scr[bi]
        m_new = jnp.maximum(b_last + m0, jnp.max(a, axis=0, keepdims=True))
        w = jnp.exp(a - m_new)
        decay = jnp.exp(b_last + m0 - m_new)
        g = b + m0
        r_rows = r.T
        for hh in range(MLSTM_HEADS):
            win = slice(HEAD_WIN_START[hh], HEAD_WIN_START[hh] + HEAD_WIN)
            off, den_lane = HEAD_WIN_OFFSET[hh], HEAD_DEN_LANE[hh]
            in_head = (win_lane >= off) & (win_lane < off + MLSTM_HEAD_DIM)
            qh = jnp.where(in_head, q[rows, win], jnp.zeros((), BF16))
            kh = k[rows, win]
            ones_lane = jnp.where(win_lane == den_lane, 1.0, 0.0).astype(BF16)
            vh = jnp.where(in_head, v[rows, win], ones_lane)
            d_log = jnp.where(causal, b[:, hh:hh + 1] + r_rows[hh:hh + 1, :], NEG_INF)
            g_col = g[:, hh:hh + 1]
            m_t = jnp.maximum(g_col, jnp.max(d_log, axis=-1, keepdims=True))
            wmat = jnp.exp(d_log - m_t) * _dot_nt(qh, kh.astype(BF16))
            c0 = c_scr[bi, hh]
            num = _dot(wmat.astype(BF16), vh) + jnp.exp(g_col - m_t) * _dot(qh, c0.astype(BF16))
            den = num[:, den_lane:den_lane + 1]
            cell = num / jnp.maximum(jnp.abs(den), jnp.exp(-m_t))
            cell = jnp.where(in_head, cell, 0.0)
            ms = jnp.sum(cell * cell, axis=-1, keepdims=True) * (1.0 / MLSTM_HEAD_DIM)
            cells.append(cell * lax.rsqrt(ms + RMS_EPS))
            kw = (kh * w[:, hh:hh + 1]).astype(BF16)
            c_scr[bi, hh] = decay[:, hh:hh + 1] * c0 + _dot_tn(kw, vh)
        m_scr[bi] = m_new
        pieces = {}
        for hh, cell in enumerate(cells):
            for half in range(HEAD_WIN // V7X_LANES):
                tile_idx = HEAD_WIN_START[hh] // V7X_LANES + half
                piece = cell[:, half * V7X_LANES:(half + 1) * V7X_LANES]
                pieces[tile_idx] = piece if tile_idx not in pieces else pieces[tile_idx] + piece
        cell_all = jnp.concatenate([pieces[t] for t in range(MLSTM_WIDTH // V7X_LANES)], axis=-1)
        hcat_scr[bi * ts + c * L:bi * ts + (c + 1) * L, :MLSTM_WIDTH] = (
            jax.nn.sigmoid(o_gate[rows]) * cell_all * hgain_ref[...]).astype(BF16)

    qm = _head_rms(p[:, _QM0:_GI0], havg_ref[...], mqgain_ref[...] * Q_SCALE).astype(BF16)
    hcat_scr[out_rows, MLSTM_WIDTH:] = _memory_attention(qm, mk_ref[bi], mv_ref[bi]).astype(BF16)


def _pad_lanes(w):
    return jnp.pad(w, [(0, 0)] * (w.ndim - 1) + [(0, V7X_LANES - w.shape[-1])])


def _mlstm_layer(x, gain, w_in, conv_w, gate_bias, h_gain, w_out, mem_k, mem_v, mq_gain,
                 head_avg, ltri, *, ts):
    b, s, d = x.shape
    W, H = MLSTM_WIDTH, MLSTM_HEADS
    w_in_p = jnp.concatenate([w_in[:, :4 * W], w_in[:, 4 * W + 2 * H:], _pad_lanes(w_in[:, 4 * W:4 * W + H]),
                              _pad_lanes(w_in[:, 4 * W + H:4 * W + 2 * H])], axis=1).astype(BF16)
    w_out_p = w_out.astype(BF16)
    tile = pl.BlockSpec((b, ts, d), lambda j: (0, j, 0))
    return pl.pallas_call(
        functools.partial(_mlstm_body, ts=ts, conv_width=conv_w.shape[0]),
        grid=(s // ts,),
        in_specs=[tile, _resident((1, d)), _resident(w_in_p.shape), _resident(conv_w.shape),
                  _resident((1, V7X_LANES)), _resident((1, V7X_LANES)), _resident((1, W)),
                  _resident(w_out_p.shape), _resident(mem_k.shape), _resident(mem_v.shape),
                  _resident((1, MEM_WIDTH)), _resident(head_avg.shape), _resident(ltri.shape)],
        out_specs=tile,
        out_shape=jax.ShapeDtypeStruct((b, s, d), F32),
        scratch_shapes=[pltpu.VMEM((b, H, HEAD_WIN, HEAD_WIN), F32), pltpu.VMEM((b, 1, V7X_LANES), F32),
                        pltpu.VMEM((b, 2 * W // V7X_LANES, CONV_TAIL + ts, V7X_LANES), F32),
                        pltpu.VMEM((b * ts, W + MEM_WIDTH), BF16)],
        compiler_params=_compiler_params(1),
        name="mlstm_mixer",
    )(x, gain.reshape(1, d), w_in_p, conv_w, _pad_lanes(gate_bias[:H].reshape(1, H)),
      _pad_lanes(gate_bias[H:].reshape(1, H)), h_gain.reshape(1, W), w_out_p,
      mem_k, mem_v, jnp.tile(mq_gain, MEM_HEADS).reshape(1, MEM_WIDTH), head_avg, ltri)


def _proj_body(x_ref, gain_ref, w_ref, hgain_ref, havg_ref, *out_refs, normed_width):
    h = _rms(x_ref[0], gain_ref[...]).astype(BF16)
    y = _dot(h, w_ref[...])
    pieces = []
    for c in range(y.shape[1] // GROUP_WIDTH):
        cols = slice(c * GROUP_WIDTH, (c + 1) * GROUP_WIDTH)
        if c * GROUP_WIDTH < normed_width:
            pieces.append(_head_rms(y[:, cols], havg_ref[...], hgain_ref[:, cols]))
        else:
            pieces.append(y[:, cols])
    start = 0
    for ref in out_refs:
        if len(ref.shape) == 4:
            n = ref.shape[1] * V7X_LANES // GROUP_WIDTH
            for c in range(n):
                for half in range(GROUP_WIDTH // V7X_LANES):
                    lanes = slice(half * V7X_LANES, (half + 1) * V7X_LANES)
                    ref[0, c * (GROUP_WIDTH // V7X_LANES) + half] = pieces[start + c][:, lanes]
        else:
            n = ref.shape[2] // GROUP_WIDTH
            for c in range(n):
                ref[0, :, c * GROUP_WIDTH:(c + 1) * GROUP_WIDTH] = pieces[start + c].astype(BF16)
        start += n


def _proj(x, gain, w, head_gain, head_avg, outputs, normed_width, *, tm, name):
    b, s, d = x.shape
    width = w.shape[1]
    out_specs, out_shapes = [], []
    for ow, slab_major in outputs:
        if slab_major:
            out_specs.append(pl.BlockSpec((1, ow // V7X_LANES, tm, V7X_LANES), lambda i, j: (i, 0, j, 0)))
            out_shapes.append(jax.ShapeDtypeStruct((b, ow // V7X_LANES, s, V7X_LANES), F32))
        else:
            out_specs.append(pl.BlockSpec((1, tm, ow), lambda i, j: (i, j, 0)))
            out_shapes.append(jax.ShapeDtypeStruct((b, s, ow), BF16))
    return pl.pallas_call(
        functools.partial(_proj_body, normed_width=normed_width),
        grid=(b, s // tm),
        in_specs=[pl.BlockSpec((1, tm, d), lambda i, j: (i, j, 0)), _resident((1, d)), _resident((d, width)),
                  _resident((1, normed_width)), _resident(head_avg.shape)],
        out_specs=out_specs,
        out_shape=out_shapes,
        compiler_params=_compiler_params(2),
        name=name,
    )(x, gain.reshape(1, d), w.astype(BF16), head_gain.reshape(1, normed_width), head_avg)


SLABS = GROUP_WIDTH // V7X_LANES
N_OUTPUTS = 2


PRE_STRIDE = 4


def _dilated_body(q_ref, k_ref, v_ref, bias_ref, o_ref, l_ref, kcarry_scr, vcarry_scr, *plane_scr,
                  dilation, nblk):
    A = ATTN_BLOCK
    tile = dilation * nblk * A
    first = pl.program_id(1) == 0
    no_prev_limit = jnp.where(first, A, 0)
    bias = bias_ref[0]

    @pl.when(first)
    def _():
        kcarry_scr[...] = jnp.zeros(kcarry_scr.shape, BF16)
        vcarry_scr[...] = jnp.zeros(vcarry_scr.shape, BF16)

    pre = PRE_STRIDE if dilation > PRE_STRIDE else 1
    post = dilation // pre
    if pre > 1:
        in_planes, out_planes = plane_scr
        for t, ref in enumerate((q_ref, k_ref, v_ref)):
            for sl in range(SLABS):
                for a in range(pre):
                    in_planes[(t * SLABS + sl) * pre + a] = ref[0, sl, pl.ds(a, tile // pre, stride=pre), :]

    def plane_rows(r, i):
        return pl.ds(i * A * post + r // pre, A, stride=post)

    def rows_of(r, i):
        start = i * A * dilation + r
        return pl.ds(start, A, stride=dilation) if dilation > 1 else pl.ds(start, A)

    def load(t, ref, r, i):
        if pre > 1:
            parts = [in_planes[(t * SLABS + sl) * pre + r % pre, plane_rows(r, i), :] for sl in range(SLABS)]
        else:
            parts = [ref[0, sl, rows_of(r, i), :] for sl in range(SLABS)]
        return jnp.concatenate(parts, axis=-1).astype(BF16)

    def store(o, ref, r, i, value):
        for sl in range(SLABS):
            piece = value[:, sl * V7X_LANES:(sl + 1) * V7X_LANES]
            if pre > 1:
                out_planes[(o * SLABS + sl) * pre + r % pre, plane_rows(r, i), :] = piece
            else:
                ref[0, sl, rows_of(r, i), :] = piece

    for r in range(dilation):
        carry_rows = slice(r * A, (r + 1) * A)
        k_cur, v_cur = kcarry_scr[carry_rows, :], vcarry_scr[carry_rows, :]
        for i in range(nblk):
            k_prev, v_prev = k_cur, v_cur
            k_cur, v_cur = load(1, k_ref, r, i), load(2, v_ref, r, i)
            out, lse = _stacked_head_attention(
                load(0, q_ref, r, i), jnp.concatenate([k_prev, k_cur], axis=0),
                jnp.concatenate([v_prev, v_cur], axis=0), bias, no_prev_limit if i == 0 else None)
            store(0, o_ref, r, i, out)
            store(1, l_ref, r, i, lse)
        kcarry_scr[carry_rows, :] = k_cur
        vcarry_scr[carry_rows, :] = v_cur

    if pre > 1:
        for o, ref in enumerate((o_ref, l_ref)):
            for sl in range(SLABS):
                for a in range(pre):
                    ref[0, sl, pl.ds(a, tile // pre, stride=pre), :] = out_planes[(o * SLABS + sl) * pre + a]


def _dilated_attention(q, k, v, bias, group, dilation, *, tile):
    b, _, s, _ = q.shape
    span = ATTN_BLOCK * dilation
    nblk = tile // span
    cur = pl.BlockSpec((1, SLABS, tile, V7X_LANES), lambda bi, n: (bi, group, n, 0))
    out_spec = pl.BlockSpec((1, SLABS, tile, V7X_LANES), lambda bi, n: (bi, 0, n, 0))
    out_shape = jax.ShapeDtypeStruct((b, SLABS, s, V7X_LANES), F32)
    carry = pltpu.VMEM((span, GROUP_WIDTH), BF16)
    scratch = [carry, carry]
    if dilation > PRE_STRIDE:
        plane = lambda n_tensors: pltpu.VMEM((n_tensors * SLABS * PRE_STRIDE, tile // PRE_STRIDE, V7X_LANES), F32)
        scratch += [plane(3), plane(N_OUTPUTS)]
    return pl.pallas_call(
        functools.partial(_dilated_body, dilation=dilation, nblk=nblk),
        grid=(b, s // tile),
        in_specs=[cur, cur, cur,
                  pl.BlockSpec((1,) + bias.shape[1:], lambda bi, n: (group, 0, 0), pipeline_mode=pl.Buffered(1))],
        out_specs=[out_spec] * N_OUTPUTS,
        out_shape=[out_shape] * N_OUTPUTS,
        scratch_shapes=scratch,
        compiler_params=_compiler_params(2),
        name=f"dilated_attn_g{group}",
    )(q, k, v, bias)


def _bias_body(table_ref, bucket_ref, o_ref):
    g = pl.program_id(0)
    bucket = bucket_ref[0]
    for hh in range(HEADS_PER_GROUP):
        acc = jnp.full(bucket.shape, NEG_INF, F32)
        for bkt in range(NUM_BUCKETS):
            acc = jnp.where(bucket == bkt, table_ref[g * HEADS_PER_GROUP + hh, bkt] * LOG2E, acc)
        o_ref[0, hh * ATTN_BLOCK:(hh + 1) * ATTN_BLOCK, :] = acc


def _dilated_bias(rel_bias):
    A = ATTN_BLOCK
    m_off = np.arange(A)[:, None] + A - np.arange(2 * A)[None, :]
    band = (m_off >= 0) & (m_off <= A)
    max_exact = NUM_BUCKETS // 2
    buckets = []
    for _, dilation in DILATED_GROUPS:
        dist = np.clip(m_off, 0, A) * dilation
        large = max_exact + (np.log(np.maximum(dist, 1).astype(np.float32) / max_exact)
                             / math.log(MAX_DISTANCE / max_exact) * (NUM_BUCKETS - max_exact)).astype(np.int32)
        bucket = np.where(dist < max_exact, dist, np.minimum(large, NUM_BUCKETS - 1))
        buckets.append(np.where(band, bucket, -1))
    buckets = jnp.asarray(np.stack(buckets), jnp.int32)
    n_groups = len(DILATED_GROUPS)
    return pl.pallas_call(
        _bias_body,
        grid=(n_groups,),
        in_specs=[pl.BlockSpec(memory_space=pltpu.SMEM), pl.BlockSpec((1, A, 2 * A), lambda g: (g, 0, 0))],
        out_specs=pl.BlockSpec((1, HEADS_PER_GROUP * A, 2 * A), lambda g: (g, 0, 0)),
        out_shape=jax.ShapeDtypeStruct((n_groups, HEADS_PER_GROUP * A, 2 * A), F32),
        compiler_params=_compiler_params(1),
        name="dilated_bias",
    )(rel_bias.astype(F32).T, buckets)


def _mixout_body(x_ref, o0_ref, o1_ref, o2_ref, l0_ref, l1_ref, l2_ref, qm_ref, mk_ref, mv_ref, wout_ref,
                 gain_ref, wg_ref, wu_ref, wd_ref, out_ref, hcat_scr, h_scr, acc_scr):
    wide = lambda ref: jnp.concatenate([ref[0, sl] for sl in range(SLABS)], axis=-1)
    lses = [wide(l0_ref), wide(l1_ref), wide(l2_ref)]
    mx = jnp.maximum(jnp.maximum(lses[0], lses[1]), lses[2])
    es = [jnp.exp(l - mx) for l in lses]
    inv = 1.0 / (es[0] + es[1] + es[2])
    for g, o_ref in enumerate((o0_ref, o1_ref, o2_ref)):
        hcat_scr[:, g * GROUP_WIDTH:(g + 1) * GROUP_WIDTH] = (wide(o_ref) * (es[g] * inv)).astype(BF16)
    hcat_scr[:, DIL_WIDTH:] = _memory_attention(qm_ref[0], mk_ref[0], mv_ref[0]).astype(BF16)
    mixed = x_ref[0] + _dot(hcat_scr[...], wout_ref[...])
    out_ref[0] = _ffn_halfstep(mixed, gain_ref, wg_ref, wu_ref, wd_ref, h_scr, acc_scr)


def _mixout_ffn(x, outs, lses, q_mem, mem_k, mem_v, w_out, ffn_gain, w_gate, w_up, w_down, layer, *, ts):
    b, s, d = x.shape
    f = w_gate.shape[2]
    tile = lambda w: pl.BlockSpec((1, ts, w), lambda i, j: (i, j, 0))
    slab_tile = pl.BlockSpec((1, SLABS, ts, V7X_LANES), lambda i, j: (i, 0, j, 0))
    mem_spec = pl.BlockSpec((1,) + mem_k.shape[1:], lambda i, j: (i, 0, 0))
    return pl.pallas_call(
        _mixout_body,
        grid=(b, s // ts),
        in_specs=([tile(d)] + [slab_tile] * 6 + [tile(MEM_WIDTH), mem_spec, mem_spec, _resident(w_out.shape)]
                  + _ffn_weight_specs(d, f, layer)),
        out_specs=tile(d),
        out_shape=jax.ShapeDtypeStruct((b, s, d), F32),
        scratch_shapes=[pltpu.VMEM((ts, DIL_WIDTH + MEM_WIDTH), BF16), pltpu.VMEM((ts, d), BF16),
                        pltpu.VMEM((ts, d), F32)],
        compiler_params=_compiler_params(2),
        name="dilated_mixer_out_ffn",
    )(x, *outs, *lses, q_mem, mem_k, mem_v, w_out.astype(BF16), ffn_gain.reshape(1, d), w_gate, w_up, w_down)


def kernel(x, mem, ffn1_norm, ffn1_w_gate, ffn1_w_up, ffn1_w_down, ffn2_norm, ffn2_w_gate, ffn2_w_up, ffn2_w_down, mix_norm, mem_norm, w_mem_kv, mem_q_norm, mem_k_norm, a_w_in, a_conv, a_gate_bias, a_h_norm, a_w_out, b_w_q, b_q_norm, b_w_out, kv_norm, w_kv, kv_k_norm, rel_bias):
    b, s, d = x.shape
    n = b * s
    tm = 1024
    head_id = np.arange(GROUP_WIDTH) // HEAD_DIM
    head_avg = jnp.asarray((head_id[:, None] == head_id[None, :]) / HEAD_DIM, BF16)
    mlstm_chunk = V7X_MXU_DIM
    ltri = jnp.asarray(np.tril(np.ones((mlstm_chunk, mlstm_chunk))), BF16)
    flat = lambda t: t.reshape(n, t.shape[-1])
    ffn1_w_gate, ffn1_w_up, ffn1_w_down, ffn2_w_gate, ffn2_w_up, ffn2_w_down = (
        w.astype(BF16) for w in (ffn1_w_gate, ffn1_w_up, ffn1_w_down, ffn2_w_gate, ffn2_w_up, ffn2_w_down))
    ffn = lambda t, norm, wg, wu, wd, layer: _ffn(flat(t), norm[layer], wg, wu, wd, layer,
                                                   tm=tm).reshape(b, s, d)

    x = ffn(x, ffn1_norm, ffn1_w_gate, ffn1_w_up, ffn1_w_down, 0)
    mem_k, mem_v = _memkv(mem, mem_norm[0], w_mem_kv[0], mem_k_norm[0], head_avg)
    x = _mlstm_layer(x, mix_norm[0], a_w_in[0], a_conv[0], a_gate_bias[0], a_h_norm[0], a_w_out[0],
                     mem_k, mem_v, mem_q_norm[0], head_avg, ltri, ts=mlstm_chunk)
    x = ffn(x, ffn2_norm, ffn2_w_gate, ffn2_w_up, ffn2_w_down, 0)

    k_sh, v_sh = _proj(x, kv_norm, w_kv, jnp.tile(kv_k_norm, DIL_WIDTH // HEAD_DIM), head_avg,
                       [(DIL_WIDTH, True), (DIL_WIDTH, True)], DIL_WIDTH, tm=tm, name="shared_kv_proj")

    x = ffn(x, ffn1_norm, ffn1_w_gate, ffn1_w_up, ffn1_w_down, 1)
    q_gain = jnp.concatenate([jnp.tile(b_q_norm[0], DIL_WIDTH // HEAD_DIM),
                              jnp.tile(mem_q_norm[1], MEM_HEADS)]) * Q_SCALE
    q_d, q_mem = _proj(x, mix_norm[1], b_w_q[0], q_gain, head_avg, [(DIL_WIDTH, True), (MEM_WIDTH, False)],
                       DIL_WIDTH + MEM_WIDTH, tm=tm, name="dilated_q_proj")
    mem_k, mem_v = _memkv(mem, mem_norm[1], w_mem_kv[1], mem_k_norm[1], head_avg)
    bias = _dilated_bias(rel_bias)
    outs, lses = [], []
    for g, (window, dilation) in enumerate(DILATED_GROUPS):
        attn_tile = window if dilation > PRE_STRIDE else max(window, math.gcd(s, 4096))
        assert window // dilation == ATTN_BLOCK and s % attn_tile == 0 and attn_tile % window == 0
        o_g, l_g = _dilated_attention(q_d, k_sh, v_sh, bias, g, dilation, tile=attn_tile)
        outs.append(o_g)
        lses.append(l_g)
    return _mixout_ffn(x, outs, lses, q_mem, mem_k, mem_v, b_w_out[0], ffn2_norm[1], ffn2_w_gate, ffn2_w_up,
                       ffn2_w_down, 1, ts=512)
```
